```python
import math
import jax, jax.numpy as jnp
from jax import lax
import numpy as np


D_MODEL = 2048
BATCH = 2
SEQ = 4096
DEPTH = 2
DEC_BATCH = 32
DEC_SEQ = 32
PAST_LEN = 4096

CHUNK = 64
N_EVEN = (DEPTH + 1) // 2
N_ODD = DEPTH // 2
A_WIDTH = D_MODEL // 2
A_HEADS = 8
A_HEAD_DIM = A_WIDTH // A_HEADS
A_BLOCK = 128
B_WIDTH = D_MODEL - A_WIDTH
B_HEADS = 4
B_KEY_DIM = 128
B_KEY_WIDTH = B_HEADS * B_KEY_DIM
B_VAL_DIM = B_WIDTH // B_HEADS
B_GATE_RANK = 16
B_GATE_TAU = 16.0
IN_EVEN = 2 * A_WIDTH + 2 * B_KEY_WIDTH + 2 * B_WIDTH + B_GATE_RANK
C_WIDTH = D_MODEL // 2
POOL_WINDOWS = (2, 4, 8, 16)
C_GROUPS = 4
C_GROUP_DIM = C_WIDTH // C_GROUPS
POOL_BUF = max(POOL_WINDOWS) - 1
D_WIDTH = D_MODEL - C_WIDTH
S5_GROUP_DIM = 16
S5_GROUPS = D_WIDTH // S5_GROUP_DIM
S5_STATE = 64
IN_ODD = C_WIDTH + D_WIDTH
D_FF = 5632
CONV_W = 3
EPS = 1e-6

kernel_name = "hybrid_streaming_encoder_step"


def rms_norm(x, g):
    xf = x.astype(jnp.float32)
    y = xf * lax.rsqrt(jnp.mean(xf * xf, axis=-1, keepdims=True) + EPS)
    return (y * g.astype(jnp.float32)).astype(x.dtype)


def layer_norm(x, g):
    xf = x.astype(jnp.float32)
    mu = jnp.mean(xf, axis=-1, keepdims=True)
    xc = xf - mu
    y = xc * lax.rsqrt(jnp.mean(xc * xc, axis=-1, keepdims=True) + EPS)
    return (y * g.astype(jnp.float32)).astype(x.dtype)


def block_causal_mask(n):
    i = jnp.arange(n)
    return (i[None, :] // CHUNK) <= (i[:, None] // CHUNK)


def spatial_gate(u, v, w_s, b_s):
    L = u.shape[2]
    w = jnp.where(block_causal_mask(L)[None], w_s[:, :L, :L], 0.0)
    s = jnp.einsum('hij,bnjhd->bnihd', w, v) + b_s[:, :L].T[None, None, :, :, None]
    return u * s


def gla_chunks(q, k, v, log_g, s0):
    def step(s, inp):
        qc, kc, vc, gc = inp
        cum = jnp.cumsum(gc, axis=1)
        tot = cum[:, -1]
        kd = kc * jnp.exp(tot[:, None] - cum)
        s = jnp.exp(tot)[..., None] * s + jnp.einsum('blhk,blhv->bhkv', kd, vc)
        o = jnp.einsum('blhk,bhkv->blhv', qc, s)
        return s, o
    xs = (jnp.moveaxis(q, 1, 0), jnp.moveaxis(k, 1, 0), jnp.moveaxis(v, 1, 0), jnp.moveaxis(log_g, 1, 0))
    s, o = lax.scan(step, s0, xs)
    return jnp.moveaxis(o, 0, 1), s


def even_mixer(h, s0, w_in, w_s, b_s, v_gain, w_gate, gate_bias, o_gain, w_out, a_len, b_len):
    bsz, t, _ = h.shape
    f32 = jnp.float32
    proj = h @ w_in
    o1 = A_WIDTH; o2 = o1 + A_WIDTH; o3 = o2 + B_KEY_WIDTH; o4 = o3 + B_KEY_WIDTH
    o5 = o4 + B_WIDTH; o6 = o5 + B_WIDTH
    a_u, a_v, q, k, v, r, g_lr = jnp.split(proj, [o1, o2, o3, o4, o5, o6], axis=-1)
    a_u = jax.nn.gelu(a_u)
    a_v = layer_norm(jax.nn.gelu(a_v), v_gain)
    shp = (bsz, t // a_len, a_len, A_HEADS, A_HEAD_DIM)
    a_out = spatial_gate(a_u.reshape(shp), a_v.reshape(shp), w_s, b_s).reshape(bsz, t, A_WIDTH)
    log_g = jax.nn.log_sigmoid((g_lr @ w_gate + gate_bias).astype(f32)) / B_GATE_TAU
    nc = t // b_len
    def heads(z, d):
        return z.astype(f32).reshape(bsz, nc, b_len, B_HEADS, d)
    o, s_new = gla_chunks(heads(q, B_KEY_DIM) * (B_KEY_DIM ** -0.5), heads(k, B_KEY_DIM),
                          heads(v, B_VAL_DIM), heads(log_g, B_KEY_DIM), s0.astype(f32))
    o = o * lax.rsqrt(jnp.mean(o * o, axis=-1, keepdims=True) + EPS)
    o = o.reshape(bsz, t, B_WIDTH) * o_gain.astype(f32) * jax.nn.silu(r.astype(f32))
    y = jnp.concatenate([a_out, o.astype(h.dtype)], axis=-1) @ w_out
    return y, s_new.astype(s0.dtype), a_v


def pool_mix(c, buf, pos0, c_map, c_scale):
    bsz, t, _ = c.shape
    f32 = jnp.float32
    xp = jnp.concatenate([buf.astype(c.dtype), c], axis=1)
    new_buf = xp[:, -POOL_BUF:]
    xf = xp.astype(f32)
    cs = jnp.concatenate([jnp.zeros((bsz, 1, C_WIDTH), f32), jnp.cumsum(xf, axis=1)], axis=1)
    pos = pos0 + jnp.arange(t)
    groups = []
    for g, w in enumerate(POOL_WINDOWS):
        lo, hi = g * C_GROUP_DIM, (g + 1) * C_GROUP_DIM
        total = cs[:, POOL_BUF + 1:POOL_BUF + 1 + t, lo:hi] - cs[:, POOL_BUF + 1 - w:POOL_BUF + 1 - w + t, lo:hi]
        cnt = jnp.minimum(pos + 1, w).astype(f32)
        groups.append(total / cnt[None, :, None])
    pooled = jnp.stack(groups, axis=2)
    delta = pooled - xf[:, POOL_BUF:].reshape(bsz, t, C_GROUPS, C_GROUP_DIM)
    y = jnp.einsum('btgc,gcd->btgd', delta, c_map.astype(f32)).reshape(bsz, t, C_WIDTH) * c_scale.astype(f32)
    return y.astype(c.dtype), new_buf


def s5_mix(u, s_re0, s_im0, a_re, a_im, log_dt, b_re, b_im, c_re, c_im, d_skip, w_glu):
    bsz, t, _ = u.shape
    f32 = jnp.float32
    ug = u.astype(f32).reshape(bsz, t, S5_GROUPS, S5_GROUP_DIM)
    lam = lax.complex(a_re.astype(f32), a_im.astype(f32))
    dt = jnp.exp(log_dt.astype(f32))[:, None]
    lam_bar = jnp.exp(lam * dt)
    b_bar = ((lam_bar - 1.0) / lam)[..., None] * lax.complex(b_re.astype(f32), b_im.astype(f32))
    bu = lax.complex(jnp.einsum('btgh,gph->btgp', ug, jnp.real(b_bar)),
                     jnp.einsum('btgh,gph->btgp', ug, jnp.imag(b_bar)))
    a_seq = jnp.broadcast_to(lam_bar, bu.shape)
    def combine(left, right):
        a1, x1 = left
        a2, x2 = right
        return a1 * a2, a2 * x1 + x2
    a_cum, s = lax.associative_scan(combine, (a_seq, bu), axis=1)
    s0 = lax.complex(s_re0.astype(f32), s_im0.astype(f32))
    s = s + a_cum * s0[:, None]
    y = (jnp.einsum('btgp,ghp->btgh', jnp.real(s), c_re.astype(f32))
         - jnp.einsum('btgp,ghp->btgh', jnp.imag(s), c_im.astype(f32))
         + d_skip.astype(f32).reshape(S5_GROUPS, S5_GROUP_DIM) * ug)
    y = jax.nn.gelu(y).reshape(bsz, t, D_WIDTH).astype(u.dtype)
    g_a, g_b = jnp.split(y @ w_glu, 2, axis=-1)
    out = g_a * jax.nn.sigmoid(g_b)
    s_last = s[:, -1]
    return out, jnp.real(s_last).astype(s_re0.dtype), jnp.imag(s_last).astype(s_im0.dtype)


def odd_mixer(h, pool_buf, s_re0, s_im0, pos0, w_in, c_map, c_scale, a_re, a_im, log_dt,
              b_re, b_im, c_re, c_im, d_skip, w_glu, w_out):
    proj = h @ w_in
    c_in, d_in = jnp.split(proj, [C_WIDTH], axis=-1)
    c_out, pool_new = pool_mix(c_in, pool_buf, pos0, c_map, c_scale)
    d_out, re_new, im_new = s5_mix(d_in, s_re0, s_im0, a_re, a_im, log_dt, b_re, b_im, c_re, c_im, d_skip, w_glu)
    y = jnp.concatenate([c_out, d_out], axis=-1) @ w_out
    return y, pool_new, re_new, im_new


def conv_ffn(h, buf, w_up, conv_w, conv_b, w_down):
    t = h.shape[1]
    gate, val = jnp.split(h @ w_up, 2, axis=-1)
    gp = jnp.concatenate([buf.astype(gate.dtype), gate], axis=1)
    conv = conv_b
    for k in range(CONV_W):
        conv = conv + conv_w[k] * gp[:, k:k + t]
    out = (jax.nn.gelu(conv) * val) @ w_down
    return out, gp[:, -(CONV_W - 1):]


def trunk(x, gla0, pool0, s5re0, s5im0, ffn0, pos0, a_len, b_len, p):
    new_gla, new_av, new_pool, new_re, new_im, new_ffn = [], [], [], [], [], []
    for layer in range(DEPTH):
        i = layer // 2
        h = rms_norm(x, p['norm_mix_pre'][layer])
        if layer % 2 == 0:
            m, s_new, av = even_mixer(h, gla0[i], p['w_in_even'][i], p['a_w_s'][i], p['a_b_s'][i],
                                      p['a_v_norm'][i], p['b_w_gate'][i], p['b_gate_bias'][i],
                                      p['b_out_norm'][i], p['w_out_even'][i], a_len, b_len)
            new_gla.append(s_new)
            new_av.append(av)
        else:
            m, pb, sre, sim = odd_mixer(h, pool0[i], s5re0[i], s5im0[i], pos0, p['w_in_odd'][i],
                                        p['c_map'][i], p['c_scale'][i], p['s5_a_re'][i], p['s5_a_im'][i],
                                        p['s5_log_dt'][i], p['s5_b_re'][i], p['s5_b_im'][i],
                                        p['s5_c_re'][i], p['s5_c_im'][i], p['s5_d'][i],
                                        p['s5_w_glu'][i], p['w_out_odd'][i])
            new_pool.append(pb)
            new_re.append(sre)
            new_im.append(sim)
        x = x + rms_norm(m, p['norm_mix_post'][layer])
        h = rms_norm(x, p['norm_ffn_pre'][layer])
        f, fb = conv_ffn(h, ffn0[layer], p['ffn_w_up'][layer], p['ffn_conv_w'][layer],
                         p['ffn_conv_b'][layer], p['ffn_w_down'][layer])
        new_ffn.append(fb)
        x = x + rms_norm(f, p['norm_ffn_post'][layer])
    return (x, jnp.stack(new_gla), jnp.stack(new_av), jnp.stack(new_pool),
            jnp.stack(new_re), jnp.stack(new_im), jnp.stack(new_ffn))


def setup_inputs(seed: int = 0) -> dict:
    key = jax.random.key(seed)
    keys = jax.random.split(key, 64)
    it = iter(range(64))

    def nrm(shape, scale):
        return scale * jax.random.normal(keys[next(it)], shape, jnp.float32)

    n_idx = jnp.arange(S5_STATE, dtype=jnp.float32)
    return {
        'x_prompt': nrm((BATCH, SEQ, D_MODEL), 1.0),
        'x_sample': nrm((DEC_BATCH, DEC_SEQ, D_MODEL), 1.0),
        'state_gla': nrm((N_EVEN, DEC_BATCH, B_HEADS, B_KEY_DIM, B_VAL_DIM), 1.0),
        'state_pool': nrm((N_ODD, DEC_BATCH, POOL_BUF, C_WIDTH), 1.0),
        'state_s5_re': nrm((N_ODD, DEC_BATCH, S5_GROUPS, S5_STATE), 0.3),
        'state_s5_im': nrm((N_ODD, DEC_BATCH, S5_GROUPS, S5_STATE), 0.3),
        'state_ffn_conv': nrm((DEPTH, DEC_BATCH, CONV_W - 1, D_FF), 1.0),
        'norm_mix_pre': 1.0 + nrm((DEPTH, D_MODEL), 0.05),
        'norm_mix_post': 1.0 + nrm((DEPTH, D_MODEL), 0.05),
        'norm_ffn_pre': 1.0 + nrm((DEPTH, D_MODEL), 0.05),
        'norm_ffn_post': 1.0 + nrm((DEPTH, D_MODEL), 0.05),
        'w_in_even': nrm((N_EVEN, D_MODEL, IN_EVEN), D_MODEL ** -0.5),
        'a_w_s': nrm((N_EVEN, A_HEADS, A_BLOCK, A_BLOCK), A_BLOCK ** -0.5),
        'a_b_s': 1.0 + nrm((N_EVEN, A_HEADS, A_BLOCK), 0.1),
        'a_v_norm': 1.0 + nrm((N_EVEN, A_WIDTH), 0.05),
        'b_w_gate': nrm((N_EVEN, B_GATE_RANK, B_KEY_WIDTH), B_GATE_RANK ** -0.5),
        'b_gate_bias': 2.0 + nrm((N_EVEN, B_KEY_WIDTH), 0.5),
        'b_out_norm': 1.0 + nrm((N_EVEN, B_WIDTH), 0.05),
        'w_out_even': nrm((N_EVEN, A_WIDTH + B_WIDTH, D_MODEL), (A_WIDTH + B_WIDTH) ** -0.5),
        'w_in_odd': nrm((N_ODD, D_MODEL, IN_ODD), D_MODEL ** -0.5),
        'c_map': nrm((N_ODD, C_GROUPS, C_GROUP_DIM, C_GROUP_DIM), C_GROUP_DIM ** -0.5),
        'c_scale': 1.0 + nrm((N_ODD, C_WIDTH), 0.1),
        's5_a_re': -0.5 + nrm((N_ODD, S5_GROUPS, S5_STATE), 0.01),
        's5_a_im': math.pi * n_idx + nrm((N_ODD, S5_GROUPS, S5_STATE), 0.01),
        's5_log_dt': jax.random.uniform(keys[next(it)], (N_ODD, S5_GROUPS), jnp.float32,
                                        minval=math.log(1e-3), maxval=math.log(1e-1)),
        's5_b_re': nrm((N_ODD, S5_GROUPS, S5_STATE, S5_GROUP_DIM), (2 * S5_GROUP_DIM) ** -0.5),
        's5_b_im': nrm((N_ODD, S5_GROUPS, S5_STATE, S5_GROUP_DIM), (2 * S5_GROUP_DIM) ** -0.5),
        's5_c_re': nrm((N_ODD, S5_GROUPS, S5_GROUP_DIM, S5_STATE), 0.5),
        's5_c_im': nrm((N_ODD, S5_GROUPS, S5_GROUP_DIM, S5_STATE), 0.5),
        's5_d': nrm((N_ODD, D_WIDTH), 0.5),
        's5_w_glu': nrm((N_ODD, D_WIDTH, 2 * D_WIDTH), D_WIDTH ** -0.5),
        'w_out_odd': nrm((N_ODD, C_WIDTH + D_WIDTH, D_MODEL), (C_WIDTH + D_WIDTH) ** -0.5),
        'ffn_w_up': nrm((DEPTH, D_MODEL, 2 * D_FF), D_MODEL ** -0.5),
        'ffn_conv_w': nrm((DEPTH, CONV_W, D_FF), CONV_W ** -0.5),
        'ffn_conv_b': nrm((DEPTH, D_FF), 0.02),
        'ffn_w_down': nrm((DEPTH, D_FF, D_MODEL), D_FF ** -0.5),
    }


def reference(x_prompt, x_sample, state_gla, state_pool, state_s5_re, state_s5_im, state_ffn_conv,
              norm_mix_pre, norm_mix_post, norm_ffn_pre, norm_ffn_post,
              w_in_even, a_w_s, a_b_s, a_v_norm, b_w_gate, b_gate_bias, b_out_norm, w_out_even,
              w_in_odd, c_map, c_scale, s5_a_re, s5_a_im, s5_log_dt, s5_b_re, s5_b_im, s5_c_re, s5_c_im,
              s5_d, s5_w_glu, w_out_odd, ffn_w_up, ffn_conv_w, ffn_conv_b, ffn_w_down):
    p = dict(norm_mix_pre=norm_mix_pre, norm_mix_post=norm_mix_post, norm_ffn_pre=norm_ffn_pre,
             norm_ffn_post=norm_ffn_post, w_in_even=w_in_even, a_w_s=a_w_s, a_b_s=a_b_s,
             a_v_norm=a_v_norm, b_w_gate=b_w_gate, b_gate_bias=b_gate_bias, b_out_norm=b_out_norm,
             w_out_even=w_out_even, w_in_odd=w_in_odd, c_map=c_map, c_scale=c_scale,
             s5_a_re=s5_a_re, s5_a_im=s5_a_im, s5_log_dt=s5_log_dt, s5_b_re=s5_b_re, s5_b_im=s5_b_im,
             s5_c_re=s5_c_re, s5_c_im=s5_c_im, s5_d=s5_d, s5_w_glu=s5_w_glu, w_out_odd=w_out_odd,
             ffn_w_up=ffn_w_up, ffn_conv_w=ffn_conv_w, ffn_conv_b=ffn_conv_b, ffn_w_down=ffn_w_down)
    bp = x_prompt.shape[0]
    dt = x_prompt.dtype
    (y_prompt, gla_prompt, _, pool_prompt, s5_re_prompt, s5_im_prompt, ffn_prompt) = trunk(
        x_prompt,
        jnp.zeros((N_EVEN, bp, B_HEADS, B_KEY_DIM, B_VAL_DIM), dt),
        jnp.zeros((N_ODD, bp, POOL_BUF, C_WIDTH), dt),
        jnp.zeros((N_ODD, bp, S5_GROUPS, S5_STATE), dt),
        jnp.zeros((N_ODD, bp, S5_GROUPS, S5_STATE), dt),
        jnp.zeros((DEPTH, bp, CONV_W - 1, D_FF), dt),
        0, A_BLOCK, CHUNK, p)
    t_s = x_sample.shape[1]
    (y_sample, gla_sample, av_sample, pool_sample, s5_re_sample, s5_im_sample, ffn_sample) = trunk(
        x_sample, state_gla, state_pool, state_s5_re, state_s5_im, state_ffn_conv,
        PAST_LEN, t_s, t_s, p)
    return (y_prompt, y_sample, gla_prompt, gla_sample, av_sample, pool_prompt, pool_sample,
            s5_re_prompt, s5_im_prompt, s5_re_sample, s5_im_sample, ffn_prompt, ffn_sample)
```

```python
import functools
import math

import jax
import jax.numpy as jnp
from jax import lax
from jax.experimental import pallas as pl
from jax.experimental.pallas import tpu as pltpu

F32 = jnp.float32
BF16 = jnp.bfloat16

D_MODEL = 2048
SEQ = 4096
DEC_BATCH = 32
DEC_SEQ = 32
PAST_LEN = 4096
CHUNK = 64
A_WIDTH = 1024
A_HEADS = 8
A_BLOCK = 128
B_HEADS = 4
B_KEY_DIM = 128
B_KEY_WIDTH = 512
B_VAL_DIM = 256
B_WIDTH = 1024
B_GATE_RANK = 16
B_GATE_TAU = 16.0
C_WIDTH = 1024
C_GROUP_DIM = 256
POOL_WINDOWS = (2, 4, 8, 16)
POOL_BUF = 15
D_WIDTH = 1024
S5_GROUPS = 64
S5_GROUP_DIM = 16
S5_STATE = 64
S5_CH = S5_GROUPS * S5_STATE
D_FF = 5632
EPS = 1e-6

LANES = 128
SUBLANES = 8
VMEM_LIMIT = 56 * 1024 * 1024


def _params(sem):
    return pltpu.CompilerParams(dimension_semantics=sem, vmem_limit_bytes=VMEM_LIMIT)


def _rms(x, g):
    return x * lax.rsqrt(jnp.mean(x * x, axis=-1, keepdims=True) + EPS) * g


def _dot(a, b):
    return jnp.dot(a, b, preferred_element_type=F32)


def _inproj_kernel(x_ref, g_ref, w_ref, *rest, with_gate):
    if with_gate:
        wlr_ref, wgate_ref, gbias_ref, o_ref, lg_ref, h_ref = rest
    else:
        o_ref, h_ref = rest
    j = pl.program_id(1)

    @pl.when(j == 0)
    def _():
        hb = _rms(x_ref[...], g_ref[...]).astype(BF16)
        h_ref[...] = hb
        if with_gate:
            glr = _dot(hb, wlr_ref[...])
            z = _dot(glr.astype(BF16), wgate_ref[...]) + gbias_ref[...]
            lg_ref[...] = (jnp.minimum(z, 0.0) - jnp.log(1.0 + jnp.exp(-jnp.abs(z)))) * (1.0 / B_GATE_TAU)

    o_ref[...] = _dot(h_ref[...], w_ref[...])


def _inproj(x, g, w, gate=None, *, tm, tn):
    m, d = x.shape
    n = w.shape[1]
    grid = (m // tm, n // tn)
    in_specs = [pl.BlockSpec((tm, d), lambda i, j: (i, 0)),
                pl.BlockSpec((1, d), lambda i, j: (0, 0)),
                pl.BlockSpec((d, tn), lambda i, j: (0, j))]
    out_shape = [jax.ShapeDtypeStruct((m, n), F32)]
    out_specs = [pl.BlockSpec((tm, tn), lambda i, j: (i, j))]
    args = [x, g, w]
    if gate is not None:
        wlr, wgate, gbias = gate
        in_specs += [pl.BlockSpec(wlr.shape, lambda i, j: (0, 0)),
                     pl.BlockSpec(wgate.shape, lambda i, j: (0, 0)),
                     pl.BlockSpec(gbias.shape, lambda i, j: (0, 0))]
        out_shape.append(jax.ShapeDtypeStruct((m, B_KEY_WIDTH), F32))
        out_specs.append(pl.BlockSpec((tm, B_KEY_WIDTH), lambda i, j: (i, 0)))
        args += [wlr, wgate, gbias]
    res = pl.pallas_call(
        functools.partial(_inproj_kernel, with_gate=gate is not None),
        out_shape=out_shape, grid=grid, in_specs=in_specs, out_specs=out_specs,
        scratch_shapes=[pltpu.VMEM((tm, d), BF16)],
        compiler_params=_params(("parallel", "arbitrary")),
        name="inproj_gate" if gate is not None else "inproj",
    )(*args)
    return res if gate is not None else res[0]


def _sgu_kernel(u_ref, v_ref, gain_ref, w_ref, b_ref, o_ref, av_ref, *, nblk):
    for n in range(nblk):
        rows = slice(n * A_BLOCK, (n + 1) * A_BLOCK)
        v = jax.nn.gelu(v_ref[rows, :])
        mu = jnp.mean(v, axis=-1, keepdims=True)
        vc = v - mu
        vn = vc * lax.rsqrt(jnp.mean(vc * vc, axis=-1, keepdims=True) + EPS) * gain_ref[...]
        av_ref[rows, :] = vn
        vb = vn.astype(BF16)
        for h in range(A_HEADS):
            cols = slice(h * LANES, (h + 1) * LANES)
            s = _dot(w_ref[h], vb[:, cols]) + b_ref[:, h:h + 1]
            o_ref[rows, cols] = (jax.nn.gelu(u_ref[rows, cols]) * s).astype(o_ref.dtype)


def _sgu(proj, gain, w, b, *, nblk):
    m = proj.shape[0]
    tm = nblk * A_BLOCK
    return pl.pallas_call(
        functools.partial(_sgu_kernel, nblk=nblk),
        out_shape=[jax.ShapeDtypeStruct((m, A_WIDTH), BF16), jax.ShapeDtypeStruct((m, A_WIDTH), F32)],
        grid=(m // tm,),
        in_specs=[pl.BlockSpec((tm, A_WIDTH), lambda i: (i, 0)),
                  pl.BlockSpec((tm, A_WIDTH), lambda i: (i, 1)),
                  pl.BlockSpec((1, A_WIDTH), lambda i: (0, 0)),
                  pl.BlockSpec(w.shape, lambda i: (0, 0, 0)),
                  pl.BlockSpec(b.shape, lambda i: (0, 0))],
        out_specs=[pl.BlockSpec((tm, A_WIDTH), lambda i: (i, 0)),
                   pl.BlockSpec((tm, A_WIDTH), lambda i: (i, 0))],
        compiler_params=_params(("parallel",)),
        name="sgu",
    )(proj, proj, gain, w, b)


def _gla_kernel(q_ref, k_ref, v_ref, r_ref, lg_ref, s0_ref, og_ref, o_ref, sout_ref, st_ref, *, clen, nch):
    c = pl.program_id(1)

    @pl.when(c == 0)
    def _():
        for h in range(B_HEADS):
            st_ref[h] = s0_ref[0, h].T

    row_i = lax.broadcasted_iota(jnp.int32, (clen, clen), 0)
    col_i = lax.broadcasted_iota(jnp.int32, (clen, clen), 1)
    tri = (row_i >= col_i).astype(F32)
    for n in range(nch):
        rows = slice(n * clen, (n + 1) * clen)
        for h in range(B_HEADS):
            kc = slice(h * B_KEY_DIM, (h + 1) * B_KEY_DIM)
            vc = slice(h * B_VAL_DIM, (h + 1) * B_VAL_DIM)
            cum = jnp.dot(tri, lg_ref[rows, kc], precision=lax.Precision.HIGHEST,
                          preferred_element_type=F32)
            tot = cum[clen - 1:clen, :]
            kd = (k_ref[rows, kc] * jnp.exp(tot - cum)).astype(BF16)
            upd = lax.dot_general(v_ref[rows, vc].astype(BF16), kd, (((0,), (0,)), ((), ())),
                                  preferred_element_type=F32)
            st = jnp.exp(tot) * st_ref[h] + upd
            st_ref[h] = st
            qh = (q_ref[rows, kc] * (B_KEY_DIM ** -0.5)).astype(BF16)
            o = lax.dot_general(qh, st.astype(BF16), (((1,), (1,)), ((), ())),
                                preferred_element_type=F32)
            o = o * lax.rsqrt(jnp.mean(o * o, axis=-1, keepdims=True) + EPS)
            o = o * og_ref[:, vc] * jax.nn.silu(r_ref[rows, vc])
            o_ref[rows, vc] = o.astype(o_ref.dtype)

    @pl.when(c == pl.num_programs(1) - 1)
    def _():
        for h in range(B_HEADS):
            sout_ref[0, h] = st_ref[h].T


def _gla(proj, lg, s0, og, *, bsz, t, clen, nch):
    rows = clen * nch
    steps = t // rows
    m = bsz * t
    row_map = lambda col: (lambda b, c: (b * steps + c, col))
    return pl.pallas_call(
        functools.partial(_gla_kernel, clen=clen, nch=nch),
        out_shape=[jax.ShapeDtypeStruct((m, B_WIDTH), BF16),
                   jax.ShapeDtypeStruct((bsz, B_HEADS, B_KEY_DIM, B_VAL_DIM), F32)],
        grid=(bsz, steps),
        in_specs=[pl.BlockSpec((rows, B_KEY_WIDTH), row_map(4)),
                  pl.BlockSpec((rows, B_KEY_WIDTH), row_map(5)),
                  pl.BlockSpec((rows, B_WIDTH), row_map(3)),
                  pl.BlockSpec((rows, B_WIDTH), row_map(4)),
                  pl.BlockSpec((rows, B_KEY_WIDTH), row_map(0)),
                  pl.BlockSpec((1, B_HEADS, B_KEY_DIM, B_VAL_DIM), lambda b, c: (b, 0, 0, 0)),
                  pl.BlockSpec((1, B_WIDTH), lambda b, c: (0, 0))],
        out_specs=[pl.BlockSpec((rows, B_WIDTH), row_map(0)),
                   pl.BlockSpec((1, B_HEADS, B_KEY_DIM, B_VAL_DIM), lambda b, c: (b, 0, 0, 0))],
        scratch_shapes=[pltpu.VMEM((B_HEADS, B_VAL_DIM, B_KEY_DIM), F32)],
        compiler_params=_params(("parallel", "arbitrary")),
        name="gla",
    )(proj, proj, proj, proj, lg, s0, og)


def _outproj_kernel(a_ref, b_ref, w_ref, x_ref, g_ref, o_ref):
    ka = a_ref.shape[1]
    y = _dot(a_ref[...], w_ref[0:ka, :]) + _dot(b_ref[...], w_ref[ka:, :])
    o_ref[...] = x_ref[...] + _rms(y, g_ref[...])


def _outproj(a, b, w, x, g, *, tm):
    m, d = x.shape
    return pl.pallas_call(
        _outproj_kernel,
        out_shape=jax.ShapeDtypeStruct((m, d), F32),
        grid=(m // tm,),
        in_specs=[pl.BlockSpec((tm, a.shape[1]), lambda i: (i, 0)),
                  pl.BlockSpec((tm, b.shape[1]), lambda i: (i, 0)),
                  pl.BlockSpec(w.shape, lambda i: (0, 0)),
                  pl.BlockSpec((tm, d), lambda i: (i, 0)),
                  pl.BlockSpec((1, d), lambda i: (0, 0))],
        out_specs=pl.BlockSpec((tm, d), lambda i: (i, 0)),
        compiler_params=_params(("parallel",)),
        name="outproj",
    )(a, b, w, x, g)


def _ffn_kernel(x_ref, gpre_ref, wg_ref, wv_ref, cw_ref, cb_ref, wd_ref, gpost_ref, st_ref,
                o_ref, tail_ref, h_ref, acc_ref, gext_ref, carry_ref, *, step, hist, tps, tm):
    i = pl.program_id(0)
    j = pl.program_id(1)

    @pl.when(j == 0)
    def _():
        h_ref[...] = _rms(x_ref[...], gpre_ref[...]).astype(BF16)
        acc_ref[...] = jnp.zeros_like(acc_ref)

    hb = h_ref[...]
    gate = _dot(hb, wg_ref[...])
    val = _dot(hb, wv_ref[...])
    first = (i % tps) == 0

    @pl.when(first)
    def _():
        gext_ref[0:hist, :] = st_ref[0]

    @pl.when(jnp.logical_not(first))
    def _():
        gext_ref[0:hist, :] = carry_ref[j]

    gext_ref[hist:hist + tm, :] = gate
    prev2 = gext_ref[hist - 2 * step:hist - 2 * step + tm, :]
    prev1 = gext_ref[hist - step:hist - step + tm, :]
    conv = cb_ref[...] + cw_ref[0:1, :] * prev2 + cw_ref[1:2, :] * prev1 + cw_ref[2:3, :] * gate
    act = (jax.nn.gelu(conv) * val).astype(BF16)
    acc_ref[...] += _dot(act, wd_ref[...])
    tail = gext_ref[tm:tm + hist, :]
    carry_ref[j] = tail
    tail_ref[0] = tail

    @pl.when(j == pl.num_programs(1) - 1)
    def _():
        o_ref[...] = x_ref[...] + _rms(acc_ref[...], gpost_ref[...])


def _ffn(x, gpre, w_up, cw, cb, w_down, gpost, state, *, step, hist, tps, tm, tf):
    m, d = x.shape
    nf = D_FF // tf
    nm = m // tm
    return pl.pallas_call(
        functools.partial(_ffn_kernel, step=step, hist=hist, tps=tps, tm=tm),
        out_shape=[jax.ShapeDtypeStruct((m, d), F32), jax.ShapeDtypeStruct((nm, hist, D_FF), F32)],
        grid=(nm, nf),
        in_specs=[pl.BlockSpec((tm, d), lambda i, j: (i, 0)),
                  pl.BlockSpec((1, d), lambda i, j: (0, 0)),
                  pl.BlockSpec((d, tf), lambda i, j: (0, j)),
                  pl.BlockSpec((d, tf), lambda i, j: (0, nf + j)),
                  pl.BlockSpec((3, tf), lambda i, j: (0, j)),
                  pl.BlockSpec((1, tf), lambda i, j: (0, j)),
                  pl.BlockSpec((tf, d), lambda i, j: (j, 0)),
                  pl.BlockSpec((1, d), lambda i, j: (0, 0)),
                  pl.BlockSpec((1, hist, tf), lambda i, j: (i // tps, 0, j))],
        out_specs=[pl.BlockSpec((tm, d), lambda i, j: (i, 0)),
                   pl.BlockSpec((1, hist, tf), lambda i, j: (i, 0, j))],
        scratch_shapes=[pltpu.VMEM((tm, d), BF16), pltpu.VMEM((tm, d), F32),
                        pltpu.VMEM((hist + tm, tf), F32), pltpu.VMEM((nf, hist, tf), F32)],
        compiler_params=_params(("arbitrary", "arbitrary")),
        name="ffn",
    )(x, gpre, w_up, w_up, cw, cb, w_down, gpost, state)


def _pool_kernel(c_ref, st_ref, cmap_ref, cs_ref, o_ref, tail_ref, ext_ref, carry_ref,
                 *, step, hist, tps, tm, pos0):
    i = pl.program_id(0)
    first = (i % tps) == 0

    @pl.when(first)
    def _():
        ext_ref[0:hist, :] = st_ref[0]

    @pl.when(jnp.logical_not(first))
    def _():
        ext_ref[0:hist, :] = carry_ref[...]

    ext_ref[hist:hist + tm, :] = c_ref[...]
    row = lax.broadcasted_iota(jnp.int32, (tm, 1), 0)
    if step > 1:
        row = lax.shift_right_logical(row, int(math.log2(step)))
    pos = pos0 + (i % tps) * (tm // step) + row
    for g, win in enumerate(POOL_WINDOWS):
        cols = slice(g * C_GROUP_DIM, (g + 1) * C_GROUP_DIM)
        cur = ext_ref[hist:hist + tm, cols]
        tot = cur
        for k in range(1, win):
            tot = tot + ext_ref[hist - k * step:hist - k * step + tm, cols]
        cnt = jnp.minimum(pos + 1, win).astype(F32)
        delta = tot / cnt - cur
        y = _dot(delta.astype(BF16), cmap_ref[g]) * cs_ref[:, cols]
        o_ref[:, cols] = y.astype(o_ref.dtype)
    tail = ext_ref[tm:tm + hist, :]
    carry_ref[...] = tail
    tail_ref[0] = tail


def _pool(proj, state, cmap, cscale, *, step, hist, tps, tm, pos0):
    m = proj.shape[0]
    nm = m // tm
    return pl.pallas_call(
        functools.partial(_pool_kernel, step=step, hist=hist, tps=tps, tm=tm, pos0=pos0),
        out_shape=[jax.ShapeDtypeStruct((m, C_WIDTH), BF16), jax.ShapeDtypeStruct((nm, hist, C_WIDTH), F32)],
        grid=(nm,),
        in_specs=[pl.BlockSpec((tm, C_WIDTH), lambda i: (i, 0)),
                  pl.BlockSpec((1, hist, C_WIDTH), lambda i: (i // tps, 0, 0)),
                  pl.BlockSpec(cmap.shape, lambda i: (0, 0, 0)),
                  pl.BlockSpec((1, C_WIDTH), lambda i: (0, 0))],
        out_specs=[pl.BlockSpec((tm, C_WIDTH), lambda i: (i, 0)),
                   pl.BlockSpec((1, hist, C_WIDTH), lambda i: (i, 0, 0))],
        scratch_shapes=[pltpu.VMEM((hist + tm, C_WIDTH), F32), pltpu.VMEM((hist, C_WIDTH), F32)],
        compiler_params=_params(("arbitrary",)),
        name="pool",
    )(proj, state, cmap, cscale)


S5_CHUNK = 512
S5_NCHUNK = S5_CH // S5_CHUNK


def _s5_kernel(u_ref, s0re_ref, s0im_ref, bblk_ref, cblk_ref, dskip_ref, wglu_ref, kc_ref,
               o_ref, tre_ref, tim_ref, sre_ref, sim_ref, cre_ref, cim_ref, *, step, tps, tm):
    i = pl.program_id(0)
    crow = cre_ref.shape[0]

    ub = u_ref[...].astype(BF16)
    for m in range(S5_NCHUNK):
        r = _dot(ub[:, m * LANES:(m + 1) * LANES], bblk_ref[m])
        sre_ref[:, m * S5_CHUNK:(m + 1) * S5_CHUNK] = r[:, :S5_CHUNK]
        sim_ref[:, m * S5_CHUNK:(m + 1) * S5_CHUNK] = r[:, S5_CHUNK:]

    @pl.when((i % tps) == 0)
    def _():
        cre_ref[...] = jnp.broadcast_to(s0re_ref[0], cre_ref.shape) if step == 1 else s0re_ref[0]
        cim_ref[...] = jnp.broadcast_to(s0im_ref[0], cim_ref.shape) if step == 1 else s0im_ref[0]

    for m in range(S5_NCHUNK):
        cols = slice(m * S5_CHUNK, (m + 1) * S5_CHUNK)
        if step == 1:
            consts = [kc_ref[k, :, cols] for k in range(8)]
            a1r, a1i, a2r, a2i, a4r, a4i, pwr, pwi = consts

            def body(rb, carry):
                cr, ci = carry
                r0 = pl.multiple_of(rb * SUBLANES, SUBLANES)
                xr = sre_ref[pl.ds(r0, SUBLANES), cols]
                xi = sim_ref[pl.ds(r0, SUBLANES), cols]
                for ar, ai, d in ((a1r, a1i, 1), (a2r, a2i, 2), (a4r, a4i, 4)):
                    sr = pltpu.roll(xr, d, 0)
                    si = pltpu.roll(xi, d, 0)
                    xr, xi = xr + (ar * sr - ai * si), xi + (ar * si + ai * sr)
                xr, xi = xr + (pwr * cr - pwi * ci), xi + (pwr * ci + pwi * cr)
                sre_ref[pl.ds(r0, SUBLANES), cols] = xr
                sim_ref[pl.ds(r0, SUBLANES), cols] = xi
                return (jnp.broadcast_to(xr[SUBLANES - 1:SUBLANES, :], xr.shape),
                        jnp.broadcast_to(xi[SUBLANES - 1:SUBLANES, :], xi.shape))

            cr, ci = lax.fori_loop(0, tm // SUBLANES, body, (cre_ref[:, cols], cim_ref[:, cols]))
        else:
            lr = jnp.broadcast_to(kc_ref[0, 0:1, cols], (crow, S5_CHUNK))
            li = jnp.broadcast_to(kc_ref[1, 0:1, cols], (crow, S5_CHUNK))

            def body(t, carry):
                cr, ci = carry
                r0 = pl.multiple_of(t * step, step)
                xr = sre_ref[pl.ds(r0, step), cols] + (lr * cr - li * ci)
                xi = sim_ref[pl.ds(r0, step), cols] + (lr * ci + li * cr)
                sre_ref[pl.ds(r0, step), cols] = xr
                sim_ref[pl.ds(r0, step), cols] = xi
                return xr, xi

            cr, ci = lax.fori_loop(0, tm // step, body, (cre_ref[:, cols], cim_ref[:, cols]))
        cre_ref[:, cols] = cr
        cim_ref[:, cols] = ci

    tre_ref[0] = cre_ref[...]
    tim_ref[0] = cim_ref[...]

    ys = []
    for m in range(S5_NCHUNK):
        cols = slice(m * S5_CHUNK, (m + 1) * S5_CHUNK)
        ys.append(_dot(sre_ref[:, cols].astype(BF16), cblk_ref[m, 0:S5_CHUNK, :])
                  + _dot(sim_ref[:, cols].astype(BF16), cblk_ref[m, S5_CHUNK:, :]))
    y = jnp.concatenate(ys, axis=1) + dskip_ref[...] * u_ref[...]
    z = _dot(jax.nn.gelu(y).astype(BF16), wglu_ref[...])
    o_ref[...] = (z[:, :D_WIDTH] * jax.nn.sigmoid(z[:, D_WIDTH:])).astype(o_ref.dtype)


def _s5(proj, s0re, s0im, bblk, cblk, dskip, wglu, kconst, *, step, tps, tm):
    m = proj.shape[0]
    nm = m // tm
    crow = s0re.shape[1] if step > 1 else SUBLANES
    srow = s0re.shape[1]
    return pl.pallas_call(
        functools.partial(_s5_kernel, step=step, tps=tps, tm=tm),
        out_shape=[jax.ShapeDtypeStruct((m, D_WIDTH), BF16),
                   jax.ShapeDtypeStruct((nm, crow, S5_CH), F32),
                   jax.ShapeDtypeStruct((nm, crow, S5_CH), F32)],
        grid=(nm,),
        in_specs=[pl.BlockSpec((tm, D_WIDTH), lambda i: (i, 1)),
                  pl.BlockSpec((1, srow, S5_CH), lambda i: (i // tps, 0, 0)),
                  pl.BlockSpec((1, srow, S5_CH), lambda i: (i // tps, 0, 0)),
                  pl.BlockSpec(bblk.shape, lambda i: (0, 0, 0)),
                  pl.BlockSpec(cblk.shape, lambda i: (0, 0, 0)),
                  pl.BlockSpec((1, D_WIDTH), lambda i: (0, 0)),
                  pl.BlockSpec(wglu.shape, lambda i: (0, 0)),
                  pl.BlockSpec(kconst.shape, lambda i: (0, 0, 0))],
        out_specs=[pl.BlockSpec((tm, D_WIDTH), lambda i: (i, 0)),
                   pl.BlockSpec((1, crow, S5_CH), lambda i: (i, 0, 0)),
                   pl.BlockSpec((1, crow, S5_CH), lambda i: (i, 0, 0))],
        scratch_shapes=[pltpu.VMEM((tm, S5_CH), F32), pltpu.VMEM((tm, S5_CH), F32),
                        pltpu.VMEM((crow, S5_CH), F32), pltpu.VMEM((crow, S5_CH), F32)],
        compiler_params=_params(("arbitrary",)),
        name="s5",
    )(proj, s0re, s0im, bblk, cblk, dskip, wglu, kconst)


def _block_diag(blocks, per):
    n, r, c = blocks.shape
    b = blocks.reshape(n // per, per, r, c)
    eye = jnp.eye(per, dtype=blocks.dtype)
    return jnp.einsum('mgrc,gh->mgrhc', b, eye).reshape(n // per, per * r, per * c)


def _s5_constants(a_re, a_im, log_dt, b_re, b_im, c_re, c_im):
    dt = jnp.exp(log_dt)[:, None]
    zr, zi = a_re * dt, a_im * dt
    mag = jnp.exp(zr)
    lr, li = mag * jnp.cos(zi), mag * jnp.sin(zi)
    den = a_re * a_re + a_im * a_im
    nr, ni = lr - 1.0, li
    kr, ki = (nr * a_re + ni * a_im) / den, (ni * a_re - nr * a_im) / den
    bbr = kr[..., None] * b_re - ki[..., None] * b_im
    bbi = kr[..., None] * b_im + ki[..., None] * b_re
    per = S5_CHUNK // S5_STATE
    bblk = jnp.concatenate([_block_diag(jnp.swapaxes(bbr, 1, 2), per),
                            _block_diag(jnp.swapaxes(bbi, 1, 2), per)], axis=2).astype(BF16)
    cblk = jnp.concatenate([_block_diag(jnp.swapaxes(c_re, 1, 2), per),
                            _block_diag(jnp.swapaxes(-c_im, 1, 2), per)], axis=1).astype(BF16)
    lr, li = lr.reshape(1, S5_CH), li.reshape(1, S5_CH)
    pr, pi = [lr], [li]
    for _ in range(SUBLANES - 1):
        pr, pi = pr + [pr[-1] * lr - pi[-1] * li], pi + [pr[-1] * li + pi[-1] * lr]
    rowid = jnp.arange(SUBLANES)[:, None]

    def masked(p, d):
        return jnp.where(rowid >= d, jnp.broadcast_to(p[d - 1], (SUBLANES, S5_CH)), 0.0)

    k_prompt = jnp.stack([masked(pr, 1), masked(pi, 1), masked(pr, 2), masked(pi, 2),
                          masked(pr, 4), masked(pi, 4),
                          jnp.concatenate(pr, axis=0), jnp.concatenate(pi, axis=0)])
    k_sample = jnp.stack([jnp.broadcast_to(lr, (SUBLANES, S5_CH)), jnp.broadcast_to(li, (SUBLANES, S5_CH))])
    return bblk, cblk, k_prompt, k_sample


def _time_major(a):
    a = jnp.swapaxes(a, 0, 1)
    return a.reshape((a.shape[0] * a.shape[1],) + a.shape[2:])


def kernel(x_prompt, x_sample, state_gla, state_pool, state_s5_re, state_s5_im, state_ffn_conv, norm_mix_pre, norm_mix_post, norm_ffn_pre, norm_ffn_post, w_in_even, a_w_s, a_b_s, a_v_norm, b_w_gate, b_gate_bias, b_out_norm, w_out_even, w_in_odd, c_map, c_scale, s5_a_re, s5_a_im, s5_log_dt, s5_b_re, s5_b_im, s5_c_re, s5_c_im, s5_d, s5_w_glu, w_out_odd, ffn_w_up, ffn_conv_w, ffn_conv_b, ffn_w_down):
    bp = x_prompt.shape[0]
    nb, ts = x_sample.shape[0], x_sample.shape[1]
    xp = x_prompt.reshape(bp * SEQ, D_MODEL)
    xs = x_sample.reshape(nb * ts, D_MODEL)

    row = lambda v: v.reshape(1, -1)
    n_main = 2 * A_WIDTH + 2 * B_KEY_WIDTH + 2 * B_WIDTH
    w_in0 = w_in_even[0][:, :n_main].astype(BF16)
    w_lr = jnp.pad(w_in_even[0][:, n_main:], ((0, 0), (0, LANES - B_GATE_RANK))).astype(BF16)
    w_gate = jnp.pad(b_w_gate[0], ((0, LANES - B_GATE_RANK), (0, 0))).astype(BF16)
    gate = (w_lr, w_gate, row(b_gate_bias[0]))
    pos = jnp.arange(A_BLOCK)
    causal = (pos[None, :] // CHUNK) <= (pos[:, None] // CHUNK)
    ws_prompt = jnp.where(causal[None], a_w_s[0], 0.0).astype(BF16)
    per = A_BLOCK // ts
    ws_small = jnp.where(causal[None, :ts, :ts], a_w_s[0][:, :ts, :ts], 0.0)
    ws_sample = jnp.einsum('hij,ab->haibj', ws_small, jnp.eye(per, dtype=F32)).reshape(A_HEADS, A_BLOCK, A_BLOCK).astype(BF16)
    bs_prompt = a_b_s[0].T
    bs_sample = jnp.tile(a_b_s[0][:, :ts].T, (per, 1))
    w_out0 = w_out_even[0].astype(BF16)
    w_in1 = w_in_odd[0].astype(BF16)
    cmap = c_map[0].astype(BF16)
    bblk, cblk, k_prompt, k_sample = _s5_constants(s5_a_re[0], s5_a_im[0], s5_log_dt[0], s5_b_re[0], s5_b_im[0],
                                                   s5_c_re[0], s5_c_im[0])
    wglu = s5_w_glu[0].astype(BF16)
    w_out1 = w_out_odd[0].astype(BF16)
    w_up = ffn_w_up.astype(BF16)
    w_down = ffn_w_down.astype(BF16)

    tm = 512
    tf = 512
    tps_p = SEQ // tm
    ffn_hist_p = SUBLANES
    pool_hist_p = 2 * SUBLANES
    step_s = nb
    tps_s = (nb * ts) // tm
    ffn_hist_s = 2 * step_s
    pool_hist_s = (POOL_BUF + 1) * step_s

    def ffn_layer(x, layer, state, *, step, hist, tps):
        return _ffn(x, row(norm_ffn_pre[layer]), w_up[layer], ffn_conv_w[layer], row(ffn_conv_b[layer]),
                    w_down[layer], row(norm_ffn_post[layer]), state, step=step, hist=hist, tps=tps, tm=tm, tf=tf)

    proj, lg = _inproj(xp, row(norm_mix_pre[0]), w_in0, gate, tm=tm, tn=1024)
    a_out, _ = _sgu(proj, row(a_v_norm[0]), ws_prompt, bs_prompt, nblk=2)
    b_out, gla_p = _gla(proj, lg, jnp.zeros((bp, B_HEADS, B_KEY_DIM, B_VAL_DIM), F32), row(b_out_norm[0]),
                        bsz=bp, t=SEQ, clen=CHUNK, nch=4)
    xp = _outproj(a_out, b_out, w_out0, xp, row(norm_mix_post[0]), tm=tm)
    xp, ffn0_p = ffn_layer(xp, 0, jnp.zeros((bp, ffn_hist_p, D_FF), F32), step=1, hist=ffn_hist_p, tps=tps_p)
    proj = _inproj(xp, row(norm_mix_pre[1]), w_in1, tm=tm, tn=1024)
    c_out, pool_tail_p = _pool(proj, jnp.zeros((bp, pool_hist_p, C_WIDTH), F32), cmap, row(c_scale[0]),
                               step=1, hist=pool_hist_p, tps=tps_p, tm=tm, pos0=0)
    zero_state = jnp.zeros((bp, 1, S5_CH), F32)
    d_out, s5re_tail_p, s5im_tail_p = _s5(proj, zero_state, zero_state, bblk, cblk, row(s5_d[0]), wglu, k_prompt,
                                          step=1, tps=tps_p, tm=tm)
    xp = _outproj(c_out, d_out, w_out1, xp, row(norm_mix_post[1]), tm=tm)
    xp, ffn1_p = ffn_layer(xp, 1, jnp.zeros((bp, ffn_hist_p, D_FF), F32), step=1, hist=ffn_hist_p, tps=tps_p)

    last = slice(tps_p - 1, None, tps_p)
    y_prompt = xp.reshape(bp, SEQ, D_MODEL)
    gla_prompt = gla_p[None]
    pool_prompt = pool_tail_p[last, pool_hist_p - POOL_BUF:][None]
    s5_re_prompt = s5re_tail_p[last, 0].reshape(1, bp, S5_GROUPS, S5_STATE)
    s5_im_prompt = s5im_tail_p[last, 0].reshape(1, bp, S5_GROUPS, S5_STATE)
    ffn_prompt = jnp.stack([ffn0_p[last, ffn_hist_p - 2:], ffn1_p[last, ffn_hist_p - 2:]])

    proj, lg = _inproj(xs, row(norm_mix_pre[0]), w_in0, gate, tm=tm, tn=1024)
    a_out, a_v = _sgu(proj, row(a_v_norm[0]), ws_sample, bs_sample, nblk=2)
    b_out, gla_s = _gla(proj, lg, state_gla[0], row(b_out_norm[0]), bsz=nb, t=ts, clen=ts, nch=1)
    xs = _outproj(a_out, b_out, w_out0, xs, row(norm_mix_post[0]), tm=tm)
    xs = _time_major(xs.reshape(nb, ts, D_MODEL))
    ffn_state = lambda layer: _time_major(state_ffn_conv[layer])[None]
    xs, ffn0_s = ffn_layer(xs, 0, ffn_state(0), step=step_s, hist=ffn_hist_s, tps=tps_s)
    proj = _inproj(xs, row(norm_mix_pre[1]), w_in1, tm=tm, tn=1024)
    pool_state = jnp.pad(_time_major(state_pool[0]), ((step_s, 0), (0, 0)))[None]
    c_out, pool_tail_s = _pool(proj, pool_state, cmap, row(c_scale[0]),
                               step=step_s, hist=pool_hist_s, tps=tps_s, tm=tm, pos0=PAST_LEN)
    d_out, s5re_tail_s, s5im_tail_s = _s5(proj, state_s5_re[0].reshape(1, nb, S5_CH),
                                          state_s5_im[0].reshape(1, nb, S5_CH),
                                          bblk, cblk, row(s5_d[0]), wglu, k_sample, step=step_s, tps=tps_s, tm=tm)
    xs = _outproj(c_out, d_out, w_out1, xs, row(norm_mix_post[1]), tm=tm)
    xs, ffn1_s = ffn_layer(xs, 1, ffn_state(1), step=step_s, hist=ffn_hist_s, tps=tps_s)

    def batch_major(a, nt):
        return jnp.swapaxes(a.reshape(nt, nb, a.shape[-1]), 0, 1)

    y_sample = batch_major(xs, ts)
    gla_sample = gla_s[None]
    av_sample = a_v.reshape(1, nb, ts, A_WIDTH)
    pool_sample = batch_major(pool_tail_s[-1, step_s:], POOL_BUF)[None]
    s5_re_sample = s5re_tail_s[-1].reshape(1, nb, S5_GROUPS, S5_STATE)
    s5_im_sample = s5im_tail_s[-1].reshape(1, nb, S5_GROUPS, S5_STATE)
    ffn_sample = jnp.stack([batch_major(ffn0_s[-1], 2), batch_major(ffn1_s[-1], 2)])

    return (y_prompt, y_sample, gla_prompt, gla_sample, av_sample, pool_prompt, pool_sample,
            s5_re_prompt, s5_im_prompt, s5_re_sample, s5_im_sample, ffn_prompt, ffn_sample)
```

```python
import functools
import math

import jax
import jax.numpy as jnp
from jax import lax
from jax.experimental import pallas as pl
from jax.experimental.pallas import tpu as pltpu

F32 = jnp.float32
BF16 = jnp.bfloat16

D_MODEL = 2048
SEQ = 4096
DEC_BATCH = 32
DEC_SEQ = 32
PAST_LEN = 4096
CHUNK = 64
A_WIDTH = 1024
A_HEADS = 8
A_BLOCK = 128
B_HEADS = 4
B_KEY_DIM = 128
B_KEY_WIDTH = 512
B_VAL_DIM = 256
B_WIDTH = 1024
B_GATE_RANK = 16
B_GATE_TAU = 16.0
C_WIDTH = 1024
C_GROUP_DIM = 256
POOL_WINDOWS = (2, 4, 8, 16)
POOL_BUF = 15
D_WIDTH = 1024
S5_GROUPS = 64
S5_GROUP_DIM = 16
S5_STATE = 64
S5_CH = S5_GROUPS * S5_STATE
D_FF = 5632
EPS = 1e-6

LANES = 128
SUBLANES = 8
VMEM_LIMIT = 56 * 1024 * 1024


def _params(sem):
    return pltpu.CompilerParams(dimension_semantics=sem, vmem_limit_bytes=VMEM_LIMIT)


def _rms(x, g):
    return x * lax.rsqrt(jnp.mean(x * x, axis=-1, keepdims=True) + EPS) * g


def _dot(a, b):
    return jnp.dot(a, b, preferred_element_type=F32)


def _inproj_kernel(x_ref, g_ref, w_ref, *rest, with_gate):
    if with_gate:
        wlr_ref, wgate_ref, gbias_ref, o_ref, lg_ref, h_ref = rest
    else:
        o_ref, h_ref = rest
    j = pl.program_id(1)

    @pl.when(j == 0)
    def _():
        hb = _rms(x_ref[...], g_ref[...]).astype(BF16)
        h_ref[...] = hb
        if with_gate:
            glr = _dot(hb, wlr_ref[...])
            z = _dot(glr.astype(BF16), wgate_ref[...]) + gbias_ref[...]
            lg_ref[...] = (jnp.minimum(z, 0.0) - jnp.log(1.0 + jnp.exp(-jnp.abs(z)))) * (1.0 / B_GATE_TAU)

    o_ref[...] = _dot(h_ref[...], w_ref[...])


def _inproj(x, g, w, gate=None, *, n, tm, tn):
    m, d = x.shape
    grid = (m // tm, n // tn)
    in_specs = [pl.BlockSpec((tm, d), lambda i, j: (i, 0)),
                pl.BlockSpec((1, d), lambda i, j: (0, 0)),
                pl.BlockSpec((d, tn), lambda i, j: (0, j))]
    out_shape = [jax.ShapeDtypeStruct((m, n), F32)]
    out_specs = [pl.BlockSpec((tm, tn), lambda i, j: (i, j))]
    args = [x, g, w]
    if gate is not None:
        wlr, wgate, gbias = gate
        in_specs += [pl.BlockSpec(wlr.shape, lambda i, j: (0, 0)),
                     pl.BlockSpec(wgate.shape, lambda i, j: (0, 0)),
                     pl.BlockSpec(gbias.shape, lambda i, j: (0, 0))]
        out_shape.append(jax.ShapeDtypeStruct((m, B_KEY_WIDTH), F32))
        out_specs.append(pl.BlockSpec((tm, B_KEY_WIDTH), lambda i, j: (i, 0)))
        args += [wlr, wgate, gbias]
    res = pl.pallas_call(
        functools.partial(_inproj_kernel, with_gate=gate is not None),
        out_shape=out_shape, grid=grid, in_specs=in_specs, out_specs=out_specs,
        scratch_shapes=[pltpu.VMEM((tm, d), BF16)],
        compiler_params=_params(("parallel", "arbitrary")),
        name="inproj_gate" if gate is not None else "inproj",
    )(*args)
    return res if gate is not None else res[0]


def _sgu_kernel(u_ref, v_ref, gain_ref, w_ref, b_ref, o_ref, av_ref, *, nblk):
    for n in range(nblk):
        rows = slice(n * A_BLOCK, (n + 1) * A_BLOCK)
        v = jax.nn.gelu(v_ref[rows, :])
        mu = jnp.mean(v, axis=-1, keepdims=True)
        vc = v - mu
        vn = vc * lax.rsqrt(jnp.mean(vc * vc, axis=-1, keepdims=True) + EPS) * gain_ref[...]
        av_ref[rows, :] = vn
        vb = vn.astype(BF16)
        for h in range(A_HEADS):
            cols = slice(h * LANES, (h + 1) * LANES)
            s = _dot(w_ref[h], vb[:, cols]) + b_ref[:, h:h + 1]
            o_ref[rows, cols] = (jax.nn.gelu(u_ref[rows, cols]) * s).astype(o_ref.dtype)


def _sgu(proj, gain, w, b, *, nblk):
    m = proj.shape[0]
    tm = nblk * A_BLOCK
    return pl.pallas_call(
        functools.partial(_sgu_kernel, nblk=nblk),
        out_shape=[jax.ShapeDtypeStruct((m, A_WIDTH), BF16), jax.ShapeDtypeStruct((m, A_WIDTH), F32)],
        grid=(m // tm,),
        in_specs=[pl.BlockSpec((tm, A_WIDTH), lambda i: (i, 0)),
                  pl.BlockSpec((tm, A_WIDTH), lambda i: (i, 1)),
                  pl.BlockSpec((1, A_WIDTH), lambda i: (0, 0)),
                  pl.BlockSpec(w.shape, lambda i: (0, 0, 0)),
                  pl.BlockSpec(b.shape, lambda i: (0, 0))],
        out_specs=[pl.BlockSpec((tm, A_WIDTH), lambda i: (i, 0)),
                   pl.BlockSpec((tm, A_WIDTH), lambda i: (i, 0))],
        compiler_params=_params(("parallel",)),
        name="sgu",
    )(proj, proj, gain, w, b)


def _gla_kernel(q_ref, k_ref, v_ref, r_ref, lg_ref, s0_ref, og_ref, o_ref, sout_ref, st_ref, *, clen, nch):
    c = pl.program_id(1)

    @pl.when(c == 0)
    def _():
        for h in range(B_HEADS):
            st_ref[h] = s0_ref[0, h].T

    row_i = lax.broadcasted_iota(jnp.int32, (clen, clen), 0)
    col_i = lax.broadcasted_iota(jnp.int32, (clen, clen), 1)
    tri = (row_i >= col_i).astype(F32)
    for n in range(nch):
        rows = slice(n * clen, (n + 1) * clen)
        for h in range(B_HEADS):
            kc = slice(h * B_KEY_DIM, (h + 1) * B_KEY_DIM)
            vc = slice(h * B_VAL_DIM, (h + 1) * B_VAL_DIM)
            cum = jnp.dot(tri, lg_ref[rows, kc], precision=lax.Precision.HIGHEST,
                          preferred_element_type=F32)
            tot = cum[clen - 1:clen, :]
            kd = (k_ref[rows, kc] * jnp.exp(tot - cum)).astype(BF16)
            upd = lax.dot_general(v_ref[rows, vc].astype(BF16), kd, (((0,), (0,)), ((), ())),
                                  preferred_element_type=F32)
            st = jnp.exp(tot) * st_ref[h] + upd
            st_ref[h] = st
            qh = (q_ref[rows, kc] * (B_KEY_DIM ** -0.5)).astype(BF16)
            o = lax.dot_general(qh, st.astype(BF16), (((1,), (1,)), ((), ())),
                                preferred_element_type=F32)
            o = o * lax.rsqrt(jnp.mean(o * o, axis=-1, keepdims=True) + EPS)
            o = o * og_ref[:, vc] * jax.nn.silu(r_ref[rows, vc])
            o_ref[rows, vc] = o.astype(o_ref.dtype)

    @pl.when(c == pl.num_programs(1) - 1)
    def _():
        for h in range(B_HEADS):
            sout_ref[0, h] = st_ref[h].T


def _gla(proj, lg, s0, og, *, bsz, t, clen, nch):
    rows = clen * nch
    steps = t // rows
    m = bsz * t
    row_map = lambda col: (lambda b, c: (b * steps + c, col))
    return pl.pallas_call(
        functools.partial(_gla_kernel, clen=clen, nch=nch),
        out_shape=[jax.ShapeDtypeStruct((m, B_WIDTH), BF16),
                   jax.ShapeDtypeStruct((bsz, B_HEADS, B_KEY_DIM, B_VAL_DIM), F32)],
        grid=(bsz, steps),
        in_specs=[pl.BlockSpec((rows, B_KEY_WIDTH), row_map(4)),
                  pl.BlockSpec((rows, B_KEY_WIDTH), row_map(5)),
                  pl.BlockSpec((rows, B_WIDTH), row_map(3)),
                  pl.BlockSpec((rows, B_WIDTH), row_map(4)),
                  pl.BlockSpec((rows, B_KEY_WIDTH), row_map(0)),
                  pl.BlockSpec((1, B_HEADS, B_KEY_DIM, B_VAL_DIM), lambda b, c: (b, 0, 0, 0)),
                  pl.BlockSpec((1, B_WIDTH), lambda b, c: (0, 0))],
        out_specs=[pl.BlockSpec((rows, B_WIDTH), row_map(0)),
                   pl.BlockSpec((1, B_HEADS, B_KEY_DIM, B_VAL_DIM), lambda b, c: (b, 0, 0, 0))],
        scratch_shapes=[pltpu.VMEM((B_HEADS, B_VAL_DIM, B_KEY_DIM), F32)],
        compiler_params=_params(("parallel", "arbitrary")),
        name="gla",
    )(proj, proj, proj, proj, lg, s0, og)


def _outproj_kernel(a_ref, b_ref, w_ref, x_ref, g_ref, o_ref):
    ka = a_ref.shape[1]
    y = _dot(a_ref[...], w_ref[0:ka, :]) + _dot(b_ref[...], w_ref[ka:, :])
    o_ref[...] = x_ref[...] + _rms(y, g_ref[...])


def _outproj(a, b, w, x, g, *, tm):
    m, d = x.shape
    return pl.pallas_call(
        _outproj_kernel,
        out_shape=jax.ShapeDtypeStruct((m, d), F32),
        grid=(m // tm,),
        in_specs=[pl.BlockSpec((tm, a.shape[1]), lambda i: (i, 0)),
                  pl.BlockSpec((tm, b.shape[1]), lambda i: (i, 0)),
                  pl.BlockSpec(w.shape, lambda i: (0, 0)),
                  pl.BlockSpec((tm, d), lambda i: (i, 0)),
                  pl.BlockSpec((1, d), lambda i: (0, 0))],
        out_specs=pl.BlockSpec((tm, d), lambda i: (i, 0)),
        compiler_params=_params(("parallel",)),
        name="outproj",
    )(a, b, w, x, g)


def _ffn_kernel(x_ref, gpre_ref, wg_ref, wv_ref, cw_ref, cb_ref, wd_ref, gpost_ref, st_ref,
                o_ref, tail_ref, h_ref, acc_ref, gext_ref, carry_ref, act_a, act_b,
                *, step, hist, tps, tm, nf):
    i = pl.program_id(0)
    j = pl.program_id(1)
    first = (i % tps) == 0

    def up_and_gate(act_ref):
        hb = h_ref[...]
        gate = _dot(hb, wg_ref[...])
        val = _dot(hb, wv_ref[...])
        gext_ref[0:hist, :] = jnp.where(first, st_ref[0], carry_ref[j])
        gext_ref[hist:hist + tm, :] = gate
        prev2 = gext_ref[hist - 2 * step:hist - 2 * step + tm, :]
        prev1 = gext_ref[hist - step:hist - step + tm, :]
        conv = cb_ref[...] + cw_ref[0:1, :] * prev2 + cw_ref[1:2, :] * prev1 + cw_ref[2:3, :] * gate
        act_ref[...] = (jax.nn.gelu(conv) * val).astype(BF16)
        tail = gate[tm - hist:, :]
        carry_ref[j] = tail
        tail_ref[0] = tail

    def down(act_ref):
        acc_ref[...] += _dot(act_ref[...], wd_ref[...])

    @pl.when(j == 0)
    def _():
        h_ref[...] = _rms(x_ref[...], gpre_ref[...]).astype(BF16)
        acc_ref[...] = jnp.zeros_like(acc_ref)

        @pl.when(i == 0)
        def _():
            carry_ref[...] = jnp.zeros_like(carry_ref)

        up_and_gate(act_a)

    for parity, (src, dst) in enumerate(((act_b, act_a), (act_a, act_b))):
        @pl.when((j > 0) & (j < nf) & (j % 2 == parity))
        def _(src=src, dst=dst):
            down(src)
            up_and_gate(dst)

    @pl.when(j == nf)
    def _():
        down(act_a if (nf - 1) % 2 == 0 else act_b)
        o_ref[...] = x_ref[...] + _rms(acc_ref[...], gpost_ref[...])


def _ffn(x, gpre, w_up, cw, cb, w_down, gpost, state, *, layer, step, hist, tps, tm, tf):
    m, d = x.shape
    nf = D_FF // tf
    nm = m // tm
    up = lambda j: jnp.minimum(j, nf - 1)
    return pl.pallas_call(
        functools.partial(_ffn_kernel, step=step, hist=hist, tps=tps, tm=tm, nf=nf),
        out_shape=[jax.ShapeDtypeStruct((m, d), F32), jax.ShapeDtypeStruct((nm, hist, D_FF), F32)],
        grid=(nm, nf + 1),
        in_specs=[pl.BlockSpec((tm, d), lambda i, j: (i, 0)),
                  pl.BlockSpec((None, 1, d), lambda i, j: (layer, 0, 0)),
                  pl.BlockSpec((None, d, tf), lambda i, j: (layer, 0, up(j))),
                  pl.BlockSpec((None, d, tf), lambda i, j: (layer, 0, nf + up(j))),
                  pl.BlockSpec((None, 3, tf), lambda i, j: (layer, 0, up(j))),
                  pl.BlockSpec((None, 1, tf), lambda i, j: (layer, 0, up(j))),
                  pl.BlockSpec((None, tf, d), lambda i, j: (layer, jnp.maximum(j - 1, 0), 0)),
                  pl.BlockSpec((None, 1, d), lambda i, j: (layer, 0, 0)),
                  pl.BlockSpec((1, hist, tf), lambda i, j: (i // tps, 0, up(j)))],
        out_specs=[pl.BlockSpec((tm, d), lambda i, j: (i, 0)),
                   pl.BlockSpec((1, hist, tf), lambda i, j: (i, 0, up(j)))],
        scratch_shapes=[pltpu.VMEM((tm, d), BF16), pltpu.VMEM((tm, d), F32),
                        pltpu.VMEM((hist + tm, tf), F32), pltpu.VMEM((nf, hist, tf), F32),
                        pltpu.VMEM((tm, tf), BF16), pltpu.VMEM((tm, tf), BF16)],
        compiler_params=_params(("arbitrary", "arbitrary")),
        name="ffn",
    )(x, gpre, w_up, w_up, cw, cb, w_down, gpost, state)


def _pool_kernel(c_ref, st_ref, cmap_ref, cs_ref, o_ref, tail_ref, ext_ref, carry_ref,
                 *, step, hist, tps, tm, pos0):
    i = pl.program_id(0)
    first = (i % tps) == 0

    @pl.when(first)
    def _():
        ext_ref[0:hist, :] = st_ref[0]

    @pl.when(jnp.logical_not(first))
    def _():
        ext_ref[0:hist, :] = carry_ref[...]

    ext_ref[hist:hist + tm, :] = c_ref[...]
    row = lax.broadcasted_iota(jnp.int32, (tm, 1), 0)
    if step > 1:
        row = lax.shift_right_logical(row, int(math.log2(step)))
    pos = pos0 + (i % tps) * (tm // step) + row
    for g, win in enumerate(POOL_WINDOWS):
        cols = slice(g * C_GROUP_DIM, (g + 1) * C_GROUP_DIM)
        cur = ext_ref[hist:hist + tm, cols]
        tot = cur
        for k in range(1, win):
            tot = tot + ext_ref[hist - k * step:hist - k * step + tm, cols]
        cnt = jnp.minimum(pos + 1, win).astype(F32)
        delta = tot / cnt - cur
        y = _dot(delta.astype(BF16), cmap_ref[g]) * cs_ref[:, cols]
        o_ref[:, cols] = y.astype(o_ref.dtype)
    tail = ext_ref[tm:tm + hist, :]
    carry_ref[...] = tail
    tail_ref[0] = tail


def _pool(proj, state, cmap, cscale, *, step, hist, tps, tm, pos0):
    m = proj.shape[0]
    nm = m // tm
    return pl.pallas_call(
        functools.partial(_pool_kernel, step=step, hist=hist, tps=tps, tm=tm, pos0=pos0),
        out_shape=[jax.ShapeDtypeStruct((m, C_WIDTH), BF16), jax.ShapeDtypeStruct((nm, hist, C_WIDTH), F32)],
        grid=(nm,),
        in_specs=[pl.BlockSpec((tm, C_WIDTH), lambda i: (i, 0)),
                  pl.BlockSpec((1, hist, C_WIDTH), lambda i: (i // tps, 0, 0)),
                  pl.BlockSpec(cmap.shape, lambda i: (0, 0, 0)),
                  pl.BlockSpec((1, C_WIDTH), lambda i: (0, 0))],
        out_specs=[pl.BlockSpec((tm, C_WIDTH), lambda i: (i, 0)),
                   pl.BlockSpec((1, hist, C_WIDTH), lambda i: (i, 0, 0))],
        scratch_shapes=[pltpu.VMEM((hist + tm, C_WIDTH), F32), pltpu.VMEM((hist, C_WIDTH), F32)],
        compiler_params=_params(("arbitrary",)),
        name="pool",
    )(proj, state, cmap, cscale)


S5_CHUNK = 512
S5_NCHUNK = S5_CH // S5_CHUNK


def _s5_kernel(u_ref, s0re_ref, s0im_ref, bblk_ref, cblk_ref, dskip_ref, wglu_ref, kc_ref,
               o_ref, tre_ref, tim_ref, sre_ref, sim_ref, cre_ref, cim_ref, *, step, tps, tm):
    i = pl.program_id(0)
    crow = cre_ref.shape[0]

    ub = u_ref[...].astype(BF16)
    for m in range(S5_NCHUNK):
        r = _dot(ub[:, m * LANES:(m + 1) * LANES], bblk_ref[m])
        sre_ref[:, m * S5_CHUNK:(m + 1) * S5_CHUNK] = r[:, :S5_CHUNK]
        sim_ref[:, m * S5_CHUNK:(m + 1) * S5_CHUNK] = r[:, S5_CHUNK:]

    @pl.when((i % tps) == 0)
    def _():
        cre_ref[...] = jnp.broadcast_to(s0re_ref[0], cre_ref.shape) if step == 1 else s0re_ref[0]
        cim_ref[...] = jnp.broadcast_to(s0im_ref[0], cim_ref.shape) if step == 1 else s0im_ref[0]

    for m in range(S5_NCHUNK):
        cols = slice(m * S5_CHUNK, (m + 1) * S5_CHUNK)
        if step == 1:
            consts = [kc_ref[k, :, cols] for k in range(8)]
            a1r, a1i, a2r, a2i, a4r, a4i, pwr, pwi = consts

            def body(rb, carry):
                cr, ci = carry
                r0 = pl.multiple_of(rb * SUBLANES, SUBLANES)
                xr = sre_ref[pl.ds(r0, SUBLANES), cols]
                xi = sim_ref[pl.ds(r0, SUBLANES), cols]
                for ar, ai, d in ((a1r, a1i, 1), (a2r, a2i, 2), (a4r, a4i, 4)):
                    sr = pltpu.roll(xr, d, 0)
                    si = pltpu.roll(xi, d, 0)
                    xr, xi = xr + (ar * sr - ai * si), xi + (ar * si + ai * sr)
                xr, xi = xr + (pwr * cr - pwi * ci), xi + (pwr * ci + pwi * cr)
                sre_ref[pl.ds(r0, SUBLANES), cols] = xr
                sim_ref[pl.ds(r0, SUBLANES), cols] = xi
                return (jnp.broadcast_to(xr[SUBLANES - 1:SUBLANES, :], xr.shape),
                        jnp.broadcast_to(xi[SUBLANES - 1:SUBLANES, :], xi.shape))

            cr, ci = lax.fori_loop(0, tm // SUBLANES, body, (cre_ref[:, cols], cim_ref[:, cols]))
        else:
            lr = jnp.broadcast_to(kc_ref[0, 0:1, cols], (crow, S5_CHUNK))
            li = jnp.broadcast_to(kc_ref[1, 0:1, cols], (crow, S5_CHUNK))

            def body(t, carry):
                cr, ci = carry
                r0 = pl.multiple_of(t * step, step)
                xr = sre_ref[pl.ds(r0, step), cols] + (lr * cr - li * ci)
                xi = sim_ref[pl.ds(r0, step), cols] + (lr * ci + li * cr)
                sre_ref[pl.ds(r0, step), cols] = xr
                sim_ref[pl.ds(r0, step), cols] = xi
                return xr, xi

            cr, ci = lax.fori_loop(0, tm // step, body, (cre_ref[:, cols], cim_ref[:, cols]))
        cre_ref[:, cols] = cr
        cim_ref[:, cols] = ci

    tre_ref[0] = cre_ref[...]
    tim_ref[0] = cim_ref[...]

    ys = []
    for m in range(S5_NCHUNK):
        cols = slice(m * S5_CHUNK, (m + 1) * S5_CHUNK)
        ys.append(_dot(sre_ref[:, cols].astype(BF16), cblk_ref[m, 0:S5_CHUNK, :])
                  + _dot(sim_ref[:, cols].astype(BF16), cblk_ref[m, S5_CHUNK:, :]))
    y = jnp.concatenate(ys, axis=1) + dskip_ref[...] * u_ref[...]
    z = _dot(jax.nn.gelu(y).astype(BF16), wglu_ref[...])
    o_ref[...] = (z[:, :D_WIDTH] * jax.nn.sigmoid(z[:, D_WIDTH:])).astype(o_ref.dtype)


def _s5(proj, s0re, s0im, bblk, cblk, dskip, wglu, kconst, *, step, tps, tm):
    m = proj.shape[0]
    nm = m // tm
    crow = s0re.shape[1] if step > 1 else SUBLANES
    srow = s0re.shape[1]
    return pl.pallas_call(
        functools.partial(_s5_kernel, step=step, tps=tps, tm=tm),
        out_shape=[jax.ShapeDtypeStruct((m, D_WIDTH), BF16),
                   jax.ShapeDtypeStruct((nm, crow, S5_CH), F32),
                   jax.ShapeDtypeStruct((nm, crow, S5_CH), F32)],
        grid=(nm,),
        in_specs=[pl.BlockSpec((tm, D_WIDTH), lambda i: (i, 1)),
                  pl.BlockSpec((1, srow, S5_CH), lambda i: (i // tps, 0, 0)),
                  pl.BlockSpec((1, srow, S5_CH), lambda i: (i // tps, 0, 0)),
                  pl.BlockSpec(bblk.shape, lambda i: (0, 0, 0)),
                  pl.BlockSpec(cblk.shape, lambda i: (0, 0, 0)),
                  pl.BlockSpec((1, D_WIDTH), lambda i: (0, 0)),
                  pl.BlockSpec(wglu.shape, lambda i: (0, 0)),
                  pl.BlockSpec(kconst.shape, lambda i: (0, 0, 0))],
        out_specs=[pl.BlockSpec((tm, D_WIDTH), lambda i: (i, 0)),
                   pl.BlockSpec((1, crow, S5_CH), lambda i: (i, 0, 0)),
                   pl.BlockSpec((1, crow, S5_CH), lambda i: (i, 0, 0))],
        scratch_shapes=[pltpu.VMEM((tm, S5_CH), F32), pltpu.VMEM((tm, S5_CH), F32),
                        pltpu.VMEM((crow, S5_CH), F32), pltpu.VMEM((crow, S5_CH), F32)],
        compiler_params=_params(("arbitrary",)),
        name="s5",
    )(proj, s0re, s0im, bblk, cblk, dskip, wglu, kconst)


def _block_diag(blocks, per):
    n, r, c = blocks.shape
    b = blocks.reshape(n // per, per, r, c)
    eye = jnp.eye(per, dtype=blocks.dtype)
    return jnp.einsum('mgrc,gh->mgrhc', b, eye).reshape(n // per, per * r, per * c)


def _s5_constants(a_re, a_im, log_dt, b_re, b_im, c_re, c_im):
    dt = jnp.exp(log_dt)[:, None]
    zr, zi = a_re * dt, a_im * dt
    mag = jnp.exp(zr)
    lr, li = mag * jnp.cos(zi), mag * jnp.sin(zi)
    den = a_re * a_re + a_im * a_im
    nr, ni = lr - 1.0, li
    kr, ki = (nr * a_re + ni * a_im) / den, (ni * a_re - nr * a_im) / den
    bbr = kr[..., None] * b_re - ki[..., None] * b_im
    bbi = kr[..., None] * b_im + ki[..., None] * b_re
    per = S5_CHUNK // S5_STATE
    bblk = jnp.concatenate([_block_diag(jnp.swapaxes(bbr, 1, 2), per),
                            _block_diag(jnp.swapaxes(bbi, 1, 2), per)], axis=2).astype(BF16)
    cblk = jnp.concatenate([_block_diag(jnp.swapaxes(c_re, 1, 2), per),
                            _block_diag(jnp.swapaxes(-c_im, 1, 2), per)], axis=1).astype(BF16)
    lr, li = lr.reshape(1, S5_CH), li.reshape(1, S5_CH)
    pr, pi = [lr], [li]
    for _ in range(SUBLANES - 1):
        pr, pi = pr + [pr[-1] * lr - pi[-1] * li], pi + [pr[-1] * li + pi[-1] * lr]
    rowid = jnp.arange(SUBLANES)[:, None]

    def masked(p, d):
        return jnp.where(rowid >= d, jnp.broadcast_to(p[d - 1], (SUBLANES, S5_CH)), 0.0)

    k_prompt = jnp.stack([masked(pr, 1), masked(pi, 1), masked(pr, 2), masked(pi, 2),
                          masked(pr, 4), masked(pi, 4),
                          jnp.concatenate(pr, axis=0), jnp.concatenate(pi, axis=0)])
    k_sample = jnp.stack([jnp.broadcast_to(lr, (SUBLANES, S5_CH)), jnp.broadcast_to(li, (SUBLANES, S5_CH))])
    return bblk, cblk, k_prompt, k_sample


def _time_major(a):
    a = jnp.swapaxes(a, 0, 1)
    return a.reshape((a.shape[0] * a.shape[1],) + a.shape[2:])


def kernel(x_prompt, x_sample, state_gla, state_pool, state_s5_re, state_s5_im, state_ffn_conv, norm_mix_pre, norm_mix_post, norm_ffn_pre, norm_ffn_post, w_in_even, a_w_s, a_b_s, a_v_norm, b_w_gate, b_gate_bias, b_out_norm, w_out_even, w_in_odd, c_map, c_scale, s5_a_re, s5_a_im, s5_log_dt, s5_b_re, s5_b_im, s5_c_re, s5_c_im, s5_d, s5_w_glu, w_out_odd, ffn_w_up, ffn_conv_w, ffn_conv_b, ffn_w_down):
    bp = x_prompt.shape[0]
    nb, ts = x_sample.shape[0], x_sample.shape[1]
    xp = x_prompt.reshape(bp * SEQ, D_MODEL)
    xs = x_sample.reshape(nb * ts, D_MODEL)

    row = lambda v: v.reshape(1, -1)
    n_main = 2 * A_WIDTH + 2 * B_KEY_WIDTH + 2 * B_WIDTH
    w_in0 = w_in_even[0].astype(BF16)
    w_lr = jnp.pad(w_in_even[0][:, n_main:], ((0, 0), (0, LANES - B_GATE_RANK))).astype(BF16)
    w_gate = jnp.pad(b_w_gate[0], ((0, LANES - B_GATE_RANK), (0, 0))).astype(BF16)
    gate = (w_lr, w_gate, row(b_gate_bias[0]))
    pos = jnp.arange(A_BLOCK)
    causal = (pos[None, :] // CHUNK) <= (pos[:, None] // CHUNK)
    ws_prompt = jnp.where(causal[None], a_w_s[0], 0.0).astype(BF16)
    per = A_BLOCK // ts
    ws_small = jnp.where(causal[None, :ts, :ts], a_w_s[0][:, :ts, :ts], 0.0)
    ws_sample = jnp.einsum('hij,ab->haibj', ws_small, jnp.eye(per, dtype=F32)).reshape(A_HEADS, A_BLOCK, A_BLOCK).astype(BF16)
    bs_prompt = a_b_s[0].T
    bs_sample = jnp.tile(a_b_s[0][:, :ts].T, (per, 1))
    w_out0 = w_out_even[0].astype(BF16)
    w_in1 = w_in_odd[0].astype(BF16)
    cmap = c_map[0].astype(BF16)
    bblk, cblk, k_prompt, k_sample = _s5_constants(s5_a_re[0], s5_a_im[0], s5_log_dt[0], s5_b_re[0], s5_b_im[0],
                                                   s5_c_re[0], s5_c_im[0])
    wglu = s5_w_glu[0].astype(BF16)
    w_out1 = w_out_odd[0].astype(BF16)
    w_up = ffn_w_up.astype(BF16)
    w_down = ffn_w_down.astype(BF16)

    tm = 512
    tf = 512
    tps_p = SEQ // tm
    ffn_hist_p = SUBLANES
    pool_hist_p = 2 * SUBLANES
    step_s = nb
    tps_s = (nb * ts) // tm
    ffn_hist_s = 2 * step_s
    pool_hist_s = (POOL_BUF + 1) * step_s

    def ffn_layer(x, layer, state, *, step, hist, tps):
        return _ffn(x, norm_ffn_pre[:, None], w_up, ffn_conv_w, ffn_conv_b[:, None], w_down, norm_ffn_post[:, None],
                    state, layer=layer, step=step, hist=hist, tps=tps, tm=tm, tf=tf)

    proj, lg = _inproj(xp, row(norm_mix_pre[0]), w_in0, gate, n=n_main, tm=1024, tn=1024)
    a_out, _ = _sgu(proj, row(a_v_norm[0]), ws_prompt, bs_prompt, nblk=2)
    b_out, gla_p = _gla(proj, lg, jnp.zeros((bp, B_HEADS, B_KEY_DIM, B_VAL_DIM), F32), row(b_out_norm[0]),
                        bsz=bp, t=SEQ, clen=CHUNK, nch=4)
    xp = _outproj(a_out, b_out, w_out0, xp, row(norm_mix_post[0]), tm=tm)
    xp, ffn0_p = ffn_layer(xp, 0, jnp.zeros((bp, ffn_hist_p, D_FF), F32), step=1, hist=ffn_hist_p, tps=tps_p)
    proj = _inproj(xp, row(norm_mix_pre[1]), w_in1, n=D_MODEL, tm=1024, tn=1024)
    c_out, pool_tail_p = _pool(proj, jnp.zeros((bp, pool_hist_p, C_WIDTH), F32), cmap, row(c_scale[0]),
                               step=1, hist=pool_hist_p, tps=tps_p, tm=tm, pos0=0)
    zero_state = jnp.zeros((bp, 1, S5_CH), F32)
    d_out, s5re_tail_p, s5im_tail_p = _s5(proj, zero_state, zero_state, bblk, cblk, row(s5_d[0]), wglu, k_prompt,
                                          step=1, tps=tps_p, tm=tm)
    xp = _outproj(c_out, d_out, w_out1, xp, row(norm_mix_post[1]), tm=tm)
    xp, ffn1_p = ffn_layer(xp, 1, jnp.zeros((bp, ffn_hist_p, D_FF), F32), step=1, hist=ffn_hist_p, tps=tps_p)

    last = slice(tps_p - 1, None, tps_p)
    y_prompt = xp.reshape(bp, SEQ, D_MODEL)
    gla_prompt = gla_p[None]
    pool_prompt = pool_tail_p[last, pool_hist_p - POOL_BUF:][None]
    s5_re_prompt = s5re_tail_p[last, 0].reshape(1, bp, S5_GROUPS, S5_STATE)
    s5_im_prompt = s5im_tail_p[last, 0].reshape(1, bp, S5_GROUPS, S5_STATE)
    ffn_prompt = jnp.stack([ffn0_p[last, ffn_hist_p - 2:], ffn1_p[last, ffn_hist_p - 2:]])

    proj, lg = _inproj(xs, row(norm_mix_pre[0]), w_in0, gate, n=n_main, tm=1024, tn=1024)
    a_out, a_v = _sgu(proj, row(a_v_norm[0]), ws_sample, bs_sample, nblk=2)
    b_out, gla_s = _gla(proj, lg, state_gla[0], row(b_out_norm[0]), bsz=nb, t=ts, clen=ts, nch=1)
    xs = _outproj(a_out, b_out, w_out0, xs, row(norm_mix_post[0]), tm=tm)
    xs = _time_major(xs.reshape(nb, ts, D_MODEL))
    ffn_state = lambda layer: _time_major(state_ffn_conv[layer])[None]
    xs, ffn0_s = ffn_layer(xs, 0, ffn_state(0), step=step_s, hist=ffn_hist_s, tps=tps_s)
    proj = _inproj(xs, row(norm_mix_pre[1]), w_in1, n=D_MODEL, tm=1024, tn=1024)
    pool_state = jnp.pad(_time_major(state_pool[0]), ((step_s, 0), (0, 0)))[None]
    c_out, pool_tail_s = _pool(proj, pool_state, cmap, row(c_scale[0]),
                               step=step_s, hist=pool_hist_s, tps=tps_s, tm=tm, pos0=PAST_LEN)
    d_out, s5re_tail_s, s5im_tail_s = _s5(proj, state_s5_re[0].reshape(1, nb, S5_CH),
                                          state_s5_im[0].reshape(1, nb, S5_CH),
                                          bblk, cblk, row(s5_d[0]), wglu, k_sample, step=step_s, tps=tps_s, tm=tm)
    xs = _outproj(c_out, d_out, w_out1, xs, row(norm_mix_post[1]), tm=tm)
    xs, ffn1_s = ffn_layer(xs, 1, ffn_state(1), step=step_s, hist=ffn_hist_s, tps=tps_s)

    def batch_major(a, nt):
        return jnp.swapaxes(a.reshape(nt, nb, a.shape[-1]), 0, 1)

    y_sample = batch_major(xs, ts)
    gla_sample = gla_s[None]
    av_sample = a_v.reshape(1, nb, ts, A_WIDTH)
    pool_sample = batch_major(pool_tail_s[-1, step_s:], POOL_BUF)[None]
    s5_re_sample = s5re_tail_s[-1].reshape(1, nb, S5_GROUPS, S5_STATE)
    s5_im_sample = s5im_tail_s[-1].reshape(1, nb, S5_GROUPS, S5_STATE)
    ffn_sample = jnp.stack([batch_major(ffn0_s[-1], 2), batch_major(ffn1_s[-1], 2)])

    return (y_prompt, y_sample, gla_prompt, gla_sample, av_sample, pool_prompt, pool_sample,
            s5_re_prompt, s5_im_prompt, s5_re_sample, s5_im_sample, ffn_prompt, ffn_sample)
```

```python
import functools
import math

import jax
import jax.numpy as jnp
from jax import lax
from jax.experimental import pallas as pl
from jax.experimental.pallas import tpu as pltpu

F32 = jnp.float32
BF16 = jnp.bfloat16

D_MODEL = 2048
SEQ = 4096
DEC_BATCH = 32
DEC_SEQ = 32
PAST_LEN = 4096
CHUNK = 64
A_WIDTH = 1024
A_HEADS = 8
A_BLOCK = 128
B_HEADS = 4
B_KEY_DIM = 128
B_KEY_WIDTH = 512
B_VAL_DIM = 256
B_WIDTH = 1024
B_GATE_RANK = 16
B_GATE_TAU = 16.0
C_WIDTH = 1024
C_GROUP_DIM = 256
POOL_WINDOWS = (2, 4, 8, 16)
POOL_BUF = 15
D_WIDTH = 1024
S5_GROUPS = 64
S5_GROUP_DIM = 16
S5_STATE = 64
S5_CH = S5_GROUPS * S5_STATE
D_FF = 5632
EPS = 1e-6

LANES = 128
SUBLANES = 8
VMEM_LIMIT = 56 * 1024 * 1024


def _params(sem):
    return pltpu.CompilerParams(dimension_semantics=sem, vmem_limit_bytes=VMEM_LIMIT)


def _rms(x, g):
    return x * lax.rsqrt(jnp.mean(x * x, axis=-1, keepdims=True) + EPS) * g


def _dot(a, b):
    return jnp.dot(a, b, preferred_element_type=F32)


def _inproj_kernel(x_ref, g_ref, w_ref, *rest, with_gate):
    if with_gate:
        wlr_ref, wgate_ref, gbias_ref, o_ref, lg_ref, h_ref = rest
    else:
        o_ref, h_ref = rest
    j = pl.program_id(1)

    @pl.when(j == 0)
    def _():
        hb = _rms(x_ref[...], g_ref[...]).astype(BF16)
        h_ref[...] = hb
        if with_gate:
            glr = _dot(hb, wlr_ref[...])
            z = _dot(glr.astype(BF16), wgate_ref[...]) + gbias_ref[...]
            lg_ref[...] = (jnp.minimum(z, 0.0) - jnp.log(1.0 + jnp.exp(-jnp.abs(z)))) * (1.0 / B_GATE_TAU)

    o_ref[...] = _dot(h_ref[...], w_ref[...])


def _inproj(x, g, w, gate=None, *, n, tm, tn):
    m, d = x.shape
    grid = (m // tm, n // tn)
    in_specs = [pl.BlockSpec((tm, d), lambda i, j: (i, 0)),
                pl.BlockSpec((1, d), lambda i, j: (0, 0)),
                pl.BlockSpec((d, tn), lambda i, j: (0, j))]
    out_shape = [jax.ShapeDtypeStruct((m, n), F32)]
    out_specs = [pl.BlockSpec((tm, tn), lambda i, j: (i, j))]
    args = [x, g, w]
    if gate is not None:
        wlr, wgate, gbias = gate
        in_specs += [pl.BlockSpec(wlr.shape, lambda i, j: (0, 0)),
                     pl.BlockSpec(wgate.shape, lambda i, j: (0, 0)),
                     pl.BlockSpec(gbias.shape, lambda i, j: (0, 0))]
        out_shape.append(jax.ShapeDtypeStruct((m, B_KEY_WIDTH), F32))
        out_specs.append(pl.BlockSpec((tm, B_KEY_WIDTH), lambda i, j: (i, 0)))
        args += [wlr, wgate, gbias]
    res = pl.pallas_call(
        functools.partial(_inproj_kernel, with_gate=gate is not None),
        out_shape=out_shape, grid=grid, in_specs=in_specs, out_specs=out_specs,
        scratch_shapes=[pltpu.VMEM((tm, d), BF16)],
        compiler_params=_params(("parallel", "arbitrary")),
        name="inproj_gate" if gate is not None else "inproj",
    )(*args)
    return res if gate is not None else res[0]


def _sgu_kernel(u_ref, v_ref, gain_ref, w_ref, b_ref, o_ref, av_ref, *, nblk):
    for n in range(nblk):
        rows = slice(n * A_BLOCK, (n + 1) * A_BLOCK)
        v = jax.nn.gelu(v_ref[rows, :])
        mu = jnp.mean(v, axis=-1, keepdims=True)
        vc = v - mu
        vn = vc * lax.rsqrt(jnp.mean(vc * vc, axis=-1, keepdims=True) + EPS) * gain_ref[...]
        av_ref[rows, :] = vn
        vb = vn.astype(BF16)
        for h in range(A_HEADS):
            cols = slice(h * LANES, (h + 1) * LANES)
            s = _dot(w_ref[h], vb[:, cols]) + b_ref[:, h:h + 1]
            o_ref[rows, cols] = (jax.nn.gelu(u_ref[rows, cols]) * s).astype(o_ref.dtype)


def _sgu(proj, gain, w, b, *, nblk):
    m = proj.shape[0]
    tm = nblk * A_BLOCK
    return pl.pallas_call(
        functools.partial(_sgu_kernel, nblk=nblk),
        out_shape=[jax.ShapeDtypeStruct((m, A_WIDTH), BF16), jax.ShapeDtypeStruct((m, A_WIDTH), F32)],
        grid=(m // tm,),
        in_specs=[pl.BlockSpec((tm, A_WIDTH), lambda i: (i, 0)),
                  pl.BlockSpec((tm, A_WIDTH), lambda i: (i, 1)),
                  pl.BlockSpec((1, A_WIDTH), lambda i: (0, 0)),
                  pl.BlockSpec(w.shape, lambda i: (0, 0, 0)),
                  pl.BlockSpec(b.shape, lambda i: (0, 0))],
        out_specs=[pl.BlockSpec((tm, A_WIDTH), lambda i: (i, 0)),
                   pl.BlockSpec((tm, A_WIDTH), lambda i: (i, 0))],
        compiler_params=_params(("parallel",)),
        name="sgu",
    )(proj, proj, gain, w, b)


def _gla_kernel(q_ref, k_ref, v_ref, r_ref, lg_ref, s0_ref, og_ref, o_ref, sout_ref, st_ref, *, clen, nch):
    c = pl.program_id(1)

    @pl.when(c == 0)
    def _():
        for h in range(B_HEADS):
            st_ref[h] = s0_ref[0, h].T

    row_i = lax.broadcasted_iota(jnp.int32, (clen, clen), 0)
    col_i = lax.broadcasted_iota(jnp.int32, (clen, clen), 1)
    tri = (row_i >= col_i).astype(F32)
    for n in range(nch):
        rows = slice(n * clen, (n + 1) * clen)
        for h in range(B_HEADS):
            kc = slice(h * B_KEY_DIM, (h + 1) * B_KEY_DIM)
            vc = slice(h * B_VAL_DIM, (h + 1) * B_VAL_DIM)
            cum = jnp.dot(tri, lg_ref[rows, kc], precision=lax.Precision.HIGHEST,
                          preferred_element_type=F32)
            tot = cum[clen - 1:clen, :]
            kd = (k_ref[rows, kc] * jnp.exp(tot - cum)).astype(BF16)
            upd = lax.dot_general(v_ref[rows, vc].astype(BF16), kd, (((0,), (0,)), ((), ())),
                                  preferred_element_type=F32)
            st = jnp.exp(tot) * st_ref[h] + upd
            st_ref[h] = st
            qh = (q_ref[rows, kc] * (B_KEY_DIM ** -0.5)).astype(BF16)
            o = lax.dot_general(qh, st.astype(BF16), (((1,), (1,)), ((), ())),
                                preferred_element_type=F32)
            o = o * lax.rsqrt(jnp.mean(o * o, axis=-1, keepdims=True) + EPS)
            o = o * og_ref[:, vc] * jax.nn.silu(r_ref[rows, vc])
            o_ref[rows, vc] = o.astype(o_ref.dtype)

    @pl.when(c == pl.num_programs(1) - 1)
    def _():
        for h in range(B_HEADS):
            sout_ref[0, h] = st_ref[h].T


def _gla(proj, lg, s0, og, *, bsz, t, clen, nch):
    rows = clen * nch
    steps = t // rows
    m = bsz * t
    row_map = lambda col: (lambda b, c: (b * steps + c, col))
    return pl.pallas_call(
        functools.partial(_gla_kernel, clen=clen, nch=nch),
        out_shape=[jax.ShapeDtypeStruct((m, B_WIDTH), BF16),
                   jax.ShapeDtypeStruct((bsz, B_HEADS, B_KEY_DIM, B_VAL_DIM), F32)],
        grid=(bsz, steps),
        in_specs=[pl.BlockSpec((rows, B_KEY_WIDTH), row_map(4)),
                  pl.BlockSpec((rows, B_KEY_WIDTH), row_map(5)),
                  pl.BlockSpec((rows, B_WIDTH), row_map(3)),
                  pl.BlockSpec((rows, B_WIDTH), row_map(4)),
                  pl.BlockSpec((rows, B_KEY_WIDTH), row_map(0)),
                  pl.BlockSpec((1, B_HEADS, B_KEY_DIM, B_VAL_DIM), lambda b, c: (b, 0, 0, 0)),
                  pl.BlockSpec((1, B_WIDTH), lambda b, c: (0, 0))],
        out_specs=[pl.BlockSpec((rows, B_WIDTH), row_map(0)),
                   pl.BlockSpec((1, B_HEADS, B_KEY_DIM, B_VAL_DIM), lambda b, c: (b, 0, 0, 0))],
        scratch_shapes=[pltpu.VMEM((B_HEADS, B_VAL_DIM, B_KEY_DIM), F32)],
        compiler_params=_params(("parallel", "arbitrary")),
        name="gla",
    )(proj, proj, proj, proj, lg, s0, og)


def _outproj_kernel(a_ref, b_ref, w_ref, x_ref, g_ref, o_ref):
    ka = a_ref.shape[1]
    y = _dot(a_ref[...], w_ref[0:ka, :]) + _dot(b_ref[...], w_ref[ka:, :])
    o_ref[...] = x_ref[...] + _rms(y, g_ref[...])


def _outproj(a, b, w, x, g, *, tm):
    m, d = x.shape
    return pl.pallas_call(
        _outproj_kernel,
        out_shape=jax.ShapeDtypeStruct((m, d), F32),
        grid=(m // tm,),
        in_specs=[pl.BlockSpec((tm, a.shape[1]), lambda i: (i, 0)),
                  pl.BlockSpec((tm, b.shape[1]), lambda i: (i, 0)),
                  pl.BlockSpec(w.shape, lambda i: (0, 0)),
                  pl.BlockSpec((tm, d), lambda i: (i, 0)),
                  pl.BlockSpec((1, d), lambda i: (0, 0))],
        out_specs=pl.BlockSpec((tm, d), lambda i: (i, 0)),
        compiler_params=_params(("parallel",)),
        name="outproj",
    )(a, b, w, x, g)


def _ffn_kernel(x_ref, gpre_ref, wg_ref, wv_ref, cw_ref, cb_ref, wd_ref, gpost_ref, st_ref,
                o_ref, tail_ref, h_ref, gext_ref, carry_ref, act_a, act_b,
                *, step, hist, tps, tm, nf):
    i = pl.program_id(0)
    j = pl.program_id(1)
    first = (i % tps) == 0

    def up_and_gate(act_ref):
        hb = h_ref[...]
        gate = _dot(hb, wg_ref[...])
        val = _dot(hb, wv_ref[...])
        gext_ref[0:hist, :] = jnp.where(first, st_ref[0], carry_ref[j])
        gext_ref[hist:hist + tm, :] = gate
        prev2 = gext_ref[hist - 2 * step:hist - 2 * step + tm, :]
        prev1 = gext_ref[hist - step:hist - step + tm, :]
        conv = cb_ref[...] + cw_ref[0:1, :] * prev2 + cw_ref[1:2, :] * prev1 + cw_ref[2:3, :] * gate
        act_ref[...] = (jax.nn.gelu(conv) * val).astype(BF16)
        tail = gate[tm - hist:, :]
        carry_ref[j] = tail
        tail_ref[0] = tail

    def down(act_ref):
        o_ref[...] += _dot(act_ref[...], wd_ref[...])

    @pl.when(j == 0)
    def _():
        h_ref[...] = _rms(x_ref[...], gpre_ref[...]).astype(BF16)
        o_ref[...] = jnp.zeros_like(o_ref)

        @pl.when(i == 0)
        def _():
            carry_ref[...] = jnp.zeros_like(carry_ref)

        up_and_gate(act_a)

    for parity, (src, dst) in enumerate(((act_b, act_a), (act_a, act_b))):
        @pl.when((j > 0) & (j < nf) & (j % 2 == parity))
        def _(src=src, dst=dst):
            down(src)
            up_and_gate(dst)

    @pl.when(j == nf)
    def _():
        down(act_a if (nf - 1) % 2 == 0 else act_b)
        o_ref[...] = x_ref[...] + _rms(o_ref[...], gpost_ref[...])


def _ffn(x, gpre, w_up, cw, cb, w_down, gpost, state, *, layer, step, hist, tps, tm, tf):
    m, d = x.shape
    nf = D_FF // tf
    nm = m // tm
    up = lambda j: jnp.minimum(j, nf - 1)
    return pl.pallas_call(
        functools.partial(_ffn_kernel, step=step, hist=hist, tps=tps, tm=tm, nf=nf),
        out_shape=[jax.ShapeDtypeStruct((m, d), F32), jax.ShapeDtypeStruct((nm, hist, D_FF), F32)],
        grid=(nm, nf + 1),
        in_specs=[pl.BlockSpec((tm, d), lambda i, j: (i, 0), pipeline_mode=pl.Buffered(1)),
                  pl.BlockSpec((None, 1, d), lambda i, j: (layer, 0, 0)),
                  pl.BlockSpec((None, d, tf), lambda i, j: (layer, 0, up(j))),
                  pl.BlockSpec((None, d, tf), lambda i, j: (layer, 0, nf + up(j))),
                  pl.BlockSpec((None, 3, tf), lambda i, j: (layer, 0, up(j))),
                  pl.BlockSpec((None, 1, tf), lambda i, j: (layer, 0, up(j))),
                  pl.BlockSpec((None, tf, d), lambda i, j: (layer, jnp.maximum(j - 1, 0), 0)),
                  pl.BlockSpec((None, 1, d), lambda i, j: (layer, 0, 0)),
                  pl.BlockSpec((1, hist, tf), lambda i, j: (i // tps, 0, up(j)))],
        out_specs=[pl.BlockSpec((tm, d), lambda i, j: (i, 0), pipeline_mode=pl.Buffered(1)),
                   pl.BlockSpec((1, hist, tf), lambda i, j: (i, 0, up(j)))],
        scratch_shapes=[pltpu.VMEM((tm, d), BF16),
                        pltpu.VMEM((hist + tm, tf), F32), pltpu.VMEM((nf, hist, tf), F32),
                        pltpu.VMEM((tm, tf), BF16), pltpu.VMEM((tm, tf), BF16)],
        compiler_params=_params(("arbitrary", "arbitrary")),
        name="ffn",
    )(x, gpre, w_up, w_up, cw, cb, w_down, gpost, state)


def _pool_kernel(c_ref, st_ref, cmap_ref, cs_ref, o_ref, tail_ref, ext_ref, carry_ref,
                 *, step, hist, tps, tm, pos0):
    i = pl.program_id(0)
    first = (i % tps) == 0

    @pl.when(first)
    def _():
        ext_ref[0:hist, :] = st_ref[0]

    @pl.when(jnp.logical_not(first))
    def _():
        ext_ref[0:hist, :] = carry_ref[...]

    ext_ref[hist:hist + tm, :] = c_ref[...]
    row = lax.broadcasted_iota(jnp.int32, (tm, 1), 0)
    if step > 1:
        row = lax.shift_right_logical(row, int(math.log2(step)))
    pos = pos0 + (i % tps) * (tm // step) + row
    for g, win in enumerate(POOL_WINDOWS):
        cols = slice(g * C_GROUP_DIM, (g + 1) * C_GROUP_DIM)
        cur = ext_ref[hist:hist + tm, cols]
        tot = cur
        for k in range(1, win):
            tot = tot + ext_ref[hist - k * step:hist - k * step + tm, cols]
        cnt = jnp.minimum(pos + 1, win).astype(F32)
        delta = tot / cnt - cur
        y = _dot(delta.astype(BF16), cmap_ref[g]) * cs_ref[:, cols]
        o_ref[:, cols] = y.astype(o_ref.dtype)
    tail = ext_ref[tm:tm + hist, :]
    carry_ref[...] = tail
    tail_ref[0] = tail


def _pool(proj, state, cmap, cscale, *, step, hist, tps, tm, pos0):
    m = proj.shape[0]
    nm = m // tm
    return pl.pallas_call(
        functools.partial(_pool_kernel, step=step, hist=hist, tps=tps, tm=tm, pos0=pos0),
        out_shape=[jax.ShapeDtypeStruct((m, C_WIDTH), BF16), jax.ShapeDtypeStruct((nm, hist, C_WIDTH), F32)],
        grid=(nm,),
        in_specs=[pl.BlockSpec((tm, C_WIDTH), lambda i: (i, 0)),
                  pl.BlockSpec((1, hist, C_WIDTH), lambda i: (i // tps, 0, 0)),
                  pl.BlockSpec(cmap.shape, lambda i: (0, 0, 0)),
                  pl.BlockSpec((1, C_WIDTH), lambda i: (0, 0))],
        out_specs=[pl.BlockSpec((tm, C_WIDTH), lambda i: (i, 0)),
                   pl.BlockSpec((1, hist, C_WIDTH), lambda i: (i, 0, 0))],
        scratch_shapes=[pltpu.VMEM((hist + tm, C_WIDTH), F32), pltpu.VMEM((hist, C_WIDTH), F32)],
        compiler_params=_params(("arbitrary",)),
        name="pool",
    )(proj, state, cmap, cscale)


S5_CHUNK = 512
S5_NCHUNK = S5_CH // S5_CHUNK


def _s5_kernel(u_ref, s0re_ref, s0im_ref, bblk_ref, cblk_ref, dskip_ref, wglu_ref, kc_ref,
               o_ref, tre_ref, tim_ref, sre_ref, sim_ref, cre_ref, cim_ref, *, step, tps, tm):
    i = pl.program_id(0)
    crow = cre_ref.shape[0]

    ub = u_ref[...].astype(BF16)
    for m in range(S5_NCHUNK):
        r = _dot(ub[:, m * LANES:(m + 1) * LANES], bblk_ref[m])
        sre_ref[:, m * S5_CHUNK:(m + 1) * S5_CHUNK] = r[:, :S5_CHUNK]
        sim_ref[:, m * S5_CHUNK:(m + 1) * S5_CHUNK] = r[:, S5_CHUNK:]

    @pl.when((i % tps) == 0)
    def _():
        cre_ref[...] = jnp.broadcast_to(s0re_ref[0], cre_ref.shape) if step == 1 else s0re_ref[0]
        cim_ref[...] = jnp.broadcast_to(s0im_ref[0], cim_ref.shape) if step == 1 else s0im_ref[0]

    for m in range(S5_NCHUNK):
        cols = slice(m * S5_CHUNK, (m + 1) * S5_CHUNK)
        if step == 1:
            consts = [kc_ref[k, :, cols] for k in range(8)]
            a1r, a1i, a2r, a2i, a4r, a4i, pwr, pwi = consts

            def body(rb, carry):
                cr, ci = carry
                r0 = pl.multiple_of(rb * SUBLANES, SUBLANES)
                xr = sre_ref[pl.ds(r0, SUBLANES), cols]
                xi = sim_ref[pl.ds(r0, SUBLANES), cols]
                for ar, ai, d in ((a1r, a1i, 1), (a2r, a2i, 2), (a4r, a4i, 4)):
                    sr = pltpu.roll(xr, d, 0)
                    si = pltpu.roll(xi, d, 0)
                    xr, xi = xr + (ar * sr - ai * si), xi + (ar * si + ai * sr)
                xr, xi = xr + (pwr * cr - pwi * ci), xi + (pwr * ci + pwi * cr)
                sre_ref[pl.ds(r0, SUBLANES), cols] = xr
                sim_ref[pl.ds(r0, SUBLANES), cols] = xi
                return (jnp.broadcast_to(xr[SUBLANES - 1:SUBLANES, :], xr.shape),
                        jnp.broadcast_to(xi[SUBLANES - 1:SUBLANES, :], xi.shape))

            cr, ci = lax.fori_loop(0, tm // SUBLANES, body, (cre_ref[:, cols], cim_ref[:, cols]))
        else:
            lr = jnp.broadcast_to(kc_ref[0, 0:1, cols], (crow, S5_CHUNK))
            li = jnp.broadcast_to(kc_ref[1, 0:1, cols], (crow, S5_CHUNK))

            def body(t, carry):
                cr, ci = carry
                r0 = pl.multiple_of(t * step, step)
                xr = sre_ref[pl.ds(r0, step), cols] + (lr * cr - li * ci)
                xi = sim_ref[pl.ds(r0, step), cols] + (lr * ci + li * cr)
                sre_ref[pl.ds(r0, step), cols] = xr
                sim_ref[pl.ds(r0, step), cols] = xi
                return xr, xi

            cr, ci = lax.fori_loop(0, tm // step, body, (cre_ref[:, cols], cim_ref[:, cols]))
        cre_ref[:, cols] = cr
        cim_ref[:, cols] = ci

    tre_ref[0] = cre_ref[...]
    tim_ref[0] = cim_ref[...]

    ys = []
    for m in range(S5_NCHUNK):
        cols = slice(m * S5_CHUNK, (m + 1) * S5_CHUNK)
        ys.append(_dot(sre_ref[:, cols].astype(BF16), cblk_ref[m, 0:S5_CHUNK, :])
                  + _dot(sim_ref[:, cols].astype(BF16), cblk_ref[m, S5_CHUNK:, :]))
    y = jnp.concatenate(ys, axis=1) + dskip_ref[...] * u_ref[...]
    z = _dot(jax.nn.gelu(y).astype(BF16), wglu_ref[...])
    o_ref[...] = (z[:, :D_WIDTH] * jax.nn.sigmoid(z[:, D_WIDTH:])).astype(o_ref.dtype)


def _s5(proj, s0re, s0im, bblk, cblk, dskip, wglu, kconst, *, step, tps, tm):
    m = proj.shape[0]
    nm = m // tm
    crow = s0re.shape[1] if step > 1 else SUBLANES
    srow = s0re.shape[1]
    return pl.pallas_call(
        functools.partial(_s5_kernel, step=step, tps=tps, tm=tm),
        out_shape=[jax.ShapeDtypeStruct((m, D_WIDTH), BF16),
                   jax.ShapeDtypeStruct((nm, crow, S5_CH), F32),
                   jax.ShapeDtypeStruct((nm, crow, S5_CH), F32)],
        grid=(nm,),
        in_specs=[pl.BlockSpec((tm, D_WIDTH), lambda i: (i, 1)),
                  pl.BlockSpec((1, srow, S5_CH), lambda i: (i // tps, 0, 0)),
                  pl.BlockSpec((1, srow, S5_CH), lambda i: (i // tps, 0, 0)),
                  pl.BlockSpec(bblk.shape, lambda i: (0, 0, 0)),
                  pl.BlockSpec(cblk.shape, lambda i: (0, 0, 0)),
                  pl.BlockSpec((1, D_WIDTH), lambda i: (0, 0)),
                  pl.BlockSpec(wglu.shape, lambda i: (0, 0)),
                  pl.BlockSpec(kconst.shape, lambda i: (0, 0, 0))],
        out_specs=[pl.BlockSpec((tm, D_WIDTH), lambda i: (i, 0)),
                   pl.BlockSpec((1, crow, S5_CH), lambda i: (i, 0, 0)),
                   pl.BlockSpec((1, crow, S5_CH), lambda i: (i, 0, 0))],
        scratch_shapes=[pltpu.VMEM((tm, S5_CH), F32), pltpu.VMEM((tm, S5_CH), F32),
                        pltpu.VMEM((crow, S5_CH), F32), pltpu.VMEM((crow, S5_CH), F32)],
        compiler_params=_params(("arbitrary",)),
        name="s5",
    )(proj, s0re, s0im, bblk, cblk, dskip, wglu, kconst)


def _block_diag(blocks, per):
    n, r, c = blocks.shape
    b = blocks.reshape(n // per, per, r, c)
    eye = jnp.eye(per, dtype=blocks.dtype)
    return jnp.einsum('mgrc,gh->mgrhc', b, eye).reshape(n // per, per * r, per * c)


def _s5_constants(a_re, a_im, log_dt, b_re, b_im, c_re, c_im):
    dt = jnp.exp(log_dt)[:, None]
    zr, zi = a_re * dt, a_im * dt
    mag = jnp.exp(zr)
    lr, li = mag * jnp.cos(zi), mag * jnp.sin(zi)
    den = a_re * a_re + a_im * a_im
    nr, ni = lr - 1.0, li
    kr, ki = (nr * a_re + ni * a_im) / den, (ni * a_re - nr * a_im) / den
    bbr = kr[..., None] * b_re - ki[..., None] * b_im
    bbi = kr[..., None] * b_im + ki[..., None] * b_re
    per = S5_CHUNK // S5_STATE
    bblk = jnp.concatenate([_block_diag(jnp.swapaxes(bbr, 1, 2), per),
                            _block_diag(jnp.swapaxes(bbi, 1, 2), per)], axis=2).astype(BF16)
    cblk = jnp.concatenate([_block_diag(jnp.swapaxes(c_re, 1, 2), per),
                            _block_diag(jnp.swapaxes(-c_im, 1, 2), per)], axis=1).astype(BF16)
    lr, li = lr.reshape(1, S5_CH), li.reshape(1, S5_CH)
    pr, pi = [lr], [li]
    for _ in range(SUBLANES - 1):
        pr, pi = pr + [pr[-1] * lr - pi[-1] * li], pi + [pr[-1] * li + pi[-1] * lr]
    rowid = jnp.arange(SUBLANES)[:, None]

    def masked(p, d):
        return jnp.where(rowid >= d, jnp.broadcast_to(p[d - 1], (SUBLANES, S5_CH)), 0.0)

    k_prompt = jnp.stack([masked(pr, 1), masked(pi, 1), masked(pr, 2), masked(pi, 2),
                          masked(pr, 4), masked(pi, 4),
                          jnp.concatenate(pr, axis=0), jnp.concatenate(pi, axis=0)])
    k_sample = jnp.stack([jnp.broadcast_to(lr, (SUBLANES, S5_CH)), jnp.broadcast_to(li, (SUBLANES, S5_CH))])
    return bblk, cblk, k_prompt, k_sample


def _time_major(a):
    a = jnp.swapaxes(a, 0, 1)
    return a.reshape((a.shape[0] * a.shape[1],) + a.shape[2:])


def kernel(x_prompt, x_sample, state_gla, state_pool, state_s5_re, state_s5_im, state_ffn_conv, norm_mix_pre, norm_mix_post, norm_ffn_pre, norm_ffn_post, w_in_even, a_w_s, a_b_s, a_v_norm, b_w_gate, b_gate_bias, b_out_norm, w_out_even, w_in_odd, c_map, c_scale, s5_a_re, s5_a_im, s5_log_dt, s5_b_re, s5_b_im, s5_c_re, s5_c_im, s5_d, s5_w_glu, w_out_odd, ffn_w_up, ffn_conv_w, ffn_conv_b, ffn_w_down):
    bp = x_prompt.shape[0]
    nb, ts = x_sample.shape[0], x_sample.shape[1]
    xp = x_prompt.reshape(bp * SEQ, D_MODEL)
    xs = x_sample.reshape(nb * ts, D_MODEL)

    row = lambda v: v.reshape(1, -1)
    n_main = 2 * A_WIDTH + 2 * B_KEY_WIDTH + 2 * B_WIDTH
    w_in0 = w_in_even[0].astype(BF16)
    w_lr = jnp.pad(w_in_even[0][:, n_main:], ((0, 0), (0, LANES - B_GATE_RANK))).astype(BF16)
    w_gate = jnp.pad(b_w_gate[0], ((0, LANES - B_GATE_RANK), (0, 0))).astype(BF16)
    gate = (w_lr, w_gate, row(b_gate_bias[0]))
    pos = jnp.arange(A_BLOCK)
    causal = (pos[None, :] // CHUNK) <= (pos[:, None] // CHUNK)
    ws_prompt = jnp.where(causal[None], a_w_s[0], 0.0).astype(BF16)
    per = A_BLOCK // ts
    ws_small = jnp.where(causal[None, :ts, :ts], a_w_s[0][:, :ts, :ts], 0.0)
    ws_sample = jnp.einsum('hij,ab->haibj', ws_small, jnp.eye(per, dtype=F32)).reshape(A_HEADS, A_BLOCK, A_BLOCK).astype(BF16)
    bs_prompt = a_b_s[0].T
    bs_sample = jnp.tile(a_b_s[0][:, :ts].T, (per, 1))
    w_out0 = w_out_even[0].astype(BF16)
    w_in1 = w_in_odd[0].astype(BF16)
    cmap = c_map[0].astype(BF16)
    bblk, cblk, k_prompt, k_sample = _s5_constants(s5_a_re[0], s5_a_im[0], s5_log_dt[0], s5_b_re[0], s5_b_im[0],
                                                   s5_c_re[0], s5_c_im[0])
    wglu = s5_w_glu[0].astype(BF16)
    w_out1 = w_out_odd[0].astype(BF16)
    w_up = ffn_w_up.astype(BF16)
    w_down = ffn_w_down.astype(BF16)

    tm = 512
    tf = 512
    tps_p = SEQ // tm
    ffn_hist_p = SUBLANES
    pool_hist_p = 2 * SUBLANES
    step_s = nb
    tps_s = (nb * ts) // tm
    ffn_hist_s = 2 * step_s
    pool_hist_s = (POOL_BUF + 1) * step_s

    tm_ffn = 1024
    tps_ffn_p = SEQ // tm_ffn
    tps_ffn_s = (nb * ts) // tm_ffn

    def ffn_layer(x, layer, state, *, step, hist, tps):
        return _ffn(x, norm_ffn_pre[:, None], w_up, ffn_conv_w, ffn_conv_b[:, None], w_down, norm_ffn_post[:, None],
                    state, layer=layer, step=step, hist=hist, tps=tps, tm=tm_ffn, tf=tf)

    proj, lg = _inproj(xp, row(norm_mix_pre[0]), w_in0, gate, n=n_main, tm=1024, tn=1024)
    a_out, _ = _sgu(proj, row(a_v_norm[0]), ws_prompt, bs_prompt, nblk=2)
    b_out, gla_p = _gla(proj, lg, jnp.zeros((bp, B_HEADS, B_KEY_DIM, B_VAL_DIM), F32), row(b_out_norm[0]),
                        bsz=bp, t=SEQ, clen=CHUNK, nch=4)
    xp = _outproj(a_out, b_out, w_out0, xp, row(norm_mix_post[0]), tm=tm)
    xp, ffn0_p = ffn_layer(xp, 0, jnp.zeros((bp, ffn_hist_p, D_FF), F32), step=1, hist=ffn_hist_p, tps=tps_ffn_p)
    proj = _inproj(xp, row(norm_mix_pre[1]), w_in1, n=D_MODEL, tm=1024, tn=1024)
    c_out, pool_tail_p = _pool(proj, jnp.zeros((bp, pool_hist_p, C_WIDTH), F32), cmap, row(c_scale[0]),
                               step=1, hist=pool_hist_p, tps=tps_p, tm=tm, pos0=0)
    zero_state = jnp.zeros((bp, 1, S5_CH), F32)
    d_out, s5re_tail_p, s5im_tail_p = _s5(proj, zero_state, zero_state, bblk, cblk, row(s5_d[0]), wglu, k_prompt,
                                          step=1, tps=tps_p, tm=tm)
    xp = _outproj(c_out, d_out, w_out1, xp, row(norm_mix_post[1]), tm=tm)
    xp, ffn1_p = ffn_layer(xp, 1, jnp.zeros((bp, ffn_hist_p, D_FF), F32), step=1, hist=ffn_hist_p, tps=tps_ffn_p)

    last = slice(tps_p - 1, None, tps_p)
    y_prompt = xp.reshape(bp, SEQ, D_MODEL)
    gla_prompt = gla_p[None]
    pool_prompt = pool_tail_p[last, pool_hist_p - POOL_BUF:][None]
    s5_re_prompt = s5re_tail_p[last, 0].reshape(1, bp, S5_GROUPS, S5_STATE)
    s5_im_prompt = s5im_tail_p[last, 0].reshape(1, bp, S5_GROUPS, S5_STATE)
    last_ffn = slice(tps_ffn_p - 1, None, tps_ffn_p)
    ffn_prompt = jnp.stack([ffn0_p[last_ffn, ffn_hist_p - 2:], ffn1_p[last_ffn, ffn_hist_p - 2:]])

    proj, lg = _inproj(xs, row(norm_mix_pre[0]), w_in0, gate, n=n_main, tm=1024, tn=1024)
    a_out, a_v = _sgu(proj, row(a_v_norm[0]), ws_sample, bs_sample, nblk=2)
    b_out, gla_s = _gla(proj, lg, state_gla[0], row(b_out_norm[0]), bsz=nb, t=ts, clen=ts, nch=1)
    xs = _outproj(a_out, b_out, w_out0, xs, row(norm_mix_post[0]), tm=tm)
    xs = _time_major(xs.reshape(nb, ts, D_MODEL))
    ffn_state = lambda layer: _time_major(state_ffn_conv[layer])[None]
    xs, ffn0_s = ffn_layer(xs, 0, ffn_state(0), step=step_s, hist=ffn_hist_s, tps=tps_ffn_s)
    proj = _inproj(xs, row(norm_mix_pre[1]), w_in1, n=D_MODEL, tm=1024, tn=1024)
    pool_state = jnp.pad(_time_major(state_pool[0]), ((step_s, 0), (0, 0)))[None]
    c_out, pool_tail_s = _pool(proj, pool_state, cmap, row(c_scale[0]),
                               step=step_s, hist=pool_hist_s, tps=tps_s, tm=tm, pos0=PAST_LEN)
    d_out, s5re_tail_s, s5im_tail_s = _s5(proj, state_s5_re[0].reshape(1, nb, S5_CH),
                                          state_s5_im[0].reshape(1, nb, S5_CH),
                                          bblk, cblk, row(s5_d[0]), wglu, k_sample, step=step_s, tps=tps_s, tm=tm)
    xs = _outproj(c_out, d_out, w_out1, xs, row(norm_mix_post[1]), tm=tm)
    xs, ffn1_s = ffn_layer(xs, 1, ffn_state(1), step=step_s, hist=ffn_hist_s, tps=tps_ffn_s)

    def batch_major(a, nt):
        return jnp.swapaxes(a.reshape(nt, nb, a.shape[-1]), 0, 1)

    y_sample = batch_major(xs, ts)
    gla_sample = gla_s[None]
    av_sample = a_v.reshape(1, nb, ts, A_WIDTH)
    pool_sample = batch_major(pool_tail_s[-1, step_s:], POOL_BUF)[None]
    s5_re_sample = s5re_tail_s[-1].reshape(1, nb, S5_GROUPS, S5_STATE)
    s5_im_sample = s5im_tail_s[-1].reshape(1, nb, S5_GROUPS, S5_STATE)
    ffn_sample = jnp.stack([batch_major(ffn0_s[-1], 2), batch_major(ffn1_s[-1], 2)])

    return (y_prompt, y_sample, gla_prompt, gla_sample, av_sample, pool_prompt, pool_sample,
            s5_re_prompt, s5_im_prompt, s5_re_sample, s5_im_sample, ffn_prompt, ffn_sample)
```

```python
import functools
import math

import jax
import jax.numpy as jnp
from jax import lax
from jax.experimental import pallas as pl
from jax.experimental.pallas import tpu as pltpu

F32 = jnp.float32
BF16 = jnp.bfloat16

D_MODEL = 2048
SEQ = 4096
DEC_BATCH = 32
DEC_SEQ = 32
PAST_LEN = 4096
CHUNK = 64
A_WIDTH = 1024
A_HEADS = 8
A_BLOCK = 128
B_HEADS = 4
B_KEY_DIM = 128
B_KEY_WIDTH = 512
B_VAL_DIM = 256
B_WIDTH = 1024
B_GATE_RANK = 16
B_GATE_TAU = 16.0
C_WIDTH = 1024
C_GROUP_DIM = 256
POOL_WINDOWS = (2, 4, 8, 16)
POOL_BUF = 15
D_WIDTH = 1024
S5_GROUPS = 64
S5_GROUP_DIM = 16
S5_STATE = 64
S5_CH = S5_GROUPS * S5_STATE
D_FF = 5632
EPS = 1e-6

LANES = 128
SUBLANES = 8
VMEM_LIMIT = 56 * 1024 * 1024


def _params(sem):
    return pltpu.CompilerParams(dimension_semantics=sem, vmem_limit_bytes=VMEM_LIMIT)


def _rms(x, g):
    return x * lax.rsqrt(jnp.mean(x * x, axis=-1, keepdims=True) + EPS) * g


def _dot(a, b):
    return jnp.dot(a, b, preferred_element_type=F32)


def _inproj_kernel(x_ref, g_ref, w_ref, *rest, with_gate):
    if with_gate:
        wlr_ref, wgate_ref, gbias_ref, o_ref, lg_ref, h_ref = rest
    else:
        o_ref, h_ref = rest
    j = pl.program_id(1)

    @pl.when(j == 0)
    def _():
        hb = _rms(x_ref[...], g_ref[...]).astype(BF16)
        h_ref[...] = hb
        if with_gate:
            glr = _dot(hb, wlr_ref[...])
            z = _dot(glr.astype(BF16), wgate_ref[...]) + gbias_ref[...]
            lg_ref[...] = (jnp.minimum(z, 0.0) - jnp.log(1.0 + jnp.exp(-jnp.abs(z)))) * (1.0 / B_GATE_TAU)

    o_ref[...] = _dot(h_ref[...], w_ref[...])


def _inproj(x, g, w, gate=None, *, n, tm, tn):
    m, d = x.shape
    grid = (m // tm, n // tn)
    in_specs = [pl.BlockSpec((tm, d), lambda i, j: (i, 0)),
                pl.BlockSpec((1, d), lambda i, j: (0, 0)),
                pl.BlockSpec((d, tn), lambda i, j: (0, j))]
    out_shape = [jax.ShapeDtypeStruct((m, n), F32)]
    out_specs = [pl.BlockSpec((tm, tn), lambda i, j: (i, j))]
    args = [x, g, w]
    if gate is not None:
        wlr, wgate, gbias = gate
        in_specs += [pl.BlockSpec(wlr.shape, lambda i, j: (0, 0)),
                     pl.BlockSpec(wgate.shape, lambda i, j: (0, 0)),
                     pl.BlockSpec(gbias.shape, lambda i, j: (0, 0))]
        out_shape.append(jax.ShapeDtypeStruct((m, B_KEY_WIDTH), F32))
        out_specs.append(pl.BlockSpec((tm, B_KEY_WIDTH), lambda i, j: (i, 0)))
        args += [wlr, wgate, gbias]
    res = pl.pallas_call(
        functools.partial(_inproj_kernel, with_gate=gate is not None),
        out_shape=out_shape, grid=grid, in_specs=in_specs, out_specs=out_specs,
        scratch_shapes=[pltpu.VMEM((tm, d), BF16)],
        compiler_params=_params(("parallel", "arbitrary")),
        name="inproj_gate" if gate is not None else "inproj",
    )(*args)
    return res if gate is not None else res[0]


def _sgu_kernel(u_ref, v_ref, gain_ref, w_ref, b_ref, o_ref, av_ref, *, nblk):
    for n in range(nblk):
        rows = slice(n * A_BLOCK, (n + 1) * A_BLOCK)
        v = jax.nn.gelu(v_ref[rows, :])
        mu = jnp.mean(v, axis=-1, keepdims=True)
        vc = v - mu
        vn = vc * lax.rsqrt(jnp.mean(vc * vc, axis=-1, keepdims=True) + EPS) * gain_ref[...]
        av_ref[rows, :] = vn
        vb = vn.astype(BF16)
        for h in range(A_HEADS):
            cols = slice(h * LANES, (h + 1) * LANES)
            s = _dot(w_ref[h], vb[:, cols]) + b_ref[:, h:h + 1]
            o_ref[rows, cols] = (jax.nn.gelu(u_ref[rows, cols]) * s).astype(o_ref.dtype)


def _sgu(proj, gain, w, b, *, nblk):
    m = proj.shape[0]
    tm = nblk * A_BLOCK
    return pl.pallas_call(
        functools.partial(_sgu_kernel, nblk=nblk),
        out_shape=[jax.ShapeDtypeStruct((m, A_WIDTH), BF16), jax.ShapeDtypeStruct((m, A_WIDTH), F32)],
        grid=(m // tm,),
        in_specs=[pl.BlockSpec((tm, A_WIDTH), lambda i: (i, 0)),
                  pl.BlockSpec((tm, A_WIDTH), lambda i: (i, 1)),
                  pl.BlockSpec((1, A_WIDTH), lambda i: (0, 0)),
                  pl.BlockSpec(w.shape, lambda i: (0, 0, 0)),
                  pl.BlockSpec(b.shape, lambda i: (0, 0))],
        out_specs=[pl.BlockSpec((tm, A_WIDTH), lambda i: (i, 0)),
                   pl.BlockSpec((tm, A_WIDTH), lambda i: (i, 0))],
        compiler_params=_params(("parallel",)),
        name="sgu",
    )(proj, proj, gain, w, b)


def _gla_kernel(q_ref, k_ref, v_ref, r_ref, lg_ref, s0_ref, og_ref, o_ref, sout_ref, st_ref, *, clen, ngrp, chain):
    c = pl.program_id(1)
    rows_all = ngrp * clen
    shift = int(math.log2(clen))

    if chain:
        @pl.when(c == 0)
        def _():
            for h in range(B_HEADS):
                st_ref[h] = s0_ref[0, h].T

    row_i = lax.broadcasted_iota(jnp.int32, (rows_all, rows_all), 0)
    col_i = lax.broadcasted_iota(jnp.int32, (rows_all, rows_all), 1)
    same_group = lax.shift_right_logical(row_i, shift) == lax.shift_right_logical(col_i, shift)
    tri = ((row_i >= col_i) & same_group).astype(F32)
    cum = jnp.dot(tri, lg_ref[...], precision=lax.Precision.HIGHEST,
                  preferred_element_type=F32)
    tots = [cum[(g + 1) * clen - 1:(g + 1) * clen, :] for g in range(ngrp)]
    tot_rows = jnp.concatenate([jnp.broadcast_to(t, (clen, B_KEY_WIDTH)) for t in tots], axis=0)
    kd = (k_ref[...] * jnp.exp(tot_rows - cum)).astype(BF16)
    qs = (q_ref[...] * (B_KEY_DIM ** -0.5)).astype(BF16)
    vb = v_ref[...].astype(BF16)
    sr = jax.nn.silu(r_ref[...])
    grp = lax.shift_right_logical(lax.broadcasted_iota(jnp.int32, (rows_all, 1), 0), shift)
    zero = jnp.zeros((), BF16)

    def by_group(x):
        return jnp.concatenate([jnp.where(grp == g, x, zero) for g in range(ngrp)], axis=1)

    for h in range(B_HEADS):
        kc = slice(h * B_KEY_DIM, (h + 1) * B_KEY_DIM)
        vc = slice(h * B_VAL_DIM, (h + 1) * B_VAL_DIM)
        upd = lax.dot_general(vb[:, vc], by_group(kd[:, kc]), (((0,), (0,)), ((), ())),
                              preferred_element_type=F32)
        states = []
        st = st_ref[h] if chain else None
        for g in range(ngrp):
            if not chain:
                st = s0_ref[g, h].T
            st = jnp.exp(tots[g][:, kc]) * st + upd[:, g * B_KEY_DIM:(g + 1) * B_KEY_DIM]
            states.append(st.astype(BF16))
            if not chain:
                sout_ref[g, h] = st.T
        if chain:
            st_ref[h] = st
        o = lax.dot_general(by_group(qs[:, kc]), jnp.concatenate(states, axis=1), (((1,), (1,)), ((), ())),
                            preferred_element_type=F32)
        o = o * lax.rsqrt(jnp.mean(o * o, axis=-1, keepdims=True) + EPS)
        o = o * og_ref[:, vc] * sr[:, vc]
        o_ref[:, vc] = o.astype(o_ref.dtype)

    if chain:
        @pl.when(c == pl.num_programs(1) - 1)
        def _():
            for h in range(B_HEADS):
                sout_ref[0, h] = st_ref[h].T


def _gla(proj, lg, s0, og, *, nseq, t, clen, ngrp, chain):
    rows = clen * ngrp
    m = nseq * t
    if chain:
        steps = t // rows
        grid = (nseq, steps)
        rmap = lambda b, c: b * steps + c
        nstate = 1
    else:
        grid = (m // rows, 1)
        rmap = lambda b, c: b
        nstate = ngrp
    blk = lambda width, col: pl.BlockSpec((rows, width), lambda b, c: (rmap(b, c), col))
    state_spec = pl.BlockSpec((nstate, B_HEADS, B_KEY_DIM, B_VAL_DIM), lambda b, c: (b, 0, 0, 0))
    return pl.pallas_call(
        functools.partial(_gla_kernel, clen=clen, ngrp=ngrp, chain=chain),
        out_shape=[jax.ShapeDtypeStruct((m, B_WIDTH), BF16),
                   jax.ShapeDtypeStruct((nseq, B_HEADS, B_KEY_DIM, B_VAL_DIM), F32)],
        grid=grid,
        in_specs=[blk(B_KEY_WIDTH, 4),
                  blk(B_KEY_WIDTH, 5),
                  blk(B_WIDTH, 3),
                  blk(B_WIDTH, 4),
                  blk(B_KEY_WIDTH, 0),
                  state_spec,
                  pl.BlockSpec((1, B_WIDTH), lambda b, c: (0, 0))],
        out_specs=[blk(B_WIDTH, 0), state_spec],
        scratch_shapes=[pltpu.VMEM((B_HEADS, B_VAL_DIM, B_KEY_DIM), F32)],
        compiler_params=_params(("parallel", "arbitrary")),
        name="gla",
    )(proj, proj, proj, proj, lg, s0, og)


def _outproj_kernel(a_ref, b_ref, w_ref, x_ref, g_ref, o_ref):
    ka = a_ref.shape[1]
    y = _dot(a_ref[...], w_ref[0:ka, :]) + _dot(b_ref[...], w_ref[ka:, :])
    o_ref[...] = x_ref[...] + _rms(y, g_ref[...])


def _outproj(a, b, w, x, g, *, tm):
    m, d = x.shape
    return pl.pallas_call(
        _outproj_kernel,
        out_shape=jax.ShapeDtypeStruct((m, d), F32),
        grid=(m // tm,),
        in_specs=[pl.BlockSpec((tm, a.shape[1]), lambda i: (i, 0)),
                  pl.BlockSpec((tm, b.shape[1]), lambda i: (i, 0)),
                  pl.BlockSpec(w.shape, lambda i: (0, 0)),
                  pl.BlockSpec((tm, d), lambda i: (i, 0)),
                  pl.BlockSpec((1, d), lambda i: (0, 0))],
        out_specs=pl.BlockSpec((tm, d), lambda i: (i, 0)),
        compiler_params=_params(("parallel",)),
        name="outproj",
    )(a, b, w, x, g)


def _ffn_kernel(x_ref, gpre_ref, wg_ref, wv_ref, cw_ref, cb_ref, wd_ref, gpost_ref, st_ref,
                o_ref, tail_ref, h_ref, gext_ref, carry_ref, act_a, act_b,
                *, step, hist, tps, tm, nf):
    i = pl.program_id(0)
    j = pl.program_id(1)
    first = (i % tps) == 0

    def up_and_gate(act_ref):
        hb = h_ref[...]
        gate = _dot(hb, wg_ref[...])
        val = _dot(hb, wv_ref[...])
        gext_ref[0:hist, :] = jnp.where(first, st_ref[0], carry_ref[j])
        gext_ref[hist:hist + tm, :] = gate
        prev2 = gext_ref[hist - 2 * step:hist - 2 * step + tm, :]
        prev1 = gext_ref[hist - step:hist - step + tm, :]
        conv = cb_ref[...] + cw_ref[0:1, :] * prev2 + cw_ref[1:2, :] * prev1 + cw_ref[2:3, :] * gate
        act_ref[...] = (jax.nn.gelu(conv) * val).astype(BF16)
        tail = gate[tm - hist:, :]
        carry_ref[j] = tail
        tail_ref[0] = tail

    def down(act_ref):
        o_ref[...] += _dot(act_ref[...], wd_ref[...])

    @pl.when(j == 0)
    def _():
        h_ref[...] = _rms(x_ref[...], gpre_ref[...]).astype(BF16)
        o_ref[...] = jnp.zeros_like(o_ref)

        @pl.when(i == 0)
        def _():
            carry_ref[...] = jnp.zeros_like(carry_ref)

        up_and_gate(act_a)

    for parity, (src, dst) in enumerate(((act_b, act_a), (act_a, act_b))):
        @pl.when((j > 0) & (j < nf) & (j % 2 == parity))
        def _(src=src, dst=dst):
            down(src)
            up_and_gate(dst)

    @pl.when(j == nf)
    def _():
        down(act_a if (nf - 1) % 2 == 0 else act_b)
        o_ref[...] = x_ref[...] + _rms(o_ref[...], gpost_ref[...])


def _ffn(x, gpre, w_up, cw, cb, w_down, gpost, state, *, layer, step, hist, tps, tm, tf):
    m, d = x.shape
    nf = D_FF // tf
    nm = m // tm
    up = lambda j: jnp.minimum(j, nf - 1)
    return pl.pallas_call(
        functools.partial(_ffn_kernel, step=step, hist=hist, tps=tps, tm=tm, nf=nf),
        out_shape=[jax.ShapeDtypeStruct((m, d), F32), jax.ShapeDtypeStruct((nm, hist, D_FF), F32)],
        grid=(nm, nf + 1),
        in_specs=[pl.BlockSpec((tm, d), lambda i, j: (i, 0)),
                  pl.BlockSpec((None, 1, d), lambda i, j: (layer, 0, 0)),
                  pl.BlockSpec((None, d, tf), lambda i, j: (layer, 0, up(j))),
                  pl.BlockSpec((None, d, tf), lambda i, j: (layer, 0, nf + up(j))),
                  pl.BlockSpec((None, 3, tf), lambda i, j: (layer, 0, up(j))),
                  pl.BlockSpec((None, 1, tf), lambda i, j: (layer, 0, up(j))),
                  pl.BlockSpec((None, tf, d), lambda i, j: (layer, jnp.maximum(j - 1, 0), 0)),
                  pl.BlockSpec((None, 1, d), lambda i, j: (layer, 0, 0)),
                  pl.BlockSpec((1, hist, tf), lambda i, j: (i // tps, 0, up(j)))],
        out_specs=[pl.BlockSpec((tm, d), lambda i, j: (i, 0), pipeline_mode=pl.Buffered(1)),
                   pl.BlockSpec((1, hist, tf), lambda i, j: (i, 0, up(j)))],
        scratch_shapes=[pltpu.VMEM((tm, d), BF16),
                        pltpu.VMEM((hist + tm, tf), F32), pltpu.VMEM((nf, hist, tf), F32),
                        pltpu.VMEM((tm, tf), BF16), pltpu.VMEM((tm, tf), BF16)],
        compiler_params=_params(("arbitrary", "arbitrary")),
        name="ffn",
    )(x, gpre, w_up, w_up, cw, cb, w_down, gpost, state)


def _pool_kernel(c_ref, st_ref, cmap_ref, cs_ref, o_ref, tail_ref, ext_ref, carry_ref,
                 *, step, hist, tps, tm, pos0):
    i = pl.program_id(0)
    first = (i % tps) == 0

    @pl.when(first)
    def _():
        ext_ref[0:hist, :] = st_ref[0]

    @pl.when(jnp.logical_not(first))
    def _():
        ext_ref[0:hist, :] = carry_ref[...]

    ext_ref[hist:hist + tm, :] = c_ref[...]
    row = lax.broadcasted_iota(jnp.int32, (tm, 1), 0)
    if step > 1:
        row = lax.shift_right_logical(row, int(math.log2(step)))
    pos = pos0 + (i % tps) * (tm // step) + row
    for g, win in enumerate(POOL_WINDOWS):
        cols = slice(g * C_GROUP_DIM, (g + 1) * C_GROUP_DIM)
        cur = ext_ref[hist:hist + tm, cols]
        tot = cur
        for k in range(1, win):
            tot = tot + ext_ref[hist - k * step:hist - k * step + tm, cols]
        cnt = jnp.minimum(pos + 1, win).astype(F32)
        delta = tot / cnt - cur
        y = _dot(delta.astype(BF16), cmap_ref[g]) * cs_ref[:, cols]
        o_ref[:, cols] = y.astype(o_ref.dtype)
    tail = ext_ref[tm:tm + hist, :]
    carry_ref[...] = tail
    tail_ref[0] = tail


def _pool(proj, state, cmap, cscale, *, step, hist, tps, tm, pos0):
    m = proj.shape[0]
    nm = m // tm
    return pl.pallas_call(
        functools.partial(_pool_kernel, step=step, hist=hist, tps=tps, tm=tm, pos0=pos0),
        out_shape=[jax.ShapeDtypeStruct((m, C_WIDTH), BF16), jax.ShapeDtypeStruct((nm, hist, C_WIDTH), F32)],
        grid=(nm,),
        in_specs=[pl.BlockSpec((tm, C_WIDTH), lambda i: (i, 0)),
                  pl.BlockSpec((1, hist, C_WIDTH), lambda i: (i // tps, 0, 0)),
                  pl.BlockSpec(cmap.shape, lambda i: (0, 0, 0)),
                  pl.BlockSpec((1, C_WIDTH), lambda i: (0, 0))],
        out_specs=[pl.BlockSpec((tm, C_WIDTH), lambda i: (i, 0)),
                   pl.BlockSpec((1, hist, C_WIDTH), lambda i: (i, 0, 0))],
        scratch_shapes=[pltpu.VMEM((hist + tm, C_WIDTH), F32), pltpu.VMEM((hist, C_WIDTH), F32)],
        compiler_params=_params(("arbitrary",)),
        name="pool",
    )(proj, state, cmap, cscale)


S5_CHUNK = 512
S5_NCHUNK = S5_CH // S5_CHUNK


def _s5_kernel(u_ref, s0re_ref, s0im_ref, bblk_ref, cblk_ref, dskip_ref, wglu_ref, kc_ref,
               o_ref, tre_ref, tim_ref, sre_ref, sim_ref, cre_ref, cim_ref, *, step, tps, tm):
    i = pl.program_id(0)
    crow = cre_ref.shape[0]

    ub = u_ref[...].astype(BF16)
    for m in range(S5_NCHUNK):
        r = _dot(ub[:, m * LANES:(m + 1) * LANES], bblk_ref[m])
        sre_ref[:, m * S5_CHUNK:(m + 1) * S5_CHUNK] = r[:, :S5_CHUNK]
        sim_ref[:, m * S5_CHUNK:(m + 1) * S5_CHUNK] = r[:, S5_CHUNK:]

    @pl.when((i % tps) == 0)
    def _():
        cre_ref[...] = jnp.broadcast_to(s0re_ref[0], cre_ref.shape) if step == 1 else s0re_ref[0]
        cim_ref[...] = jnp.broadcast_to(s0im_ref[0], cim_ref.shape) if step == 1 else s0im_ref[0]

    for m in range(S5_NCHUNK):
        cols = slice(m * S5_CHUNK, (m + 1) * S5_CHUNK)
        if step == 1:
            consts = [kc_ref[k, :, cols] for k in range(8)]
            a1r, a1i, a2r, a2i, a4r, a4i, pwr, pwi = consts

            def body(rb, carry):
                cr, ci = carry
                r0 = pl.multiple_of(rb * SUBLANES, SUBLANES)
                xr = sre_ref[pl.ds(r0, SUBLANES), cols]
                xi = sim_ref[pl.ds(r0, SUBLANES), cols]
                for ar, ai, d in ((a1r, a1i, 1), (a2r, a2i, 2), (a4r, a4i, 4)):
                    sr = pltpu.roll(xr, d, 0)
                    si = pltpu.roll(xi, d, 0)
                    xr, xi = xr + (ar * sr - ai * si), xi + (ar * si + ai * sr)
                xr, xi = xr + (pwr * cr - pwi * ci), xi + (pwr * ci + pwi * cr)
                sre_ref[pl.ds(r0, SUBLANES), cols] = xr
                sim_ref[pl.ds(r0, SUBLANES), cols] = xi
                return (jnp.broadcast_to(xr[SUBLANES - 1:SUBLANES, :], xr.shape),
                        jnp.broadcast_to(xi[SUBLANES - 1:SUBLANES, :], xi.shape))

            cr, ci = lax.fori_loop(0, tm // SUBLANES, body, (cre_ref[:, cols], cim_ref[:, cols]))
        else:
            lr = jnp.broadcast_to(kc_ref[0, 0:1, cols], (crow, S5_CHUNK))
            li = jnp.broadcast_to(kc_ref[1, 0:1, cols], (crow, S5_CHUNK))

            def body(t, carry):
                cr, ci = carry
                r0 = pl.multiple_of(t * step, step)
                xr = sre_ref[pl.ds(r0, step), cols] + (lr * cr - li * ci)
                xi = sim_ref[pl.ds(r0, step), cols] + (lr * ci + li * cr)
                sre_ref[pl.ds(r0, step), cols] = xr
                sim_ref[pl.ds(r0, step), cols] = xi
                return xr, xi

            cr, ci = lax.fori_loop(0, tm // step, body, (cre_ref[:, cols], cim_ref[:, cols]))
        cre_ref[:, cols] = cr
        cim_ref[:, cols] = ci

    tre_ref[0] = cre_ref[...]
    tim_ref[0] = cim_ref[...]

    ys = []
    for m in range(S5_NCHUNK):
        cols = slice(m * S5_CHUNK, (m + 1) * S5_CHUNK)
        ys.append(_dot(sre_ref[:, cols].astype(BF16), cblk_ref[m, 0:S5_CHUNK, :])
                  + _dot(sim_ref[:, cols].astype(BF16), cblk_ref[m, S5_CHUNK:, :]))
    y = jnp.concatenate(ys, axis=1) + dskip_ref[...] * u_ref[...]
    z = _dot(jax.nn.gelu(y).astype(BF16), wglu_ref[...])
    o_ref[...] = (z[:, :D_WIDTH] * jax.nn.sigmoid(z[:, D_WIDTH:])).astype(o_ref.dtype)


def _s5(proj, s0re, s0im, bblk, cblk, dskip, wglu, kconst, *, step, tps, tm):
    m = proj.shape[0]
    nm = m // tm
    crow = s0re.shape[1] if step > 1 else SUBLANES
    srow = s0re.shape[1]
    return pl.pallas_call(
        functools.partial(_s5_kernel, step=step, tps=tps, tm=tm),
        out_shape=[jax.ShapeDtypeStruct((m, D_WIDTH), BF16),
                   jax.ShapeDtypeStruct((nm, crow, S5_CH), F32),
                   jax.ShapeDtypeStruct((nm, crow, S5_CH), F32)],
        grid=(nm,),
        in_specs=[pl.BlockSpec((tm, D_WIDTH), lambda i: (i, 1)),
                  pl.BlockSpec((1, srow, S5_CH), lambda i: (i // tps, 0, 0)),
                  pl.BlockSpec((1, srow, S5_CH), lambda i: (i // tps, 0, 0)),
                  pl.BlockSpec(bblk.shape, lambda i: (0, 0, 0)),
                  pl.BlockSpec(cblk.shape, lambda i: (0, 0, 0)),
                  pl.BlockSpec((1, D_WIDTH), lambda i: (0, 0)),
                  pl.BlockSpec(wglu.shape, lambda i: (0, 0)),
                  pl.BlockSpec(kconst.shape, lambda i: (0, 0, 0))],
        out_specs=[pl.BlockSpec((tm, D_WIDTH), lambda i: (i, 0)),
                   pl.BlockSpec((1, crow, S5_CH), lambda i: (i, 0, 0)),
                   pl.BlockSpec((1, crow, S5_CH), lambda i: (i, 0, 0))],
        scratch_shapes=[pltpu.VMEM((tm, S5_CH), F32), pltpu.VMEM((tm, S5_CH), F32),
                        pltpu.VMEM((crow, S5_CH), F32), pltpu.VMEM((crow, S5_CH), F32)],
        compiler_params=_params(("arbitrary",)),
        name="s5",
    )(proj, s0re, s0im, bblk, cblk, dskip, wglu, kconst)


def _block_diag(blocks, per):
    n, r, c = blocks.shape
    b = blocks.reshape(n // per, per, r, c)
    eye = jnp.eye(per, dtype=blocks.dtype)
    return jnp.einsum('mgrc,gh->mgrhc', b, eye).reshape(n // per, per * r, per * c)


def _s5_constants(a_re, a_im, log_dt, b_re, b_im, c_re, c_im):
    dt = jnp.exp(log_dt)[:, None]
    zr, zi = a_re * dt, a_im * dt
    mag = jnp.exp(zr)
    lr, li = mag * jnp.cos(zi), mag * jnp.sin(zi)
    den = a_re * a_re + a_im * a_im
    nr, ni = lr - 1.0, li
    kr, ki = (nr * a_re + ni * a_im) / den, (ni * a_re - nr * a_im) / den
    bbr = kr[..., None] * b_re - ki[..., None] * b_im
    bbi = kr[..., None] * b_im + ki[..., None] * b_re
    per = S5_CHUNK // S5_STATE
    bblk = jnp.concatenate([_block_diag(jnp.swapaxes(bbr, 1, 2), per),
                            _block_diag(jnp.swapaxes(bbi, 1, 2), per)], axis=2).astype(BF16)
    cblk = jnp.concatenate([_block_diag(jnp.swapaxes(c_re, 1, 2), per),
                            _block_diag(jnp.swapaxes(-c_im, 1, 2), per)], axis=1).astype(BF16)
    lr, li = lr.reshape(1, S5_CH), li.reshape(1, S5_CH)
    pr, pi = [lr], [li]
    for _ in range(SUBLANES - 1):
        pr, pi = pr + [pr[-1] * lr - pi[-1] * li], pi + [pr[-1] * li + pi[-1] * lr]
    rowid = jnp.arange(SUBLANES)[:, None]

    def masked(p, d):
        return jnp.where(rowid >= d, jnp.broadcast_to(p[d - 1], (SUBLANES, S5_CH)), 0.0)

    k_prompt = jnp.stack([masked(pr, 1), masked(pi, 1), masked(pr, 2), masked(pi, 2),
                          masked(pr, 4), masked(pi, 4),
                          jnp.concatenate(pr, axis=0), jnp.concatenate(pi, axis=0)])
    k_sample = jnp.stack([jnp.broadcast_to(lr, (SUBLANES, S5_CH)), jnp.broadcast_to(li, (SUBLANES, S5_CH))])
    return bblk, cblk, k_prompt, k_sample


def _time_major(a):
    a = jnp.swapaxes(a, 0, 1)
    return a.reshape((a.shape[0] * a.shape[1],) + a.shape[2:])


def kernel(x_prompt, x_sample, state_gla, state_pool, state_s5_re, state_s5_im, state_ffn_conv, norm_mix_pre, norm_mix_post, norm_ffn_pre, norm_ffn_post, w_in_even, a_w_s, a_b_s, a_v_norm, b_w_gate, b_gate_bias, b_out_norm, w_out_even, w_in_odd, c_map, c_scale, s5_a_re, s5_a_im, s5_log_dt, s5_b_re, s5_b_im, s5_c_re, s5_c_im, s5_d, s5_w_glu, w_out_odd, ffn_w_up, ffn_conv_w, ffn_conv_b, ffn_w_down):
    bp = x_prompt.shape[0]
    nb, ts = x_sample.shape[0], x_sample.shape[1]
    xp = x_prompt.reshape(bp * SEQ, D_MODEL)
    xs = x_sample.reshape(nb * ts, D_MODEL)

    row = lambda v: v.reshape(1, -1)
    n_main = 2 * A_WIDTH + 2 * B_KEY_WIDTH + 2 * B_WIDTH
    w_in0 = w_in_even[0].astype(BF16)
    w_lr = jnp.pad(w_in_even[0][:, n_main:], ((0, 0), (0, LANES - B_GATE_RANK))).astype(BF16)
    w_gate = jnp.pad(b_w_gate[0], ((0, LANES - B_GATE_RANK), (0, 0))).astype(BF16)
    gate = (w_lr, w_gate, row(b_gate_bias[0]))
    pos = jnp.arange(A_BLOCK)
    causal = (pos[None, :] // CHUNK) <= (pos[:, None] // CHUNK)
    ws_prompt = jnp.where(causal[None], a_w_s[0], 0.0).astype(BF16)
    per = A_BLOCK // ts
    ws_small = jnp.where(causal[None, :ts, :ts], a_w_s[0][:, :ts, :ts], 0.0)
    ws_sample = jnp.einsum('hij,ab->haibj', ws_small, jnp.eye(per, dtype=F32)).reshape(A_HEADS, A_BLOCK, A_BLOCK).astype(BF16)
    bs_prompt = a_b_s[0].T
    bs_sample = jnp.tile(a_b_s[0][:, :ts].T, (per, 1))
    w_out0 = w_out_even[0].astype(BF16)
    w_in1 = w_in_odd[0].astype(BF16)
    cmap = c_map[0].astype(BF16)
    bblk, cblk, k_prompt, k_sample = _s5_constants(s5_a_re[0], s5_a_im[0], s5_log_dt[0], s5_b_re[0], s5_b_im[0],
                                                   s5_c_re[0], s5_c_im[0])
    wglu = s5_w_glu[0].astype(BF16)
    w_out1 = w_out_odd[0].astype(BF16)
    w_up = ffn_w_up.astype(BF16)
    w_down = ffn_w_down.astype(BF16)

    tm = 512
    tf = 512
    tps_p = SEQ // tm
    ffn_hist_p = SUBLANES
    pool_hist_p = 2 * SUBLANES
    step_s = nb
    tps_s = (nb * ts) // tm
    ffn_hist_s = 2 * step_s
    pool_hist_s = (POOL_BUF + 1) * step_s

    tm_ffn = 1024
    tps_ffn_p = SEQ // tm_ffn
    tps_ffn_s = (nb * ts) // tm_ffn

    def ffn_layer(x, layer, state, *, step, hist, tps):
        return _ffn(x, norm_ffn_pre[:, None], w_up, ffn_conv_w, ffn_conv_b[:, None], w_down, norm_ffn_post[:, None],
                    state, layer=layer, step=step, hist=hist, tps=tps, tm=tm_ffn, tf=tf)

    proj, lg = _inproj(xp, row(norm_mix_pre[0]), w_in0, gate, n=n_main, tm=1024, tn=1024)
    a_out, _ = _sgu(proj, row(a_v_norm[0]), ws_prompt, bs_prompt, nblk=2)
    b_out, gla_p = _gla(proj, lg, jnp.zeros((bp, B_HEADS, B_KEY_DIM, B_VAL_DIM), F32), row(b_out_norm[0]),
                        nseq=bp, t=SEQ, clen=CHUNK, ngrp=4, chain=True)
    xp = _outproj(a_out, b_out, w_out0, xp, row(norm_mix_post[0]), tm=tm)
    xp, ffn0_p = ffn_layer(xp, 0, jnp.zeros((bp, ffn_hist_p, D_FF), F32), step=1, hist=ffn_hist_p, tps=tps_ffn_p)
    proj = _inproj(xp, row(norm_mix_pre[1]), w_in1, n=D_MODEL, tm=1024, tn=1024)
    c_out, pool_tail_p = _pool(proj, jnp.zeros((bp, pool_hist_p, C_WIDTH), F32), cmap, row(c_scale[0]),
                               step=1, hist=pool_hist_p, tps=tps_p, tm=tm, pos0=0)
    zero_state = jnp.zeros((bp, 1, S5_CH), F32)
    d_out, s5re_tail_p, s5im_tail_p = _s5(proj, zero_state, zero_state, bblk, cblk, row(s5_d[0]), wglu, k_prompt,
                                          step=1, tps=tps_p, tm=tm)
    xp = _outproj(c_out, d_out, w_out1, xp, row(norm_mix_post[1]), tm=tm)
    xp, ffn1_p = ffn_layer(xp, 1, jnp.zeros((bp, ffn_hist_p, D_FF), F32), step=1, hist=ffn_hist_p, tps=tps_ffn_p)

    last = slice(tps_p - 1, None, tps_p)
    y_prompt = xp.reshape(bp, SEQ, D_MODEL)
    gla_prompt = gla_p[None]
    pool_prompt = pool_tail_p[last, pool_hist_p - POOL_BUF:][None]
    s5_re_prompt = s5re_tail_p[last, 0].reshape(1, bp, S5_GROUPS, S5_STATE)
    s5_im_prompt = s5im_tail_p[last, 0].reshape(1, bp, S5_GROUPS, S5_STATE)
    last_ffn = slice(tps_ffn_p - 1, None, tps_ffn_p)
    ffn_prompt = jnp.stack([ffn0_p[last_ffn, ffn_hist_p - 2:], ffn1_p[last_ffn, ffn_hist_p - 2:]])

    proj, lg = _inproj(xs, row(norm_mix_pre[0]), w_in0, gate, n=n_main, tm=1024, tn=1024)
    a_out, a_v = _sgu(proj, row(a_v_norm[0]), ws_sample, bs_sample, nblk=2)
    b_out, gla_s = _gla(proj, lg, state_gla[0], row(b_out_norm[0]), nseq=nb, t=ts, clen=ts, ngrp=8, chain=False)
    xs = _outproj(a_out, b_out, w_out0, xs, row(norm_mix_post[0]), tm=tm)
    xs = _time_major(xs.reshape(nb, ts, D_MODEL))
    ffn_state = lambda layer: _time_major(state_ffn_conv[layer])[None]
    xs, ffn0_s = ffn_layer(xs, 0, ffn_state(0), step=step_s, hist=ffn_hist_s, tps=tps_ffn_s)
    proj = _inproj(xs, row(norm_mix_pre[1]), w_in1, n=D_MODEL, tm=1024, tn=1024)
    pool_state = jnp.pad(_time_major(state_pool[0]), ((step_s, 0), (0, 0)))[None]
    c_out, pool_tail_s = _pool(proj, pool_state, cmap, row(c_scale[0]),
                               step=step_s, hist=pool_hist_s, tps=tps_s, tm=tm, pos0=PAST_LEN)
    d_out, s5re_tail_s, s5im_tail_s = _s5(proj, state_s5_re[0].reshape(1, nb, S5_CH),
                                          state_s5_im[0].reshape(1, nb, S5_CH),
                                          bblk, cblk, row(s5_d[0]), wglu, k_sample, step=step_s, tps=tps_s, tm=tm)
    xs = _outproj(c_out, d_out, w_out1, xs, row(norm_mix_post[1]), tm=tm)
    xs, ffn1_s = ffn_layer(xs, 1, ffn_state(1), step=step_s, hist=ffn_hist_s, tps=tps_ffn_s)

    def batch_major(a, nt):
        return jnp.swapaxes(a.reshape(nt, nb, a.shape[-1]), 0, 1)

    y_sample = batch_major(xs, ts)
    gla_sample = gla_s[None]
    av_sample = a_v.reshape(1, nb, ts, A_WIDTH)
    pool_sample = batch_major(pool_tail_s[-1, step_s:], POOL_BUF)[None]
    s5_re_sample = s5re_tail_s[-1].reshape(1, nb, S5_GROUPS, S5_STATE)
    s5_im_sample = s5im_tail_s[-1].reshape(1, nb, S5_GROUPS, S5_STATE)
    ffn_sample = jnp.stack([batch_major(ffn0_s[-1], 2), batch_major(ffn1_s[-1], 2)])

    return (y_prompt, y_sample, gla_prompt, gla_sample, av_sample, pool_prompt, pool_sample,
            s5_re_prompt, s5_im_prompt, s5_re_sample, s5_im_sample, ffn_prompt, ffn_sample)
```

```python
import functools
import math

import jax
import jax.numpy as jnp
from jax import lax
from jax.experimental import pallas as pl
from jax.experimental.pallas import tpu as pltpu

F32 = jnp.float32
BF16 = jnp.bfloat16

D_MODEL = 2048
SEQ = 4096
DEC_BATCH = 32
DEC_SEQ = 32
PAST_LEN = 4096
CHUNK = 64
A_WIDTH = 1024
A_HEADS = 8
A_BLOCK = 128
B_HEADS = 4
B_KEY_DIM = 128
B_KEY_WIDTH = 512
B_VAL_DIM = 256
B_WIDTH = 1024
B_GATE_RANK = 16
B_GATE_TAU = 16.0
C_WIDTH = 1024
C_GROUP_DIM = 256
POOL_WINDOWS = (2, 4, 8, 16)
POOL_BUF = 15
D_WIDTH = 1024
S5_GROUPS = 64
S5_GROUP_DIM = 16
S5_STATE = 64
S5_CH = S5_GROUPS * S5_STATE
D_FF = 5632
EPS = 1e-6

LANES = 128
SUBLANES = 8
VMEM_LIMIT = 56 * 1024 * 1024


def _params(sem):
    return pltpu.CompilerParams(dimension_semantics=sem, vmem_limit_bytes=VMEM_LIMIT)


def _rms(x, g):
    return x * lax.rsqrt(jnp.mean(x * x, axis=-1, keepdims=True) + EPS) * g


def _dot(a, b):
    return jnp.dot(a, b, preferred_element_type=F32)


def _inproj_kernel(x_ref, g_ref, w_ref, *rest, with_gate):
    if with_gate:
        wlr_ref, wgate_ref, gbias_ref, o_ref, lg_ref, h_ref = rest
    else:
        o_ref, h_ref = rest
    j = pl.program_id(1)

    @pl.when(j == 0)
    def _():
        hb = _rms(x_ref[...], g_ref[...]).astype(BF16)
        h_ref[...] = hb
        if with_gate:
            glr = _dot(hb, wlr_ref[...])
            z = _dot(glr.astype(BF16), wgate_ref[...]) + gbias_ref[...]
            lg_ref[...] = (jnp.minimum(z, 0.0) - jnp.log(1.0 + jnp.exp(-jnp.abs(z)))) * (1.0 / B_GATE_TAU)

    o_ref[...] = _dot(h_ref[...], w_ref[...])


def _inproj(x, g, w, gate=None, *, n, tm, tn):
    m, d = x.shape
    grid = (m // tm, n // tn)
    in_specs = [pl.BlockSpec((tm, d), lambda i, j: (i, 0)),
                pl.BlockSpec((1, d), lambda i, j: (0, 0)),
                pl.BlockSpec((d, tn), lambda i, j: (0, j))]
    out_shape = [jax.ShapeDtypeStruct((m, n), F32)]
    out_specs = [pl.BlockSpec((tm, tn), lambda i, j: (i, j))]
    args = [x, g, w]
    if gate is not None:
        wlr, wgate, gbias = gate
        in_specs += [pl.BlockSpec(wlr.shape, lambda i, j: (0, 0)),
                     pl.BlockSpec(wgate.shape, lambda i, j: (0, 0)),
                     pl.BlockSpec(gbias.shape, lambda i, j: (0, 0))]
        out_shape.append(jax.ShapeDtypeStruct((m, B_KEY_WIDTH), F32))
        out_specs.append(pl.BlockSpec((tm, B_KEY_WIDTH), lambda i, j: (i, 0)))
        args += [wlr, wgate, gbias]
    res = pl.pallas_call(
        functools.partial(_inproj_kernel, with_gate=gate is not None),
        out_shape=out_shape, grid=grid, in_specs=in_specs, out_specs=out_specs,
        scratch_shapes=[pltpu.VMEM((tm, d), BF16)],
        compiler_params=_params(("parallel", "arbitrary")),
        name="inproj_gate" if gate is not None else "inproj",
    )(*args)
    return res if gate is not None else res[0]


def _sgu_kernel(u_ref, v_ref, gain_ref, w_ref, b_ref, o_ref, *av_ref, nblk):
    for n in range(nblk):
        rows = slice(n * A_BLOCK, (n + 1) * A_BLOCK)
        v = jax.nn.gelu(v_ref[rows, :])
        mu = jnp.mean(v, axis=-1, keepdims=True)
        vc = v - mu
        vn = vc * lax.rsqrt(jnp.mean(vc * vc, axis=-1, keepdims=True) + EPS) * gain_ref[...]
        if av_ref:
            av_ref[0][rows, :] = vn
        vb = vn.astype(BF16)
        for h in range(A_HEADS):
            cols = slice(h * LANES, (h + 1) * LANES)
            s = _dot(w_ref[h], vb[:, cols]) + b_ref[:, h:h + 1]
            o_ref[rows, cols] = (jax.nn.gelu(u_ref[rows, cols]) * s).astype(o_ref.dtype)


def _sgu(proj, gain, w, b, *, nblk, emit_av):
    m = proj.shape[0]
    tm = nblk * A_BLOCK
    n_out = 2 if emit_av else 1
    return pl.pallas_call(
        functools.partial(_sgu_kernel, nblk=nblk),
        out_shape=[jax.ShapeDtypeStruct((m, A_WIDTH), BF16), jax.ShapeDtypeStruct((m, A_WIDTH), F32)][:n_out],
        grid=(m // tm,),
        in_specs=[pl.BlockSpec((tm, A_WIDTH), lambda i: (i, 0)),
                  pl.BlockSpec((tm, A_WIDTH), lambda i: (i, 1)),
                  pl.BlockSpec((1, A_WIDTH), lambda i: (0, 0)),
                  pl.BlockSpec(w.shape, lambda i: (0, 0, 0)),
                  pl.BlockSpec(b.shape, lambda i: (0, 0))],
        out_specs=[pl.BlockSpec((tm, A_WIDTH), lambda i: (i, 0)),
                   pl.BlockSpec((tm, A_WIDTH), lambda i: (i, 0))][:n_out],
        compiler_params=_params(("parallel",)),
        name="sgu",
    )(proj, proj, gain, w, b)


def _gla_kernel(q_ref, k_ref, v_ref, r_ref, lg_ref, s0_ref, og_ref, o_ref, sout_ref, st_ref, *, clen, ngrp, chain):
    c = pl.program_id(1)
    rows_all = ngrp * clen
    shift = int(math.log2(clen))

    if chain:
        @pl.when(c == 0)
        def _():
            for h in range(B_HEADS):
                st_ref[h] = s0_ref[0, h].T

    row_i = lax.broadcasted_iota(jnp.int32, (rows_all, rows_all), 0)
    col_i = lax.broadcasted_iota(jnp.int32, (rows_all, rows_all), 1)
    same_group = lax.shift_right_logical(row_i, shift) == lax.shift_right_logical(col_i, shift)
    tri = ((row_i >= col_i) & same_group).astype(F32)
    cum = jnp.dot(tri, lg_ref[...], precision=lax.Precision.HIGHEST,
                  preferred_element_type=F32)
    tots = [cum[(g + 1) * clen - 1:(g + 1) * clen, :] for g in range(ngrp)]
    tot_rows = jnp.concatenate([jnp.broadcast_to(t, (clen, B_KEY_WIDTH)) for t in tots], axis=0)
    kd = (k_ref[...] * jnp.exp(tot_rows - cum)).astype(BF16)
    qs = (q_ref[...] * (B_KEY_DIM ** -0.5)).astype(BF16)
    vb = v_ref[...].astype(BF16)
    sr = jax.nn.silu(r_ref[...])
    grp = lax.shift_right_logical(lax.broadcasted_iota(jnp.int32, (rows_all, 1), 0), shift)
    zero = jnp.zeros((), BF16)

    def by_group(x):
        return jnp.concatenate([jnp.where(grp == g, x, zero) for g in range(ngrp)], axis=1)

    for h in range(B_HEADS):
        kc = slice(h * B_KEY_DIM, (h + 1) * B_KEY_DIM)
        vc = slice(h * B_VAL_DIM, (h + 1) * B_VAL_DIM)
        upd = lax.dot_general(vb[:, vc], by_group(kd[:, kc]), (((0,), (0,)), ((), ())),
                              preferred_element_type=F32)
        states = []
        st = st_ref[h] if chain else None
        for g in range(ngrp):
            if not chain:
                st = s0_ref[g, h].T
            st = jnp.exp(tots[g][:, kc]) * st + upd[:, g * B_KEY_DIM:(g + 1) * B_KEY_DIM]
            states.append(st.astype(BF16))
            if not chain:
                sout_ref[g, h] = st.T
        if chain:
            st_ref[h] = st
        o = lax.dot_general(by_group(qs[:, kc]), jnp.concatenate(states, axis=1), (((1,), (1,)), ((), ())),
                            preferred_element_type=F32)
        o = o * lax.rsqrt(jnp.mean(o * o, axis=-1, keepdims=True) + EPS)
        o = o * og_ref[:, vc] * sr[:, vc]
        o_ref[:, vc] = o.astype(o_ref.dtype)

    if chain:
        @pl.when(c == pl.num_programs(1) - 1)
        def _():
            for h in range(B_HEADS):
                sout_ref[0, h] = st_ref[h].T


def _gla(proj, lg, s0, og, *, nseq, t, clen, ngrp, chain):
    rows = clen * ngrp
    m = nseq * t
    if chain:
        steps = t // rows
        grid = (nseq, steps)
        rmap = lambda b, c: b * steps + c
        nstate = 1
    else:
        grid = (m // rows, 1)
        rmap = lambda b, c: b
        nstate = ngrp
    blk = lambda width, col: pl.BlockSpec((rows, width), lambda b, c: (rmap(b, c), col))
    state_spec = pl.BlockSpec((nstate, B_HEADS, B_KEY_DIM, B_VAL_DIM), lambda b, c: (b, 0, 0, 0))
    return pl.pallas_call(
        functools.partial(_gla_kernel, clen=clen, ngrp=ngrp, chain=chain),
        out_shape=[jax.ShapeDtypeStruct((m, B_WIDTH), BF16),
                   jax.ShapeDtypeStruct((nseq, B_HEADS, B_KEY_DIM, B_VAL_DIM), F32)],
        grid=grid,
        in_specs=[blk(B_KEY_WIDTH, 4),
                  blk(B_KEY_WIDTH, 5),
                  blk(B_WIDTH, 3),
                  blk(B_WIDTH, 4),
                  blk(B_KEY_WIDTH, 0),
                  state_spec,
                  pl.BlockSpec((1, B_WIDTH), lambda b, c: (0, 0))],
        out_specs=[blk(B_WIDTH, 0), state_spec],
        scratch_shapes=[pltpu.VMEM((B_HEADS, B_VAL_DIM, B_KEY_DIM), F32)],
        compiler_params=_params(("parallel", "arbitrary")),
        name="gla",
    )(proj, proj, proj, proj, lg, s0, og)


def _outproj_kernel(a_ref, b_ref, w_ref, x_ref, g_ref, o_ref):
    ka = a_ref.shape[1]
    y = _dot(a_ref[...], w_ref[0:ka, :]) + _dot(b_ref[...], w_ref[ka:, :])
    o_ref[...] = x_ref[...] + _rms(y, g_ref[...])


def _outproj(a, b, w, x, g, *, tm):
    m, d = x.shape
    return pl.pallas_call(
        _outproj_kernel,
        out_shape=jax.ShapeDtypeStruct((m, d), F32),
        grid=(m // tm,),
        in_specs=[pl.BlockSpec((tm, a.shape[1]), lambda i: (i, 0)),
                  pl.BlockSpec((tm, b.shape[1]), lambda i: (i, 0)),
                  pl.BlockSpec(w.shape, lambda i: (0, 0)),
                  pl.BlockSpec((tm, d), lambda i: (i, 0)),
                  pl.BlockSpec((1, d), lambda i: (0, 0))],
        out_specs=pl.BlockSpec((tm, d), lambda i: (i, 0)),
        compiler_params=_params(("parallel",)),
        name="outproj",
    )(a, b, w, x, g)


def _ffn_kernel(x_ref, gpre_ref, wg_ref, wv_ref, cw_ref, cb_ref, wd_ref, gpost_ref, st_ref,
                o_ref, tail_ref, *rest, step, hist, tps, tm, nf, convert):
    if convert:
        wg_out, wv_out, wd_out, h_ref, gext_ref, carry_ref, act_a, act_b = rest
    else:
        h_ref, gext_ref, carry_ref, act_a, act_b = rest
    i = pl.program_id(0)
    j = pl.program_id(1)
    first = (i % tps) == 0

    def up_and_gate(act_ref):
        hb = h_ref[...]
        wg, wv = wg_ref[...], wv_ref[...]
        if convert:
            wg, wv = wg.astype(BF16), wv.astype(BF16)
            wg_out[...] = wg
            wv_out[...] = wv
        gate = _dot(hb, wg)
        val = _dot(hb, wv)
        gext_ref[0:hist, :] = jnp.where(first, st_ref[0], carry_ref[j])
        gext_ref[hist:hist + tm, :] = gate
        prev2 = gext_ref[hist - 2 * step:hist - 2 * step + tm, :]
        prev1 = gext_ref[hist - step:hist - step + tm, :]
        conv = cb_ref[...] + cw_ref[0:1, :] * prev2 + cw_ref[1:2, :] * prev1 + cw_ref[2:3, :] * gate
        act_ref[...] = (jax.nn.gelu(conv) * val).astype(BF16)
        tail = gate[tm - hist:, :]
        carry_ref[j] = tail
        tail_ref[0] = tail

    def down(act_ref):
        wd = wd_ref[...]
        if convert:
            wd = wd.astype(BF16)
            wd_out[...] = wd
        o_ref[...] += _dot(act_ref[...], wd)

    @pl.when(j == 0)
    def _():
        h_ref[...] = _rms(x_ref[...], gpre_ref[...]).astype(BF16)
        o_ref[...] = jnp.zeros_like(o_ref)

        @pl.when(i == 0)
        def _():
            carry_ref[...] = jnp.zeros_like(carry_ref)

        up_and_gate(act_a)

    for parity, (src, dst) in enumerate(((act_b, act_a), (act_a, act_b))):
        @pl.when((j > 0) & (j < nf) & (j % 2 == parity))
        def _(src=src, dst=dst):
            down(src)
            up_and_gate(dst)

    @pl.when(j == nf)
    def _():
        down(act_a if (nf - 1) % 2 == 0 else act_b)
        o_ref[...] = x_ref[...] + _rms(o_ref[...], gpost_ref[...])


def _ffn(x, gpre, weights, cw, cb, gpost, state, *, layer, step, hist, tps, tm, tf, convert):
    m, d = x.shape
    nf = D_FF // tf
    nm = m // tm
    up = lambda j: jnp.minimum(j, nf - 1)
    down = lambda j: jnp.maximum(j - 1, 0)
    if convert:
        w_up, w_down = weights
        w_args = (w_up, w_up, w_down)
        w_specs = [pl.BlockSpec((None, d, tf), lambda i, j: (layer, 0, up(j))),
                   pl.BlockSpec((None, d, tf), lambda i, j: (layer, 0, nf + up(j))),
                   pl.BlockSpec((None, tf, d), lambda i, j: (layer, down(j), 0))]
        extra_shapes = [jax.ShapeDtypeStruct((d, D_FF), BF16), jax.ShapeDtypeStruct((d, D_FF), BF16),
                        jax.ShapeDtypeStruct((D_FF, d), BF16)]
        extra_specs = [pl.BlockSpec((d, tf), lambda i, j: (0, up(j))),
                       pl.BlockSpec((d, tf), lambda i, j: (0, up(j))),
                       pl.BlockSpec((tf, d), lambda i, j: (down(j), 0))]
    else:
        w_args = weights
        w_specs = [pl.BlockSpec((d, tf), lambda i, j: (0, up(j))),
                   pl.BlockSpec((d, tf), lambda i, j: (0, up(j))),
                   pl.BlockSpec((tf, d), lambda i, j: (down(j), 0))]
        extra_shapes, extra_specs = [], []
    return pl.pallas_call(
        functools.partial(_ffn_kernel, step=step, hist=hist, tps=tps, tm=tm, nf=nf, convert=convert),
        out_shape=[jax.ShapeDtypeStruct((m, d), F32), jax.ShapeDtypeStruct((nm, hist, D_FF), F32)] + extra_shapes,
        grid=(nm, nf + 1),
        in_specs=[pl.BlockSpec((tm, d), lambda i, j: (i, 0)),
                  pl.BlockSpec((None, 1, d), lambda i, j: (layer, 0, 0)),
                  w_specs[0], w_specs[1],
                  pl.BlockSpec((None, 3, tf), lambda i, j: (layer, 0, up(j))),
                  pl.BlockSpec((None, 1, tf), lambda i, j: (layer, 0, up(j))),
                  w_specs[2],
                  pl.BlockSpec((None, 1, d), lambda i, j: (layer, 0, 0)),
                  pl.BlockSpec((1, hist, tf), lambda i, j: (i // tps, 0, up(j)))],
        out_specs=[pl.BlockSpec((tm, d), lambda i, j: (i, 0), pipeline_mode=pl.Buffered(1)),
                   pl.BlockSpec((1, hist, tf), lambda i, j: (i, 0, up(j)))] + extra_specs,
        scratch_shapes=[pltpu.VMEM((tm, d), BF16),
                        pltpu.VMEM((hist + tm, tf), F32), pltpu.VMEM((nf, hist, tf), F32),
                        pltpu.VMEM((tm, tf), BF16), pltpu.VMEM((tm, tf), BF16)],
        compiler_params=_params(("arbitrary", "arbitrary")),
        name="ffn_convert" if convert else "ffn",
    )(x, gpre, w_args[0], w_args[1], cw, cb, w_args[2], gpost, state)


def _pool_kernel(c_ref, st_ref, cmap_ref, cs_ref, o_ref, tail_ref, ext_ref, carry_ref,
                 *, step, hist, tps, tm, pos0):
    i = pl.program_id(0)
    first = (i % tps) == 0

    @pl.when(first)
    def _():
        ext_ref[0:hist, :] = st_ref[0]

    @pl.when(jnp.logical_not(first))
    def _():
        ext_ref[0:hist, :] = carry_ref[...]

    ext_ref[hist:hist + tm, :] = c_ref[...]
    row = lax.broadcasted_iota(jnp.int32, (tm, 1), 0)
    if step > 1:
        row = lax.shift_right_logical(row, int(math.log2(step)))
    pos = pos0 + (i % tps) * (tm // step) + row
    for g, win in enumerate(POOL_WINDOWS):
        cols = slice(g * C_GROUP_DIM, (g + 1) * C_GROUP_DIM)
        cur = ext_ref[hist:hist + tm, cols]
        tot = cur
        for k in range(1, win):
            tot = tot + ext_ref[hist - k * step:hist - k * step + tm, cols]
        cnt = jnp.minimum(pos + 1, win).astype(F32)
        delta = tot / cnt - cur
        y = _dot(delta.astype(BF16), cmap_ref[g]) * cs_ref[:, cols]
        o_ref[:, cols] = y.astype(o_ref.dtype)
    tail = ext_ref[tm:tm + hist, :]
    carry_ref[...] = tail
    tail_ref[0] = tail


def _pool(proj, state, cmap, cscale, *, step, hist, tps, tm, pos0):
    m = proj.shape[0]
    nm = m // tm
    return pl.pallas_call(
        functools.partial(_pool_kernel, step=step, hist=hist, tps=tps, tm=tm, pos0=pos0),
        out_shape=[jax.ShapeDtypeStruct((m, C_WIDTH), BF16), jax.ShapeDtypeStruct((nm, hist, C_WIDTH), F32)],
        grid=(nm,),
        in_specs=[pl.BlockSpec((tm, C_WIDTH), lambda i: (i, 0)),
                  pl.BlockSpec((1, hist, C_WIDTH), lambda i: (i // tps, 0, 0)),
                  pl.BlockSpec(cmap.shape, lambda i: (0, 0, 0)),
                  pl.BlockSpec((1, C_WIDTH), lambda i: (0, 0))],
        out_specs=[pl.BlockSpec((tm, C_WIDTH), lambda i: (i, 0)),
                   pl.BlockSpec((1, hist, C_WIDTH), lambda i: (i, 0, 0))],
        scratch_shapes=[pltpu.VMEM((hist + tm, C_WIDTH), F32), pltpu.VMEM((hist, C_WIDTH), F32)],
        compiler_params=_params(("arbitrary",)),
        name="pool",
    )(proj, state, cmap, cscale)


S5_CHUNK = 512
S5_NCHUNK = S5_CH // S5_CHUNK


def _s5_kernel(u_ref, s0re_ref, s0im_ref, bblk_ref, cblk_ref, dskip_ref, wglu_ref, kc_ref,
               o_ref, tre_ref, tim_ref, sre_ref, sim_ref, cre_ref, cim_ref, *, step, tps, tm):
    i = pl.program_id(0)
    crow = cre_ref.shape[0]

    ub = u_ref[...].astype(BF16)
    for m in range(S5_NCHUNK):
        r = _dot(ub[:, m * LANES:(m + 1) * LANES], bblk_ref[m])
        sre_ref[:, m * S5_CHUNK:(m + 1) * S5_CHUNK] = r[:, :S5_CHUNK]
        sim_ref[:, m * S5_CHUNK:(m + 1) * S5_CHUNK] = r[:, S5_CHUNK:]

    @pl.when((i % tps) == 0)
    def _():
        cre_ref[...] = jnp.broadcast_to(s0re_ref[0], cre_ref.shape) if step == 1 else s0re_ref[0]
        cim_ref[...] = jnp.broadcast_to(s0im_ref[0], cim_ref.shape) if step == 1 else s0im_ref[0]

    for m in range(S5_NCHUNK):
        cols = slice(m * S5_CHUNK, (m + 1) * S5_CHUNK)
        if step == 1:
            consts = [kc_ref[k, :, cols] for k in range(8)]
            a1r, a1i, a2r, a2i, a4r, a4i, pwr, pwi = consts

            def body(rb, carry):
                cr, ci = carry
                r0 = pl.multiple_of(rb * SUBLANES, SUBLANES)
                xr = sre_ref[pl.ds(r0, SUBLANES), cols]
                xi = sim_ref[pl.ds(r0, SUBLANES), cols]
                for ar, ai, d in ((a1r, a1i, 1), (a2r, a2i, 2), (a4r, a4i, 4)):
                    sr = pltpu.roll(xr, d, 0)
                    si = pltpu.roll(xi, d, 0)
                    xr, xi = xr + (ar * sr - ai * si), xi + (ar * si + ai * sr)
                xr, xi = xr + (pwr * cr - pwi * ci), xi + (pwr * ci + pwi * cr)
                sre_ref[pl.ds(r0, SUBLANES), cols] = xr
                sim_ref[pl.ds(r0, SUBLANES), cols] = xi
                return (jnp.broadcast_to(xr[SUBLANES - 1:SUBLANES, :], xr.shape),
                        jnp.broadcast_to(xi[SUBLANES - 1:SUBLANES, :], xi.shape))

            cr, ci = lax.fori_loop(0, tm // SUBLANES, body, (cre_ref[:, cols], cim_ref[:, cols]))
        else:
            lr = jnp.broadcast_to(kc_ref[0, 0:1, cols], (crow, S5_CHUNK))
            li = jnp.broadcast_to(kc_ref[1, 0:1, cols], (crow, S5_CHUNK))

            def body(t, carry):
                cr, ci = carry
                r0 = pl.multiple_of(t * step, step)
                xr = sre_ref[pl.ds(r0, step), cols] + (lr * cr - li * ci)
                xi = sim_ref[pl.ds(r0, step), cols] + (lr * ci + li * cr)
                sre_ref[pl.ds(r0, step), cols] = xr
                sim_ref[pl.ds(r0, step), cols] = xi
                return xr, xi

            cr, ci = lax.fori_loop(0, tm // step, body, (cre_ref[:, cols], cim_ref[:, cols]))
        cre_ref[:, cols] = cr
        cim_ref[:, cols] = ci

    tre_ref[0] = cre_ref[...]
    tim_ref[0] = cim_ref[...]

    ys = []
    for m in range(S5_NCHUNK):
        cols = slice(m * S5_CHUNK, (m + 1) * S5_CHUNK)
        ys.append(_dot(sre_ref[:, cols].astype(BF16), cblk_ref[m, 0:S5_CHUNK, :])
                  + _dot(sim_ref[:, cols].astype(BF16), cblk_ref[m, S5_CHUNK:, :]))
    y = jnp.concatenate(ys, axis=1) + dskip_ref[...] * u_ref[...]
    z = _dot(jax.nn.gelu(y).astype(BF16), wglu_ref[...])
    o_ref[...] = (z[:, :D_WIDTH] * jax.nn.sigmoid(z[:, D_WIDTH:])).astype(o_ref.dtype)


def _s5(proj, s0re, s0im, bblk, cblk, dskip, wglu, kconst, *, step, tps, tm):
    m = proj.shape[0]
    nm = m // tm
    crow = s0re.shape[1] if step > 1 else SUBLANES
    srow = s0re.shape[1]
    return pl.pallas_call(
        functools.partial(_s5_kernel, step=step, tps=tps, tm=tm),
        out_shape=[jax.ShapeDtypeStruct((m, D_WIDTH), BF16),
                   jax.ShapeDtypeStruct((nm, crow, S5_CH), F32),
                   jax.ShapeDtypeStruct((nm, crow, S5_CH), F32)],
        grid=(nm,),
        in_specs=[pl.BlockSpec((tm, D_WIDTH), lambda i: (i, 1)),
                  pl.BlockSpec((1, srow, S5_CH), lambda i: (i // tps, 0, 0)),
                  pl.BlockSpec((1, srow, S5_CH), lambda i: (i // tps, 0, 0)),
                  pl.BlockSpec(bblk.shape, lambda i: (0, 0, 0)),
                  pl.BlockSpec(cblk.shape, lambda i: (0, 0, 0)),
                  pl.BlockSpec((1, D_WIDTH), lambda i: (0, 0)),
                  pl.BlockSpec(wglu.shape, lambda i: (0, 0)),
                  pl.BlockSpec(kconst.shape, lambda i: (0, 0, 0))],
        out_specs=[pl.BlockSpec((tm, D_WIDTH), lambda i: (i, 0)),
                   pl.BlockSpec((1, crow, S5_CH), lambda i: (i, 0, 0)),
                   pl.BlockSpec((1, crow, S5_CH), lambda i: (i, 0, 0))],
        scratch_shapes=[pltpu.VMEM((tm, S5_CH), F32), pltpu.VMEM((tm, S5_CH), F32),
                        pltpu.VMEM((crow, S5_CH), F32), pltpu.VMEM((crow, S5_CH), F32)],
        compiler_params=_params(("arbitrary",)),
        name="s5",
    )(proj, s0re, s0im, bblk, cblk, dskip, wglu, kconst)


def _block_diag(blocks, per):
    n, r, c = blocks.shape
    b = blocks.reshape(n // per, per, r, c)
    eye = jnp.eye(per, dtype=blocks.dtype)
    return jnp.einsum('mgrc,gh->mgrhc', b, eye).reshape(n // per, per * r, per * c)


def _s5_constants(a_re, a_im, log_dt, b_re, b_im, c_re, c_im):
    dt = jnp.exp(log_dt)[:, None]
    zr, zi = a_re * dt, a_im * dt
    mag = jnp.exp(zr)
    lr, li = mag * jnp.cos(zi), mag * jnp.sin(zi)
    den = a_re * a_re + a_im * a_im
    nr, ni = lr - 1.0, li
    kr, ki = (nr * a_re + ni * a_im) / den, (ni * a_re - nr * a_im) / den
    bbr = kr[..., None] * b_re - ki[..., None] * b_im
    bbi = kr[..., None] * b_im + ki[..., None] * b_re
    per = S5_CHUNK // S5_STATE
    bblk = jnp.concatenate([_block_diag(jnp.swapaxes(bbr, 1, 2), per),
                            _block_diag(jnp.swapaxes(bbi, 1, 2), per)], axis=2).astype(BF16)
    cblk = jnp.concatenate([_block_diag(jnp.swapaxes(c_re, 1, 2), per),
                            _block_diag(jnp.swapaxes(-c_im, 1, 2), per)], axis=1).astype(BF16)
    lr, li = lr.reshape(1, S5_CH), li.reshape(1, S5_CH)
    pr, pi = [lr], [li]
    for _ in range(SUBLANES - 1):
        pr, pi = pr + [pr[-1] * lr - pi[-1] * li], pi + [pr[-1] * li + pi[-1] * lr]
    rowid = jnp.arange(SUBLANES)[:, None]

    def masked(p, d):
        return jnp.where(rowid >= d, jnp.broadcast_to(p[d - 1], (SUBLANES, S5_CH)), 0.0)

    k_prompt = jnp.stack([masked(pr, 1), masked(pi, 1), masked(pr, 2), masked(pi, 2),
                          masked(pr, 4), masked(pi, 4),
                          jnp.concatenate(pr, axis=0), jnp.concatenate(pi, axis=0)])
    k_sample = jnp.stack([jnp.broadcast_to(lr, (SUBLANES, S5_CH)), jnp.broadcast_to(li, (SUBLANES, S5_CH))])
    return bblk, cblk, k_prompt, k_sample


def _time_major(a):
    a = jnp.swapaxes(a, 0, 1)
    return a.reshape((a.shape[0] * a.shape[1],) + a.shape[2:])


def kernel(x_prompt, x_sample, state_gla, state_pool, state_s5_re, state_s5_im, state_ffn_conv, norm_mix_pre, norm_mix_post, norm_ffn_pre, norm_ffn_post, w_in_even, a_w_s, a_b_s, a_v_norm, b_w_gate, b_gate_bias, b_out_norm, w_out_even, w_in_odd, c_map, c_scale, s5_a_re, s5_a_im, s5_log_dt, s5_b_re, s5_b_im, s5_c_re, s5_c_im, s5_d, s5_w_glu, w_out_odd, ffn_w_up, ffn_conv_w, ffn_conv_b, ffn_w_down):
    bp = x_prompt.shape[0]
    nb, ts = x_sample.shape[0], x_sample.shape[1]
    xp = x_prompt.reshape(bp * SEQ, D_MODEL)
    xs = x_sample.reshape(nb * ts, D_MODEL)

    row = lambda v: v.reshape(1, -1)
    n_main = 2 * A_WIDTH + 2 * B_KEY_WIDTH + 2 * B_WIDTH
    w_in0 = w_in_even[0].astype(BF16)
    w_lr = jnp.pad(w_in_even[0][:, n_main:], ((0, 0), (0, LANES - B_GATE_RANK))).astype(BF16)
    w_gate = jnp.pad(b_w_gate[0], ((0, LANES - B_GATE_RANK), (0, 0))).astype(BF16)
    gate = (w_lr, w_gate, row(b_gate_bias[0]))
    pos = jnp.arange(A_BLOCK)
    causal = (pos[None, :] // CHUNK) <= (pos[:, None] // CHUNK)
    ws_prompt = jnp.where(causal[None], a_w_s[0], 0.0).astype(BF16)
    per = A_BLOCK // ts
    ws_small = jnp.where(causal[None, :ts, :ts], a_w_s[0][:, :ts, :ts], 0.0)
    ws_sample = jnp.einsum('hij,ab->haibj', ws_small, jnp.eye(per, dtype=F32)).reshape(A_HEADS, A_BLOCK, A_BLOCK).astype(BF16)
    bs_prompt = a_b_s[0].T
    bs_sample = jnp.tile(a_b_s[0][:, :ts].T, (per, 1))
    w_out0 = w_out_even[0].astype(BF16)
    w_in1 = w_in_odd[0].astype(BF16)
    cmap = c_map[0].astype(BF16)
    bblk, cblk, k_prompt, k_sample = _s5_constants(s5_a_re[0], s5_a_im[0], s5_log_dt[0], s5_b_re[0], s5_b_im[0],
                                                   s5_c_re[0], s5_c_im[0])
    wglu = s5_w_glu[0].astype(BF16)
    w_out1 = w_out_odd[0].astype(BF16)

    tm = 512
    tf = 512
    tps_p = SEQ // tm
    ffn_hist_p = SUBLANES
    pool_hist_p = 2 * SUBLANES
    step_s = nb
    tps_s = (nb * ts) // tm
    ffn_hist_s = 2 * step_s
    pool_hist_s = (POOL_BUF + 1) * step_s

    tm_ffn = 1024
    tps_ffn_p = SEQ // tm_ffn
    tps_ffn_s = (nb * ts) // tm_ffn

    ffn_w16 = {}

    def ffn_layer(x, layer, state, *, step, hist, tps, convert):
        weights = (ffn_w_up, ffn_w_down) if convert else ffn_w16[layer]
        res = _ffn(x, norm_ffn_pre[:, None], weights, ffn_conv_w, ffn_conv_b[:, None], norm_ffn_post[:, None],
                   state, layer=layer, step=step, hist=hist, tps=tps, tm=tm_ffn, tf=256 if convert else tf,
                   convert=convert)
        if convert:
            ffn_w16[layer] = tuple(res[2:])
        return res[0], res[1]

    proj, lg = _inproj(xs, row(norm_mix_pre[0]), w_in0, gate, n=n_main, tm=1024, tn=1024)
    a_out, a_v = _sgu(proj, row(a_v_norm[0]), ws_sample, bs_sample, nblk=2, emit_av=True)
    b_out, gla_s = _gla(proj, lg, state_gla[0], row(b_out_norm[0]), nseq=nb, t=ts, clen=ts, ngrp=8, chain=False)
    xs = _outproj(a_out, b_out, w_out0, xs, row(norm_mix_post[0]), tm=tm)
    xs = _time_major(xs.reshape(nb, ts, D_MODEL))
    ffn_state = lambda layer: _time_major(state_ffn_conv[layer])[None]
    xs, ffn0_s = ffn_layer(xs, 0, ffn_state(0), step=step_s, hist=ffn_hist_s, tps=tps_ffn_s, convert=True)
    proj = _inproj(xs, row(norm_mix_pre[1]), w_in1, n=D_MODEL, tm=1024, tn=1024)
    pool_state = jnp.pad(_time_major(state_pool[0]), ((step_s, 0), (0, 0)))[None]
    c_out, pool_tail_s = _pool(proj, pool_state, cmap, row(c_scale[0]),
                               step=step_s, hist=pool_hist_s, tps=tps_s, tm=tm, pos0=PAST_LEN)
    d_out, s5re_tail_s, s5im_tail_s = _s5(proj, state_s5_re[0].reshape(1, nb, S5_CH),
                                          state_s5_im[0].reshape(1, nb, S5_CH),
                                          bblk, cblk, row(s5_d[0]), wglu, k_sample, step=step_s, tps=tps_s, tm=tm)
    xs = _outproj(c_out, d_out, w_out1, xs, row(norm_mix_post[1]), tm=tm)
    xs, ffn1_s = ffn_layer(xs, 1, ffn_state(1), step=step_s, hist=ffn_hist_s, tps=tps_ffn_s, convert=True)

    proj, lg = _inproj(xp, row(norm_mix_pre[0]), w_in0, gate, n=n_main, tm=1024, tn=1024)
    a_out = _sgu(proj, row(a_v_norm[0]), ws_prompt, bs_prompt, nblk=2, emit_av=False)[0]
    b_out, gla_p = _gla(proj, lg, jnp.zeros((bp, B_HEADS, B_KEY_DIM, B_VAL_DIM), F32), row(b_out_norm[0]),
                        nseq=bp, t=SEQ, clen=CHUNK, ngrp=4, chain=True)
    xp = _outproj(a_out, b_out, w_out0, xp, row(norm_mix_post[0]), tm=tm)
    xp, ffn0_p = ffn_layer(xp, 0, jnp.zeros((bp, ffn_hist_p, D_FF), F32), step=1, hist=ffn_hist_p, tps=tps_ffn_p,
                           convert=False)
    proj = _inproj(xp, row(norm_mix_pre[1]), w_in1, n=D_MODEL, tm=1024, tn=1024)
    c_out, pool_tail_p = _pool(proj, jnp.zeros((bp, pool_hist_p, C_WIDTH), F32), cmap, row(c_scale[0]),
                               step=1, hist=pool_hist_p, tps=tps_p, tm=tm, pos0=0)
    zero_state = jnp.zeros((bp, 1, S5_CH), F32)
    d_out, s5re_tail_p, s5im_tail_p = _s5(proj, zero_state, zero_state, bblk, cblk, row(s5_d[0]), wglu, k_prompt,
                                          step=1, tps=tps_p, tm=tm)
    xp = _outproj(c_out, d_out, w_out1, xp, row(norm_mix_post[1]), tm=tm)
    xp, ffn1_p = ffn_layer(xp, 1, jnp.zeros((bp, ffn_hist_p, D_FF), F32), step=1, hist=ffn_hist_p, tps=tps_ffn_p,
                           convert=False)

    last = slice(tps_p - 1, None, tps_p)
    y_prompt = xp.reshape(bp, SEQ, D_MODEL)
    gla_prompt = gla_p[None]
    pool_prompt = pool_tail_p[last, pool_hist_p - POOL_BUF:][None]
    s5_re_prompt = s5re_tail_p[last, 0].reshape(1, bp, S5_GROUPS, S5_STATE)
    s5_im_prompt = s5im_tail_p[last, 0].reshape(1, bp, S5_GROUPS, S5_STATE)
    last_ffn = slice(tps_ffn_p - 1, None, tps_ffn_p)
    ffn_prompt = jnp.stack([ffn0_p[last_ffn, ffn_hist_p - 2:], ffn1_p[last_ffn, ffn_hist_p - 2:]])

    def batch_major(a, nt):
        return jnp.swapaxes(a.reshape(nt, nb, a.shape[-1]), 0, 1)

    y_sample = batch_major(xs, ts)
    gla_sample = gla_s[None]
    av_sample = a_v.reshape(1, nb, ts, A_WIDTH)
    pool_sample = batch_major(pool_tail_s[-1, step_s:], POOL_BUF)[None]
    s5_re_sample = s5re_tail_s[-1].reshape(1, nb, S5_GROUPS, S5_STATE)
    s5_im_sample = s5im_tail_s[-1].reshape(1, nb, S5_GROUPS, S5_STATE)
    ffn_sample = jnp.stack([batch_major(ffn0_s[-1], 2), batch_major(ffn1_s[-1], 2)])

    return (y_prompt, y_sample, gla_prompt, gla_sample, av_sample, pool_prompt, pool_sample,
            s5_re_prompt, s5_im_prompt, s5_re_sample, s5_im_sample, ffn_prompt, ffn_sample)
```

```python
import functools
import math

import jax
import jax.numpy as jnp
from jax import lax
from jax.experimental import pallas as pl
from jax.experimental.pallas import tpu as pltpu

F32 = jnp.float32
BF16 = jnp.bfloat16

D_MODEL = 2048
SEQ = 4096
DEC_BATCH = 32
DEC_SEQ = 32
PAST_LEN = 4096
CHUNK = 64
A_WIDTH = 1024
A_HEADS = 8
A_BLOCK = 128
B_HEADS = 4
B_KEY_DIM = 128
B_KEY_WIDTH = 512
B_VAL_DIM = 256
B_WIDTH = 1024
B_GATE_RANK = 16
B_GATE_TAU = 16.0
C_WIDTH = 1024
C_GROUP_DIM = 256
POOL_WINDOWS = (2, 4, 8, 16)
POOL_BUF = 15
D_WIDTH = 1024
S5_GROUPS = 64
S5_GROUP_DIM = 16
S5_STATE = 64
S5_CH = S5_GROUPS * S5_STATE
D_FF = 5632
EPS = 1e-6

LANES = 128
SUBLANES = 8
VMEM_LIMIT = 56 * 1024 * 1024


def _params(sem):
    return pltpu.CompilerParams(dimension_semantics=sem, vmem_limit_bytes=VMEM_LIMIT)


def _rms(x, g):
    return x * lax.rsqrt(jnp.mean(x * x, axis=-1, keepdims=True) + EPS) * g


def _dot(a, b):
    return jnp.dot(a, b, preferred_element_type=F32)


def _inproj_kernel(x_ref, g_ref, w_ref, *rest, with_gate):
    if with_gate:
        wlr_ref, wgate_ref, gbias_ref, o_ref, lg_ref, h_ref = rest
    else:
        o_ref, h_ref = rest
    j = pl.program_id(1)

    @pl.when(j == 0)
    def _():
        hb = _rms(x_ref[...], g_ref[...]).astype(BF16)
        h_ref[...] = hb
        if with_gate:
            glr = _dot(hb, wlr_ref[...])
            z = _dot(glr.astype(BF16), wgate_ref[...]) + gbias_ref[...]
            lg_ref[...] = (jnp.minimum(z, 0.0) - jnp.log(1.0 + jnp.exp(-jnp.abs(z)))) * (1.0 / B_GATE_TAU)

    o_ref[...] = _dot(h_ref[...], w_ref[...])


def _inproj(x, g, w, gate=None, *, n, tm, tn):
    m, d = x.shape
    grid = (m // tm, n // tn)
    in_specs = [pl.BlockSpec((tm, d), lambda i, j: (i, 0)),
                pl.BlockSpec((1, d), lambda i, j: (0, 0)),
                pl.BlockSpec((d, tn), lambda i, j: (0, j))]
    out_shape = [jax.ShapeDtypeStruct((m, n), F32)]
    out_specs = [pl.BlockSpec((tm, tn), lambda i, j: (i, j))]
    args = [x, g, w]
    if gate is not None:
        wlr, wgate, gbias = gate
        in_specs += [pl.BlockSpec(wlr.shape, lambda i, j: (0, 0)),
                     pl.BlockSpec(wgate.shape, lambda i, j: (0, 0)),
                     pl.BlockSpec(gbias.shape, lambda i, j: (0, 0))]
        out_shape.append(jax.ShapeDtypeStruct((m, B_KEY_WIDTH), F32))
        out_specs.append(pl.BlockSpec((tm, B_KEY_WIDTH), lambda i, j: (i, 0)))
        args += [wlr, wgate, gbias]
    res = pl.pallas_call(
        functools.partial(_inproj_kernel, with_gate=gate is not None),
        out_shape=out_shape, grid=grid, in_specs=in_specs, out_specs=out_specs,
        scratch_shapes=[pltpu.VMEM((tm, d), BF16)],
        compiler_params=_params(("parallel", "arbitrary")),
        name="inproj_gate" if gate is not None else "inproj",
    )(*args)
    return res if gate is not None else res[0]


def _sgu_kernel(u_ref, v_ref, gain_ref, w_ref, b_ref, o_ref, *av_ref, nblk):
    for n in range(nblk):
        rows = slice(n * A_BLOCK, (n + 1) * A_BLOCK)
        v = jax.nn.gelu(v_ref[rows, :])
        mu = jnp.mean(v, axis=-1, keepdims=True)
        vc = v - mu
        vn = vc * lax.rsqrt(jnp.mean(vc * vc, axis=-1, keepdims=True) + EPS) * gain_ref[...]
        if av_ref:
            av_ref[0][rows, :] = vn
        vb = vn.astype(BF16)
        for h in range(A_HEADS):
            cols = slice(h * LANES, (h + 1) * LANES)
            s = _dot(w_ref[h], vb[:, cols]) + b_ref[:, h:h + 1]
            o_ref[rows, cols] = (jax.nn.gelu(u_ref[rows, cols]) * s).astype(o_ref.dtype)


def _sgu(proj, gain, w, b, *, nblk, emit_av):
    m = proj.shape[0]
    tm = nblk * A_BLOCK
    n_out = 2 if emit_av else 1
    return pl.pallas_call(
        functools.partial(_sgu_kernel, nblk=nblk),
        out_shape=[jax.ShapeDtypeStruct((m, A_WIDTH), BF16), jax.ShapeDtypeStruct((m, A_WIDTH), F32)][:n_out],
        grid=(m // tm,),
        in_specs=[pl.BlockSpec((tm, A_WIDTH), lambda i: (i, 0)),
                  pl.BlockSpec((tm, A_WIDTH), lambda i: (i, 1)),
                  pl.BlockSpec((1, A_WIDTH), lambda i: (0, 0)),
                  pl.BlockSpec(w.shape, lambda i: (0, 0, 0)),
                  pl.BlockSpec(b.shape, lambda i: (0, 0))],
        out_specs=[pl.BlockSpec((tm, A_WIDTH), lambda i: (i, 0)),
                   pl.BlockSpec((tm, A_WIDTH), lambda i: (i, 0))][:n_out],
        compiler_params=_params(("parallel",)),
        name="sgu",
    )(proj, proj, gain, w, b)


def _gla_kernel(q_ref, k_ref, v_ref, r_ref, lg_ref, s0_ref, og_ref, o_ref, sout_ref, st_ref, *, clen, ngrp, chain):
    c = pl.program_id(1)
    rows_all = ngrp * clen
    shift = int(math.log2(clen))

    if chain:
        @pl.when(c == 0)
        def _():
            for h in range(B_HEADS):
                st_ref[h] = s0_ref[0, h].T

    row_i = lax.broadcasted_iota(jnp.int32, (rows_all, rows_all), 0)
    col_i = lax.broadcasted_iota(jnp.int32, (rows_all, rows_all), 1)
    same_group = lax.shift_right_logical(row_i, shift) == lax.shift_right_logical(col_i, shift)
    tri = ((row_i >= col_i) & same_group).astype(F32)
    cum = jnp.dot(tri, lg_ref[...], precision=lax.Precision.HIGHEST,
                  preferred_element_type=F32)
    tots = [cum[(g + 1) * clen - 1:(g + 1) * clen, :] for g in range(ngrp)]
    tot_rows = jnp.concatenate([jnp.broadcast_to(t, (clen, B_KEY_WIDTH)) for t in tots], axis=0)
    kd = (k_ref[...] * jnp.exp(tot_rows - cum)).astype(BF16)
    qs = (q_ref[...] * (B_KEY_DIM ** -0.5)).astype(BF16)
    vb = v_ref[...].astype(BF16)
    sr = jax.nn.silu(r_ref[...])
    grp = lax.shift_right_logical(lax.broadcasted_iota(jnp.int32, (rows_all, 1), 0), shift)
    zero = jnp.zeros((), BF16)

    def by_group(x):
        return jnp.concatenate([jnp.where(grp == g, x, zero) for g in range(ngrp)], axis=1)

    for h in range(B_HEADS):
        kc = slice(h * B_KEY_DIM, (h + 1) * B_KEY_DIM)
        vc = slice(h * B_VAL_DIM, (h + 1) * B_VAL_DIM)
        upd = lax.dot_general(vb[:, vc], by_group(kd[:, kc]), (((0,), (0,)), ((), ())),
                              preferred_element_type=F32)
        states = []
        st = st_ref[h] if chain else None
        for g in range(ngrp):
            if not chain:
                st = s0_ref[g, h].T
            st = jnp.exp(tots[g][:, kc]) * st + upd[:, g * B_KEY_DIM:(g + 1) * B_KEY_DIM]
            states.append(st.astype(BF16))
            if not chain:
                sout_ref[g, h] = st.T
        if chain:
            st_ref[h] = st
        o = lax.dot_general(by_group(qs[:, kc]), jnp.concatenate(states, axis=1), (((1,), (1,)), ((), ())),
                            preferred_element_type=F32)
        o = o * lax.rsqrt(jnp.mean(o * o, axis=-1, keepdims=True) + EPS)
        o = o * og_ref[:, vc] * sr[:, vc]
        o_ref[:, vc] = o.astype(o_ref.dtype)

    if chain:
        @pl.when(c == pl.num_programs(1) - 1)
        def _():
            for h in range(B_HEADS):
                sout_ref[0, h] = st_ref[h].T


def _gla(proj, lg, s0, og, *, nseq, t, clen, ngrp, chain):
    rows = clen * ngrp
    m = nseq * t
    if chain:
        steps = t // rows
        grid = (nseq, steps)
        rmap = lambda b, c: b * steps + c
        nstate = 1
    else:
        grid = (m // rows, 1)
        rmap = lambda b, c: b
        nstate = ngrp
    blk = lambda width, col: pl.BlockSpec((rows, width), lambda b, c: (rmap(b, c), col))
    state_spec = pl.BlockSpec((nstate, B_HEADS, B_KEY_DIM, B_VAL_DIM), lambda b, c: (b, 0, 0, 0))
    return pl.pallas_call(
        functools.partial(_gla_kernel, clen=clen, ngrp=ngrp, chain=chain),
        out_shape=[jax.ShapeDtypeStruct((m, B_WIDTH), BF16),
                   jax.ShapeDtypeStruct((nseq, B_HEADS, B_KEY_DIM, B_VAL_DIM), F32)],
        grid=grid,
        in_specs=[blk(B_KEY_WIDTH, 4),
                  blk(B_KEY_WIDTH, 5),
                  blk(B_WIDTH, 3),
                  blk(B_WIDTH, 4),
                  blk(B_KEY_WIDTH, 0),
                  state_spec,
                  pl.BlockSpec((1, B_WIDTH), lambda b, c: (0, 0))],
        out_specs=[blk(B_WIDTH, 0), state_spec],
        scratch_shapes=[pltpu.VMEM((B_HEADS, B_VAL_DIM, B_KEY_DIM), F32)],
        compiler_params=_params(("parallel", "arbitrary")),
        name="gla",
    )(proj, proj, proj, proj, lg, s0, og)


def _outproj_kernel(a_ref, b_ref, w_ref, x_ref, g_ref, o_ref):
    ka = a_ref.shape[1]
    y = _dot(a_ref[...], w_ref[0:ka, :]) + _dot(b_ref[...], w_ref[ka:, :])
    o_ref[...] = x_ref[...] + _rms(y, g_ref[...])


def _outproj(a, b, w, x, g, *, tm):
    m, d = x.shape
    return pl.pallas_call(
        _outproj_kernel,
        out_shape=jax.ShapeDtypeStruct((m, d), F32),
        grid=(m // tm,),
        in_specs=[pl.BlockSpec((tm, a.shape[1]), lambda i: (i, 0)),
                  pl.BlockSpec((tm, b.shape[1]), lambda i: (i, 0)),
                  pl.BlockSpec(w.shape, lambda i: (0, 0)),
                  pl.BlockSpec((tm, d), lambda i: (i, 0)),
                  pl.BlockSpec((1, d), lambda i: (0, 0))],
        out_specs=pl.BlockSpec((tm, d), lambda i: (i, 0)),
        compiler_params=_params(("parallel",)),
        name="outproj",
    )(a, b, w, x, g)


def _ffn_kernel(x_ref, gpre_ref, wg_ref, wv_ref, cw_ref, cb_ref, wd_ref, gpost_ref, st_ref,
                o_ref, tail_ref, *rest, step, hist, tps, tm, nf, convert):
    if convert:
        wg_out, wv_out, wd_out, h_ref, gext_ref, carry_ref, act_a, act_b = rest
    else:
        h_ref, gext_ref, carry_ref, act_a, act_b = rest
    i = pl.program_id(0)
    j = pl.program_id(1)
    first = (i % tps) == 0

    def up_and_gate(act_ref):
        hb = h_ref[...]
        wg, wv = wg_ref[...], wv_ref[...]
        if convert:
            wg, wv = wg.astype(BF16), wv.astype(BF16)
            wg_out[...] = wg
            wv_out[...] = wv
        gate = _dot(hb, wg)
        val = _dot(hb, wv)
        gext_ref[0:hist, :] = jnp.where(first, st_ref[0], carry_ref[j])
        gext_ref[hist:hist + tm, :] = gate
        prev2 = gext_ref[hist - 2 * step:hist - 2 * step + tm, :]
        prev1 = gext_ref[hist - step:hist - step + tm, :]
        conv = cb_ref[...] + cw_ref[0:1, :] * prev2 + cw_ref[1:2, :] * prev1 + cw_ref[2:3, :] * gate
        act_ref[...] = (jax.nn.gelu(conv) * val).astype(BF16)
        tail = gate[tm - hist:, :]
        carry_ref[j] = tail
        tail_ref[0] = tail

    def down(act_ref):
        wd = wd_ref[...]
        if convert:
            wd = wd.astype(BF16)
            wd_out[...] = wd
        o_ref[...] += _dot(act_ref[...], wd)

    @pl.when(j == 0)
    def _():
        h_ref[...] = _rms(x_ref[...], gpre_ref[...]).astype(BF16)
        o_ref[...] = jnp.zeros_like(o_ref)

        @pl.when(i == 0)
        def _():
            carry_ref[...] = jnp.zeros_like(carry_ref)

        up_and_gate(act_a)

    for parity, (src, dst) in enumerate(((act_b, act_a), (act_a, act_b))):
        @pl.when((j > 0) & (j < nf) & (j % 2 == parity))
        def _(src=src, dst=dst):
            down(src)
            up_and_gate(dst)

    @pl.when(j == nf)
    def _():
        down(act_a if (nf - 1) % 2 == 0 else act_b)
        o_ref[...] = x_ref[...] + _rms(o_ref[...], gpost_ref[...])


def _ffn(x, gpre, weights, cw, cb, gpost, state, *, layer, step, hist, tps, tm, tf, convert):
    m, d = x.shape
    nf = D_FF // tf
    nm = m // tm
    up = lambda j: jnp.minimum(j, nf - 1)
    down = lambda j: jnp.maximum(j - 1, 0)
    if convert:
        w_up, w_down = weights
        w_args = (w_up, w_up, w_down)
        w_specs = [pl.BlockSpec((None, d, tf), lambda i, j: (layer, 0, up(j))),
                   pl.BlockSpec((None, d, tf), lambda i, j: (layer, 0, nf + up(j))),
                   pl.BlockSpec((None, tf, d), lambda i, j: (layer, down(j), 0))]
        extra_shapes = [jax.ShapeDtypeStruct((d, D_FF), BF16), jax.ShapeDtypeStruct((d, D_FF), BF16),
                        jax.ShapeDtypeStruct((D_FF, d), BF16)]
        extra_specs = [pl.BlockSpec((d, tf), lambda i, j: (0, up(j))),
                       pl.BlockSpec((d, tf), lambda i, j: (0, up(j))),
                       pl.BlockSpec((tf, d), lambda i, j: (down(j), 0))]
    else:
        w_args = weights
        w_specs = [pl.BlockSpec((d, tf), lambda i, j: (0, up(j))),
                   pl.BlockSpec((d, tf), lambda i, j: (0, up(j))),
                   pl.BlockSpec((tf, d), lambda i, j: (down(j), 0))]
        extra_shapes, extra_specs = [], []
    return pl.pallas_call(
        functools.partial(_ffn_kernel, step=step, hist=hist, tps=tps, tm=tm, nf=nf, convert=convert),
        out_shape=[jax.ShapeDtypeStruct((m, d), F32), jax.ShapeDtypeStruct((nm, hist, D_FF), F32)] + extra_shapes,
        grid=(nm, nf + 1),
        in_specs=[pl.BlockSpec((tm, d), lambda i, j: (i, 0)),
                  pl.BlockSpec((None, 1, d), lambda i, j: (layer, 0, 0)),
                  w_specs[0], w_specs[1],
                  pl.BlockSpec((None, 3, tf), lambda i, j: (layer, 0, up(j))),
                  pl.BlockSpec((None, 1, tf), lambda i, j: (layer, 0, up(j))),
                  w_specs[2],
                  pl.BlockSpec((None, 1, d), lambda i, j: (layer, 0, 0)),
                  pl.BlockSpec((1, hist, tf), lambda i, j: (i // tps, 0, up(j)))],
        out_specs=[pl.BlockSpec((tm, d), lambda i, j: (i, 0), pipeline_mode=pl.Buffered(1)),
                   pl.BlockSpec((1, hist, tf), lambda i, j: (i, 0, up(j)))] + extra_specs,
        scratch_shapes=[pltpu.VMEM((tm, d), BF16),
                        pltpu.VMEM((hist + tm, tf), F32), pltpu.VMEM((nf, hist, tf), F32),
                        pltpu.VMEM((tm, tf), BF16), pltpu.VMEM((tm, tf), BF16)],
        compiler_params=_params(("arbitrary", "arbitrary")),
        name="ffn_convert" if convert else "ffn",
    )(x, gpre, w_args[0], w_args[1], cw, cb, w_args[2], gpost, state)


def _pool_kernel(c_ref, st_ref, cmap_ref, cs_ref, o_ref, tail_ref, ext_ref, carry_ref,
                 *, step, hist, tps, tm, pos0):
    i = pl.program_id(0)
    first = (i % tps) == 0

    @pl.when(first)
    def _():
        ext_ref[0:hist, :] = st_ref[0]

    @pl.when(jnp.logical_not(first))
    def _():
        ext_ref[0:hist, :] = carry_ref[...]

    ext_ref[hist:hist + tm, :] = c_ref[...]
    row = lax.broadcasted_iota(jnp.int32, (tm, 1), 0)
    if step > 1:
        row = lax.shift_right_logical(row, int(math.log2(step)))
    pos = pos0 + (i % tps) * (tm // step) + row
    for g, win in enumerate(POOL_WINDOWS):
        cols = slice(g * C_GROUP_DIM, (g + 1) * C_GROUP_DIM)
        cur = ext_ref[hist:hist + tm, cols]
        tot = cur
        for k in range(1, win):
            tot = tot + ext_ref[hist - k * step:hist - k * step + tm, cols]
        cnt = jnp.minimum(pos + 1, win).astype(F32)
        delta = tot / cnt - cur
        y = _dot(delta.astype(BF16), cmap_ref[g]) * cs_ref[:, cols]
        o_ref[:, cols] = y.astype(o_ref.dtype)
    tail = ext_ref[tm:tm + hist, :]
    carry_ref[...] = tail
    tail_ref[0] = tail


def _pool(proj, state, cmap, cscale, *, step, hist, tps, tm, pos0):
    m = proj.shape[0]
    nm = m // tm
    return pl.pallas_call(
        functools.partial(_pool_kernel, step=step, hist=hist, tps=tps, tm=tm, pos0=pos0),
        out_shape=[jax.ShapeDtypeStruct((m, C_WIDTH), BF16), jax.ShapeDtypeStruct((nm, hist, C_WIDTH), F32)],
        grid=(nm,),
        in_specs=[pl.BlockSpec((tm, C_WIDTH), lambda i: (i, 0)),
                  pl.BlockSpec((1, hist, C_WIDTH), lambda i: (i // tps, 0, 0)),
                  pl.BlockSpec(cmap.shape, lambda i: (0, 0, 0)),
                  pl.BlockSpec((1, C_WIDTH), lambda i: (0, 0))],
        out_specs=[pl.BlockSpec((tm, C_WIDTH), lambda i: (i, 0)),
                   pl.BlockSpec((1, hist, C_WIDTH), lambda i: (i, 0, 0))],
        scratch_shapes=[pltpu.VMEM((hist + tm, C_WIDTH), F32), pltpu.VMEM((hist, C_WIDTH), F32)],
        compiler_params=_params(("arbitrary",)),
        name="pool",
    )(proj, state, cmap, cscale)


S5_CHUNK = 512
S5_NCHUNK = S5_CH // S5_CHUNK
S5_FOLD = SUBLANES // 2


def _s5_kernel(u_ref, s0re_ref, s0im_ref, bblk_ref, cblk_ref, dskip_ref, wglu_ref, kc_ref,
               o_ref, tre_ref, tim_ref, sre_ref, sim_ref, cre_ref, cim_ref, *, step, tps, tm):
    i = pl.program_id(0)
    crow = cre_ref.shape[0]

    @pl.when((i % tps) == 0)
    def _():
        cre_ref[...] = jnp.broadcast_to(s0re_ref[0], cre_ref.shape) if step == 1 else s0re_ref[0]
        cim_ref[...] = jnp.broadcast_to(s0im_ref[0], cim_ref.shape) if step == 1 else s0im_ref[0]

    u = u_ref[...]
    lhs = [u.astype(BF16)]
    if step == 1:
        row_in_block = lax.broadcasted_iota(jnp.int32, (tm, 1), 0) & (SUBLANES - 1)
        for k in range(1, S5_FOLD):
            lhs.append(jnp.where(row_in_block >= k, pltpu.roll(u, k, 0), 0.0).astype(BF16))
    ys = []
    for m in range(S5_NCHUNK):
        cols = slice(m * S5_CHUNK, (m + 1) * S5_CHUNK)
        ucols = slice(m * LANES, (m + 1) * LANES)
        if step == 1:
            r = _dot(jnp.concatenate([x[:, ucols] for x in lhs], axis=1), bblk_ref[m])
        else:
            r = _dot(lhs[0][:, ucols], bblk_ref[m, 0:LANES, :])
        sre_ref[:, cols] = r[:, :S5_CHUNK]
        sim_ref[:, cols] = r[:, S5_CHUNK:]
        cr, ci = cre_ref[:, cols], cim_ref[:, cols]
        if step == 1:
            ar, ai, pwr, pwi = [kc_ref[k, :, cols] for k in range(4)]
            for rb in range(tm // SUBLANES):
                rows = slice(rb * SUBLANES, (rb + 1) * SUBLANES)
                xr = sre_ref[rows, cols]
                xi = sim_ref[rows, cols]
                sr = pltpu.roll(xr, S5_FOLD, 0)
                si = pltpu.roll(xi, S5_FOLD, 0)
                xr, xi = xr + (ar * sr - ai * si), xi + (ar * si + ai * sr)
                xr, xi = xr + (pwr * cr - pwi * ci), xi + (pwr * ci + pwi * cr)
                sre_ref[rows, cols] = xr
                sim_ref[rows, cols] = xi
                cr = jnp.broadcast_to(xr[SUBLANES - 1:SUBLANES, :], xr.shape)
                ci = jnp.broadcast_to(xi[SUBLANES - 1:SUBLANES, :], xi.shape)
        else:
            lr = jnp.broadcast_to(kc_ref[0, 0:1, cols], (crow, S5_CHUNK))
            li = jnp.broadcast_to(kc_ref[1, 0:1, cols], (crow, S5_CHUNK))
            for t in range(tm // step):
                rows = slice(t * step, (t + 1) * step)
                cr, ci = (sre_ref[rows, cols] + (lr * cr - li * ci),
                          sim_ref[rows, cols] + (lr * ci + li * cr))
                sre_ref[rows, cols] = cr
                sim_ref[rows, cols] = ci
        cre_ref[:, cols] = cr
        cim_ref[:, cols] = ci
        ys.append(_dot(sre_ref[:, cols].astype(BF16), cblk_ref[m, 0:S5_CHUNK, :])
                  + _dot(sim_ref[:, cols].astype(BF16), cblk_ref[m, S5_CHUNK:, :]))

    tre_ref[0] = cre_ref[...]
    tim_ref[0] = cim_ref[...]


    y = jnp.concatenate(ys, axis=1) + dskip_ref[...] * u
    z = _dot(jax.nn.gelu(y).astype(BF16), wglu_ref[...])
    o_ref[...] = (z[:, :D_WIDTH] * jax.nn.sigmoid(z[:, D_WIDTH:])).astype(o_ref.dtype)


def _s5(proj, s0re, s0im, bblk, cblk, dskip, wglu, kconst, *, step, tps, tm):
    m = proj.shape[0]
    nm = m // tm
    crow = s0re.shape[1] if step > 1 else SUBLANES
    srow = s0re.shape[1]
    return pl.pallas_call(
        functools.partial(_s5_kernel, step=step, tps=tps, tm=tm),
        out_shape=[jax.ShapeDtypeStruct((m, D_WIDTH), BF16),
                   jax.ShapeDtypeStruct((nm, crow, S5_CH), F32),
                   jax.ShapeDtypeStruct((nm, crow, S5_CH), F32)],
        grid=(nm,),
        in_specs=[pl.BlockSpec((tm, D_WIDTH), lambda i: (i, 1)),
                  pl.BlockSpec((1, srow, S5_CH), lambda i: (i // tps, 0, 0)),
                  pl.BlockSpec((1, srow, S5_CH), lambda i: (i // tps, 0, 0)),
                  pl.BlockSpec(bblk.shape, lambda i: (0, 0, 0), pipeline_mode=pl.Buffered(1)),
                  pl.BlockSpec(cblk.shape, lambda i: (0, 0, 0), pipeline_mode=pl.Buffered(1)),
                  pl.BlockSpec((1, D_WIDTH), lambda i: (0, 0)),
                  pl.BlockSpec(wglu.shape, lambda i: (0, 0), pipeline_mode=pl.Buffered(1)),
                  pl.BlockSpec(kconst.shape, lambda i: (0, 0, 0), pipeline_mode=pl.Buffered(1))],
        out_specs=[pl.BlockSpec((tm, D_WIDTH), lambda i: (i, 0)),
                   pl.BlockSpec((1, crow, S5_CH), lambda i: (i, 0, 0)),
                   pl.BlockSpec((1, crow, S5_CH), lambda i: (i, 0, 0))],
        scratch_shapes=[pltpu.VMEM((tm, S5_CH), F32), pltpu.VMEM((tm, S5_CH), F32),
                        pltpu.VMEM((crow, S5_CH), F32), pltpu.VMEM((crow, S5_CH), F32)],
        compiler_params=_params(("arbitrary",)),
        name="s5",
    )(proj, s0re, s0im, bblk, cblk, dskip, wglu, kconst)


def _block_diag(blocks, per):
    n, r, c = blocks.shape
    b = blocks.reshape(n // per, per, r, c)
    eye = jnp.eye(per, dtype=blocks.dtype)
    return jnp.einsum('mgrc,gh->mgrhc', b, eye).reshape(n // per, per * r, per * c)


def _s5_constants(a_re, a_im, log_dt, b_re, b_im, c_re, c_im):
    dt = jnp.exp(log_dt)[:, None]
    zr, zi = a_re * dt, a_im * dt
    mag = jnp.exp(zr)
    lr, li = mag * jnp.cos(zi), mag * jnp.sin(zi)
    den = a_re * a_re + a_im * a_im
    nr, ni = lr - 1.0, li
    kr, ki = (nr * a_re + ni * a_im) / den, (ni * a_re - nr * a_im) / den
    bbr = kr[..., None] * b_re - ki[..., None] * b_im
    bbi = kr[..., None] * b_im + ki[..., None] * b_re
    per = S5_CHUNK // S5_STATE
    blocks = []
    qr, qi = jnp.ones_like(lr), jnp.zeros_like(li)
    for _ in range(S5_FOLD):
        fr = qr[..., None] * bbr - qi[..., None] * bbi
        fi = qr[..., None] * bbi + qi[..., None] * bbr
        blocks.append(jnp.concatenate([_block_diag(jnp.swapaxes(fr, 1, 2), per),
                                       _block_diag(jnp.swapaxes(fi, 1, 2), per)], axis=2))
        qr, qi = qr * lr - qi * li, qr * li + qi * lr
    bblk = jnp.concatenate(blocks, axis=1).astype(BF16)
    cblk = jnp.concatenate([_block_diag(jnp.swapaxes(c_re, 1, 2), per),
                            _block_diag(jnp.swapaxes(-c_im, 1, 2), per)], axis=1).astype(BF16)
    lr, li = lr.reshape(1, S5_CH), li.reshape(1, S5_CH)
    pr, pi = [lr], [li]
    for _ in range(SUBLANES - 1):
        pr, pi = pr + [pr[-1] * lr - pi[-1] * li], pi + [pr[-1] * li + pi[-1] * lr]
    rowid = jnp.arange(SUBLANES)[:, None]

    def masked(p, d):
        return jnp.where(rowid >= d, jnp.broadcast_to(p[d - 1], (SUBLANES, S5_CH)), 0.0)

    k_prompt = jnp.stack([masked(pr, S5_FOLD), masked(pi, S5_FOLD),
                          jnp.concatenate(pr, axis=0), jnp.concatenate(pi, axis=0)])
    k_sample = jnp.stack([jnp.broadcast_to(lr, (SUBLANES, S5_CH)), jnp.broadcast_to(li, (SUBLANES, S5_CH))])
    return bblk, cblk, k_prompt, k_sample


def _time_major(a):
    a = jnp.swapaxes(a, 0, 1)
    return a.reshape((a.shape[0] * a.shape[1],) + a.shape[2:])


def kernel(x_prompt, x_sample, state_gla, state_pool, state_s5_re, state_s5_im, state_ffn_conv, norm_mix_pre, norm_mix_post, norm_ffn_pre, norm_ffn_post, w_in_even, a_w_s, a_b_s, a_v_norm, b_w_gate, b_gate_bias, b_out_norm, w_out_even, w_in_odd, c_map, c_scale, s5_a_re, s5_a_im, s5_log_dt, s5_b_re, s5_b_im, s5_c_re, s5_c_im, s5_d, s5_w_glu, w_out_odd, ffn_w_up, ffn_conv_w, ffn_conv_b, ffn_w_down):
    bp = x_prompt.shape[0]
    nb, ts = x_sample.shape[0], x_sample.shape[1]
    xp = x_prompt.reshape(bp * SEQ, D_MODEL)
    xs = x_sample.reshape(nb * ts, D_MODEL)

    row = lambda v: v.reshape(1, -1)
    n_main = 2 * A_WIDTH + 2 * B_KEY_WIDTH + 2 * B_WIDTH
    w_in0 = w_in_even[0].astype(BF16)
    w_lr = jnp.pad(w_in_even[0][:, n_main:], ((0, 0), (0, LANES - B_GATE_RANK))).astype(BF16)
    w_gate = jnp.pad(b_w_gate[0], ((0, LANES - B_GATE_RANK), (0, 0))).astype(BF16)
    gate = (w_lr, w_gate, row(b_gate_bias[0]))
    pos = jnp.arange(A_BLOCK)
    causal = (pos[None, :] // CHUNK) <= (pos[:, None] // CHUNK)
    ws_prompt = jnp.where(causal[None], a_w_s[0], 0.0).astype(BF16)
    per = A_BLOCK // ts
    ws_small = jnp.where(causal[None, :ts, :ts], a_w_s[0][:, :ts, :ts], 0.0)
    ws_sample = jnp.einsum('hij,ab->haibj', ws_small, jnp.eye(per, dtype=F32)).reshape(A_HEADS, A_BLOCK, A_BLOCK).astype(BF16)
    bs_prompt = a_b_s[0].T
    bs_sample = jnp.tile(a_b_s[0][:, :ts].T, (per, 1))
    w_out0 = w_out_even[0].astype(BF16)
    w_in1 = w_in_odd[0].astype(BF16)
    cmap = c_map[0].astype(BF16)
    bblk, cblk, k_prompt, k_sample = _s5_constants(s5_a_re[0], s5_a_im[0], s5_log_dt[0], s5_b_re[0], s5_b_im[0],
                                                   s5_c_re[0], s5_c_im[0])
    wglu = s5_w_glu[0].astype(BF16)
    w_out1 = w_out_odd[0].astype(BF16)

    tm = 512
    tf = 512
    tps_p = SEQ // tm
    ffn_hist_p = SUBLANES
    pool_hist_p = 2 * SUBLANES
    step_s = nb
    tps_s = (nb * ts) // tm
    ffn_hist_s = 2 * step_s
    pool_hist_s = (POOL_BUF + 1) * step_s

    tm_ffn = 1024
    tps_ffn_p = SEQ // tm_ffn
    tps_ffn_s = (nb * ts) // tm_ffn

    ffn_w16 = {}

    def ffn_layer(x, layer, state, *, step, hist, tps, convert):
        weights = (ffn_w_up, ffn_w_down) if convert else ffn_w16[layer]
        res = _ffn(x, norm_ffn_pre[:, None], weights, ffn_conv_w, ffn_conv_b[:, None], norm_ffn_post[:, None],
                   state, layer=layer, step=step, hist=hist, tps=tps, tm=tm_ffn, tf=256 if convert else tf,
                   convert=convert)
        if convert:
            ffn_w16[layer] = tuple(res[2:])
        return res[0], res[1]

    proj, lg = _inproj(xs, row(norm_mix_pre[0]), w_in0, gate, n=n_main, tm=1024, tn=1024)
    a_out, a_v = _sgu(proj, row(a_v_norm[0]), ws_sample, bs_sample, nblk=2, emit_av=True)
    b_out, gla_s = _gla(proj, lg, state_gla[0], row(b_out_norm[0]), nseq=nb, t=ts, clen=ts, ngrp=8, chain=False)
    xs = _outproj(a_out, b_out, w_out0, xs, row(norm_mix_post[0]), tm=tm)
    xs = _time_major(xs.reshape(nb, ts, D_MODEL))
    ffn_state = lambda layer: _time_major(state_ffn_conv[layer])[None]
    xs, ffn0_s = ffn_layer(xs, 0, ffn_state(0), step=step_s, hist=ffn_hist_s, tps=tps_ffn_s, convert=True)
    proj = _inproj(xs, row(norm_mix_pre[1]), w_in1, n=D_MODEL, tm=1024, tn=1024)
    pool_state = jnp.pad(_time_major(state_pool[0]), ((step_s, 0), (0, 0)))[None]
    c_out, pool_tail_s = _pool(proj, pool_state, cmap, row(c_scale[0]),
                               step=step_s, hist=pool_hist_s, tps=tps_s, tm=tm, pos0=PAST_LEN)
    d_out, s5re_tail_s, s5im_tail_s = _s5(proj, state_s5_re[0].reshape(1, nb, S5_CH),
                                          state_s5_im[0].reshape(1, nb, S5_CH),
                                          bblk, cblk, row(s5_d[0]), wglu, k_sample, step=step_s, tps=tps_s, tm=tm)
    xs = _outproj(c_out, d_out, w_out1, xs, row(norm_mix_post[1]), tm=tm)
    xs, ffn1_s = ffn_layer(xs, 1, ffn_state(1), step=step_s, hist=ffn_hist_s, tps=tps_ffn_s, convert=True)

    proj, lg = _inproj(xp, row(norm_mix_pre[0]), w_in0, gate, n=n_main, tm=1024, tn=1024)
    a_out = _sgu(proj, row(a_v_norm[0]), ws_prompt, bs_prompt, nblk=2, emit_av=False)[0]
    b_out, gla_p = _gla(proj, lg, jnp.zeros((bp, B_HEADS, B_KEY_DIM, B_VAL_DIM), F32), row(b_out_norm[0]),
                        nseq=bp, t=SEQ, clen=CHUNK, ngrp=4, chain=True)
    xp = _outproj(a_out, b_out, w_out0, xp, row(norm_mix_post[0]), tm=tm)
    xp, ffn0_p = ffn_layer(xp, 0, jnp.zeros((bp, ffn_hist_p, D_FF), F32), step=1, hist=ffn_hist_p, tps=tps_ffn_p,
                           convert=False)
    proj = _inproj(xp, row(norm_mix_pre[1]), w_in1, n=D_MODEL, tm=1024, tn=1024)
    c_out, pool_tail_p = _pool(proj, jnp.zeros((bp, pool_hist_p, C_WIDTH), F32), cmap, row(c_scale[0]),
                               step=1, hist=pool_hist_p, tps=tps_p, tm=tm, pos0=0)
    zero_state = jnp.zeros((bp, 1, S5_CH), F32)
    d_out, s5re_tail_p, s5im_tail_p = _s5(proj, zero_state, zero_state, bblk, cblk, row(s5_d[0]), wglu, k_prompt,
                                          step=1, tps=tps_p, tm=tm)
    xp = _outproj(c_out, d_out, w_out1, xp, row(norm_mix_post[1]), tm=tm)
    xp, ffn1_p = ffn_layer(xp, 1, jnp.zeros((bp, ffn_hist_p, D_FF), F32), step=1, hist=ffn_hist_p, tps=tps_ffn_p,
                           convert=False)

    last = slice(tps_p - 1, None, tps_p)
    y_prompt = xp.reshape(bp, SEQ, D_MODEL)
    gla_prompt = gla_p[None]
    pool_prompt = pool_tail_p[last, pool_hist_p - POOL_BUF:][None]
    s5_re_prompt = s5re_tail_p[last, 0].reshape(1, bp, S5_GROUPS, S5_STATE)
    s5_im_prompt = s5im_tail_p[last, 0].reshape(1, bp, S5_GROUPS, S5_STATE)
    last_ffn = slice(tps_ffn_p - 1, None, tps_ffn_p)
    ffn_prompt = jnp.stack([ffn0_p[last_ffn, ffn_hist_p - 2:], ffn1_p[last_ffn, ffn_hist_p - 2:]])

    def batch_major(a, nt):
        return jnp.swapaxes(a.reshape(nt, nb, a.shape[-1]), 0, 1)

    y_sample = batch_major(xs, ts)
    gla_sample = gla_s[None]
    av_sample = a_v.reshape(1, nb, ts, A_WIDTH)
    pool_sample = batch_major(pool_tail_s[-1, step_s:], POOL_BUF)[None]
    s5_re_sample = s5re_tail_s[-1].reshape(1, nb, S5_GROUPS, S5_STATE)
    s5_im_sample = s5im_tail_s[-1].reshape(1, nb, S5_GROUPS, S5_STATE)
    ffn_sample = jnp.stack([batch_major(ffn0_s[-1], 2), batch_major(ffn1_s[-1], 2)])

    return (y_prompt, y_sample, gla_prompt, gla_sample, av_sample, pool_prompt, pool_sample,
            s5_re_prompt, s5_im_prompt, s5_re_sample, s5_im_sample, ffn_prompt, ffn_sample)
```

```python
import functools
import math

import jax
import jax.numpy as jnp
from jax import lax
from jax.experimental import pallas as pl
from jax.experimental.pallas import tpu as pltpu

F32 = jnp.float32
BF16 = jnp.bfloat16

D_MODEL = 2048
SEQ = 4096
DEC_BATCH = 32
DEC_SEQ = 32
PAST_LEN = 4096
CHUNK = 64
A_WIDTH = 1024
A_HEADS = 8
A_BLOCK = 128
B_HEADS = 4
B_KEY_DIM = 128
B_KEY_WIDTH = 512
B_VAL_DIM = 256
B_WIDTH = 1024
B_GATE_RANK = 16
B_GATE_TAU = 16.0
C_WIDTH = 1024
C_GROUP_DIM = 256
POOL_WINDOWS = (2, 4, 8, 16)
POOL_BUF = 15
D_WIDTH = 1024
S5_GROUPS = 64
S5_GROUP_DIM = 16
S5_STATE = 64
S5_CH = S5_GROUPS * S5_STATE
D_FF = 5632
EPS = 1e-6

LANES = 128
SUBLANES = 8
VMEM_LIMIT = 56 * 1024 * 1024


def _params(sem):
    return pltpu.CompilerParams(dimension_semantics=sem, vmem_limit_bytes=VMEM_LIMIT)


def _rms(x, g):
    return x * lax.rsqrt(jnp.mean(x * x, axis=-1, keepdims=True) + EPS) * g


def _dot(a, b):
    return jnp.dot(a, b, preferred_element_type=F32)


def _inproj_kernel(x_ref, g_ref, w_ref, *rest, with_gate, convert):
    rest = list(rest)
    if with_gate:
        wlr_ref, wgate_ref, gbias_ref = rest[:3]
        rest = rest[3:]
    o_ref = rest.pop(0)
    if with_gate:
        lg_ref = rest.pop(0)
    if convert:
        w16_ref = rest.pop(0)
    h_ref, = rest
    j = pl.program_id(1)

    @pl.when(j == 0)
    def _():
        hb = _rms(x_ref[...], g_ref[...]).astype(BF16)
        h_ref[...] = hb
        if with_gate:
            glr = _dot(hb, wlr_ref[...])
            z = _dot(glr.astype(BF16), wgate_ref[...]) + gbias_ref[...]
            lg_ref[...] = (jnp.minimum(z, 0.0) - jnp.log(1.0 + jnp.exp(-jnp.abs(z)))) * (1.0 / B_GATE_TAU)

    w = w_ref[...]
    if convert:
        w = w.astype(BF16)
        w16_ref[...] = w
    o_ref[...] = _dot(h_ref[...], w)


def _inproj(x, g, w, gate=None, *, n, tm, tn, convert=False):
    m, d = x.shape
    grid = (m // tm, n // tn)
    one_tile = m == tm
    in_specs = [pl.BlockSpec((tm, d), lambda i, j: (i, 0), pipeline_mode=pl.Buffered(1) if one_tile else None),
                pl.BlockSpec((1, d), lambda i, j: (0, 0)),
                pl.BlockSpec((None, d, tn), lambda i, j: (0, 0, j)) if convert
                else pl.BlockSpec((d, tn), lambda i, j: (0, j))]
    out_shape = [jax.ShapeDtypeStruct((m, n), F32)]
    out_specs = [pl.BlockSpec((tm, tn), lambda i, j: (i, j))]
    args = [x, g, w]
    if gate is not None:
        wlr, wgate, gbias = gate
        in_specs += [pl.BlockSpec(wlr.shape, lambda i, j: (0, 0)),
                     pl.BlockSpec(wgate.shape, lambda i, j: (0, 0)),
                     pl.BlockSpec(gbias.shape, lambda i, j: (0, 0))]
        out_shape.append(jax.ShapeDtypeStruct((m, B_KEY_WIDTH), F32))
        out_specs.append(pl.BlockSpec((tm, B_KEY_WIDTH), lambda i, j: (i, 0)))
        args += [wlr, wgate, gbias]
    if convert:
        out_shape.append(jax.ShapeDtypeStruct((d, n), BF16))
        out_specs.append(pl.BlockSpec((d, tn), lambda i, j: (0, j)))
    res = pl.pallas_call(
        functools.partial(_inproj_kernel, with_gate=gate is not None, convert=convert),
        out_shape=out_shape, grid=grid, in_specs=in_specs, out_specs=out_specs,
        scratch_shapes=[pltpu.VMEM((tm, d), BF16)],
        compiler_params=_params(("parallel", "arbitrary")),
        name=("inproj_gate" if gate is not None else "inproj") + ("_convert" if convert else ""),
    )(*args)
    return res if len(res) > 1 else res[0]


def _sgu_kernel(u_ref, v_ref, gain_ref, w_ref, b_ref, o_ref, *av_ref, nblk):
    for n in range(nblk):
        rows = slice(n * A_BLOCK, (n + 1) * A_BLOCK)
        v = jax.nn.gelu(v_ref[rows, :])
        mu = jnp.mean(v, axis=-1, keepdims=True)
        vc = v - mu
        vn = vc * lax.rsqrt(jnp.mean(vc * vc, axis=-1, keepdims=True) + EPS) * gain_ref[...]
        if av_ref:
            av_ref[0][rows, :] = vn
        vb = vn.astype(BF16)
        for h in range(A_HEADS):
            cols = slice(h * LANES, (h + 1) * LANES)
            s = _dot(w_ref[h], vb[:, cols]) + b_ref[:, h:h + 1]
            o_ref[rows, cols] = (jax.nn.gelu(u_ref[rows, cols]) * s).astype(o_ref.dtype)


def _sgu(proj, gain, w, b, *, nblk, emit_av):
    m = proj.shape[0]
    tm = nblk * A_BLOCK
    n_out = 2 if emit_av else 1
    return pl.pallas_call(
        functools.partial(_sgu_kernel, nblk=nblk),
        out_shape=[jax.ShapeDtypeStruct((m, A_WIDTH), BF16), jax.ShapeDtypeStruct((m, A_WIDTH), F32)][:n_out],
        grid=(m // tm,),
        in_specs=[pl.BlockSpec((tm, A_WIDTH), lambda i: (i, 0)),
                  pl.BlockSpec((tm, A_WIDTH), lambda i: (i, 1)),
                  pl.BlockSpec((1, A_WIDTH), lambda i: (0, 0)),
                  pl.BlockSpec(w.shape, lambda i: (0, 0, 0)),
                  pl.BlockSpec(b.shape, lambda i: (0, 0))],
        out_specs=[pl.BlockSpec((tm, A_WIDTH), lambda i: (i, 0)),
                   pl.BlockSpec((tm, A_WIDTH), lambda i: (i, 0))][:n_out],
        compiler_params=_params(("parallel",)),
        name="sgu",
    )(proj, proj, gain, w, b)


def _gla_kernel(q_ref, k_ref, v_ref, r_ref, lg_ref, s0_ref, og_ref, o_ref, sout_ref, st_ref, *, clen, ngrp, chain):
    c = pl.program_id(1)
    rows_all = ngrp * clen
    shift = int(math.log2(clen))

    if chain:
        @pl.when(c == 0)
        def _():
            for h in range(B_HEADS):
                st_ref[h] = s0_ref[0, h].T

    row_i = lax.broadcasted_iota(jnp.int32, (rows_all, rows_all), 0)
    col_i = lax.broadcasted_iota(jnp.int32, (rows_all, rows_all), 1)
    same_group = lax.shift_right_logical(row_i, shift) == lax.shift_right_logical(col_i, shift)
    tri = ((row_i >= col_i) & same_group).astype(F32)
    cum = jnp.dot(tri, lg_ref[...], precision=lax.Precision.HIGHEST,
                  preferred_element_type=F32)
    tots = [cum[(g + 1) * clen - 1:(g + 1) * clen, :] for g in range(ngrp)]
    tot_rows = jnp.concatenate([jnp.broadcast_to(t, (clen, B_KEY_WIDTH)) for t in tots], axis=0)
    kd = (k_ref[...] * jnp.exp(tot_rows - cum)).astype(BF16)
    qs = (q_ref[...] * (B_KEY_DIM ** -0.5)).astype(BF16)
    vb = v_ref[...].astype(BF16)
    sr = jax.nn.silu(r_ref[...])
    grp = lax.shift_right_logical(lax.broadcasted_iota(jnp.int32, (rows_all, 1), 0), shift)
    zero = jnp.zeros((), BF16)

    def by_group(x):
        return jnp.concatenate([jnp.where(grp == g, x, zero) for g in range(ngrp)], axis=1)

    for h in range(B_HEADS):
        kc = slice(h * B_KEY_DIM, (h + 1) * B_KEY_DIM)
        vc = slice(h * B_VAL_DIM, (h + 1) * B_VAL_DIM)
        upd = lax.dot_general(vb[:, vc], by_group(kd[:, kc]), (((0,), (0,)), ((), ())),
                              preferred_element_type=F32)
        states = []
        st = st_ref[h] if chain else None
        for g in range(ngrp):
            if not chain:
                st = s0_ref[g, h].T
            st = jnp.exp(tots[g][:, kc]) * st + upd[:, g * B_KEY_DIM:(g + 1) * B_KEY_DIM]
            states.append(st.astype(BF16))
            if not chain:
                sout_ref[g, h] = st.T
        if chain:
            st_ref[h] = st
        o = lax.dot_general(by_group(qs[:, kc]), jnp.concatenate(states, axis=1), (((1,), (1,)), ((), ())),
                            preferred_element_type=F32)
        o = o * lax.rsqrt(jnp.mean(o * o, axis=-1, keepdims=True) + EPS)
        o = o * og_ref[:, vc] * sr[:, vc]
        o_ref[:, vc] = o.astype(o_ref.dtype)

    if chain:
        @pl.when(c == pl.num_programs(1) - 1)
        def _():
            for h in range(B_HEADS):
                sout_ref[0, h] = st_ref[h].T


def _gla(proj, lg, s0, og, *, nseq, t, clen, ngrp, chain):
    rows = clen * ngrp
    m = nseq * t
    if chain:
        steps = t // rows
        grid = (nseq, steps)
        rmap = lambda b, c: b * steps + c
        nstate = 1
    else:
        grid = (m // rows, 1)
        rmap = lambda b, c: b
        nstate = ngrp
    blk = lambda width, col: pl.BlockSpec((rows, width), lambda b, c: (rmap(b, c), col))
    state_spec = pl.BlockSpec((nstate, B_HEADS, B_KEY_DIM, B_VAL_DIM), lambda b, c: (b, 0, 0, 0))
    return pl.pallas_call(
        functools.partial(_gla_kernel, clen=clen, ngrp=ngrp, chain=chain),
        out_shape=[jax.ShapeDtypeStruct((m, B_WIDTH), BF16),
                   jax.ShapeDtypeStruct((nseq, B_HEADS, B_KEY_DIM, B_VAL_DIM), F32)],
        grid=grid,
        in_specs=[blk(B_KEY_WIDTH, 4),
                  blk(B_KEY_WIDTH, 5),
                  blk(B_WIDTH, 3),
                  blk(B_WIDTH, 4),
                  blk(B_KEY_WIDTH, 0),
                  state_spec,
                  pl.BlockSpec((1, B_WIDTH), lambda b, c: (0, 0))],
        out_specs=[blk(B_WIDTH, 0), state_spec],
        scratch_shapes=[pltpu.VMEM((B_HEADS, B_VAL_DIM, B_KEY_DIM), F32)],
        compiler_params=_params(("parallel", "arbitrary")),
        name="gla",
    )(proj, proj, proj, proj, lg, s0, og)


def _outproj_kernel(a_ref, b_ref, w_ref, x_ref, g_ref, o_ref, *w16_ref, convert):
    wb_ref = w_ref
    if convert:
        wb_ref, = w16_ref

        @pl.when(pl.program_id(0) == 0)
        def _():
            wb_ref[...] = w_ref[...].astype(BF16)

    ka = a_ref.shape[1]
    y = _dot(a_ref[...], wb_ref[0:ka, :]) + _dot(b_ref[...], wb_ref[ka:, :])
    o_ref[...] = x_ref[...] + _rms(y, g_ref[...])


def _outproj(a, b, w, x, g, *, tm, convert=False):
    m, d = x.shape
    k = a.shape[1] + b.shape[1]
    once = pl.Buffered(1)
    out_shape = [jax.ShapeDtypeStruct((m, d), F32)]
    out_specs = [pl.BlockSpec((tm, d), lambda i: (i, 0))]
    if convert:
        w_spec = pl.BlockSpec((None, k, d), lambda i: (0, 0, 0), pipeline_mode=once)
        out_shape.append(jax.ShapeDtypeStruct((k, d), BF16))
        out_specs.append(pl.BlockSpec((k, d), lambda i: (0, 0), pipeline_mode=once))
    else:
        w_spec = pl.BlockSpec((k, d), lambda i: (0, 0), pipeline_mode=once)
    res = pl.pallas_call(
        functools.partial(_outproj_kernel, convert=convert),
        out_shape=out_shape,
        grid=(m // tm,),
        in_specs=[pl.BlockSpec((tm, a.shape[1]), lambda i: (i, 0)),
                  pl.BlockSpec((tm, b.shape[1]), lambda i: (i, 0)),
                  w_spec,
                  pl.BlockSpec((tm, d), lambda i: (i, 0)),
                  pl.BlockSpec((1, d), lambda i: (0, 0))],
        out_specs=out_specs,
        compiler_params=_params(("arbitrary",)),
        name="outproj_convert" if convert else "outproj",
    )(a, b, w, x, g)
    return res if convert else res[0]


def _ffn_kernel(x_ref, gpre_ref, wg_ref, wv_ref, cw_ref, cb_ref, wd_ref, gpost_ref, st_ref,
                o_ref, tail_ref, *rest, step, hist, tps, tm, nf, convert):
    if convert:
        wg_out, wv_out, wd_out, h_ref, gext_ref, carry_ref, act_a, act_b = rest
    else:
        h_ref, gext_ref, carry_ref, act_a, act_b = rest
    i = pl.program_id(0)
    j = pl.program_id(1)
    first = (i % tps) == 0

    def up_and_gate(act_ref):
        hb = h_ref[...]
        wg, wv = wg_ref[...], wv_ref[...]
        if convert:
            wg, wv = wg.astype(BF16), wv.astype(BF16)
            wg_out[...] = wg
            wv_out[...] = wv
        gate = _dot(hb, wg)
        val = _dot(hb, wv)
        gext_ref[0:hist, :] = jnp.where(first, st_ref[0], carry_ref[j])
        gext_ref[hist:hist + tm, :] = gate
        prev2 = gext_ref[hist - 2 * step:hist - 2 * step + tm, :]
        prev1 = gext_ref[hist - step:hist - step + tm, :]
        conv = cb_ref[...] + cw_ref[0:1, :] * prev2 + cw_ref[1:2, :] * prev1 + cw_ref[2:3, :] * gate
        act_ref[...] = (jax.nn.gelu(conv) * val).astype(BF16)
        tail = gate[tm - hist:, :]
        carry_ref[j] = tail
        tail_ref[0] = tail

    def down(act_ref):
        wd = wd_ref[...]
        if convert:
            wd = wd.astype(BF16)
            wd_out[...] = wd
        o_ref[...] += _dot(act_ref[...], wd)

    @pl.when(j == 0)
    def _():
        h_ref[...] = _rms(x_ref[...], gpre_ref[...]).astype(BF16)
        o_ref[...] = jnp.zeros_like(o_ref)

        @pl.when(i == 0)
        def _():
            carry_ref[...] = jnp.zeros_like(carry_ref)

        up_and_gate(act_a)

    for parity, (src, dst) in enumerate(((act_b, act_a), (act_a, act_b))):
        @pl.when((j > 0) & (j < nf) & (j % 2 == parity))
        def _(src=src, dst=dst):
            down(src)
            up_and_gate(dst)

    @pl.when(j == nf)
    def _():
        down(act_a if (nf - 1) % 2 == 0 else act_b)
        o_ref[...] = x_ref[...] + _rms(o_ref[...], gpost_ref[...])


def _ffn(x, gpre, weights, cw, cb, gpost, state, *, layer, step, hist, tps, tm, tf, convert):
    m, d = x.shape
    nf = D_FF // tf
    nm = m // tm
    up = lambda j: jnp.minimum(j, nf - 1)
    down = lambda j: jnp.maximum(j - 1, 0)
    if convert:
        w_up, w_down = weights
        w_args = (w_up, w_up, w_down)
        w_specs = [pl.BlockSpec((None, d, tf), lambda i, j: (layer, 0, up(j))),
                   pl.BlockSpec((None, d, tf), lambda i, j: (layer, 0, nf + up(j))),
                   pl.BlockSpec((None, tf, d), lambda i, j: (layer, down(j), 0))]
        extra_shapes = [jax.ShapeDtypeStruct((d, D_FF), BF16), jax.ShapeDtypeStruct((d, D_FF), BF16),
                        jax.ShapeDtypeStruct((D_FF, d), BF16)]
        extra_specs = [pl.BlockSpec((d, tf), lambda i, j: (0, up(j))),
                       pl.BlockSpec((d, tf), lambda i, j: (0, up(j))),
                       pl.BlockSpec((tf, d), lambda i, j: (down(j), 0))]
    else:
        w_args = weights
        w_specs = [pl.BlockSpec((d, tf), lambda i, j: (0, up(j))),
                   pl.BlockSpec((d, tf), lambda i, j: (0, up(j))),
                   pl.BlockSpec((tf, d), lambda i, j: (down(j), 0))]
        extra_shapes, extra_specs = [], []
    return pl.pallas_call(
        functools.partial(_ffn_kernel, step=step, hist=hist, tps=tps, tm=tm, nf=nf, convert=convert),
        out_shape=[jax.ShapeDtypeStruct((m, d), F32), jax.ShapeDtypeStruct((nm, hist, D_FF), F32)] + extra_shapes,
        grid=(nm, nf + 1),
        in_specs=[pl.BlockSpec((tm, d), lambda i, j: (i, 0)),
                  pl.BlockSpec((None, 1, d), lambda i, j: (layer, 0, 0)),
                  w_specs[0], w_specs[1],
                  pl.BlockSpec((None, 3, tf), lambda i, j: (layer, 0, up(j))),
                  pl.BlockSpec((None, 1, tf), lambda i, j: (layer, 0, up(j))),
                  w_specs[2],
                  pl.BlockSpec((None, 1, d), lambda i, j: (layer, 0, 0)),
                  pl.BlockSpec((1, hist, tf), lambda i, j: (i // tps, 0, up(j)))],
        out_specs=[pl.BlockSpec((tm, d), lambda i, j: (i, 0), pipeline_mode=pl.Buffered(1)),
                   pl.BlockSpec((1, hist, tf), lambda i, j: (i, 0, up(j)))] + extra_specs,
        scratch_shapes=[pltpu.VMEM((tm, d), BF16),
                        pltpu.VMEM((hist + tm, tf), F32), pltpu.VMEM((nf, hist, tf), F32),
                        pltpu.VMEM((tm, tf), BF16), pltpu.VMEM((tm, tf), BF16)],
        compiler_params=_params(("arbitrary", "arbitrary")),
        name="ffn_convert" if convert else "ffn",
    )(x, gpre, w_args[0], w_args[1], cw, cb, w_args[2], gpost, state)


def _pool_kernel(c_ref, st_ref, cmap_ref, cs_ref, o_ref, tail_ref, ext_ref, carry_ref,
                 *, step, hist, tps, tm, pos0):
    i = pl.program_id(0)
    first = (i % tps) == 0

    @pl.when(first)
    def _():
        ext_ref[0:hist, :] = st_ref[0]

    @pl.when(jnp.logical_not(first))
    def _():
        ext_ref[0:hist, :] = carry_ref[...]

    ext_ref[hist:hist + tm, :] = c_ref[...]
    row = lax.broadcasted_iota(jnp.int32, (tm, 1), 0)
    if step > 1:
        row = lax.shift_right_logical(row, int(math.log2(step)))
    pos = pos0 + (i % tps) * (tm // step) + row
    for g, win in enumerate(POOL_WINDOWS):
        cols = slice(g * C_GROUP_DIM, (g + 1) * C_GROUP_DIM)
        cur = ext_ref[hist:hist + tm, cols]
        tot = cur
        for k in range(1, win):
            tot = tot + ext_ref[hist - k * step:hist - k * step + tm, cols]
        cnt = jnp.minimum(pos + 1, win).astype(F32)
        delta = tot / cnt - cur
        y = _dot(delta.astype(BF16), cmap_ref[g]) * cs_ref[:, cols]
        o_ref[:, cols] = y.astype(o_ref.dtype)
    tail = ext_ref[tm:tm + hist, :]
    carry_ref[...] = tail
    tail_ref[0] = tail


def _pool(proj, state, cmap, cscale, *, step, hist, tps, tm, pos0):
    m = proj.shape[0]
    nm = m // tm
    return pl.pallas_call(
        functools.partial(_pool_kernel, step=step, hist=hist, tps=tps, tm=tm, pos0=pos0),
        out_shape=[jax.ShapeDtypeStruct((m, C_WIDTH), BF16), jax.ShapeDtypeStruct((nm, hist, C_WIDTH), F32)],
        grid=(nm,),
        in_specs=[pl.BlockSpec((tm, C_WIDTH), lambda i: (i, 0)),
                  pl.BlockSpec((1, hist, C_WIDTH), lambda i: (i // tps, 0, 0)),
                  pl.BlockSpec(cmap.shape, lambda i: (0, 0, 0)),
                  pl.BlockSpec((1, C_WIDTH), lambda i: (0, 0))],
        out_specs=[pl.BlockSpec((tm, C_WIDTH), lambda i: (i, 0)),
                   pl.BlockSpec((1, hist, C_WIDTH), lambda i: (i, 0, 0))],
        scratch_shapes=[pltpu.VMEM((hist + tm, C_WIDTH), F32), pltpu.VMEM((hist, C_WIDTH), F32)],
        compiler_params=_params(("arbitrary",)),
        name="pool",
    )(proj, state, cmap, cscale)


S5_CHUNK = 512
S5_NCHUNK = S5_CH // S5_CHUNK
S5_FOLD = SUBLANES // 2


def _s5_kernel(u_ref, s0re_ref, s0im_ref, bblk_ref, cblk_ref, dskip_ref, wglu_ref, kc_ref,
               o_ref, tre_ref, tim_ref, *scratch, step, tps, tm):
    sre_refs, sim_refs = scratch[:S5_NCHUNK], scratch[S5_NCHUNK:2 * S5_NCHUNK]
    cre_ref, cim_ref, g_ref = scratch[2 * S5_NCHUNK:]
    i = pl.program_id(0)
    last = pl.num_programs(0) - 1
    crow = cre_ref.shape[0]

    def init_carry():
        @pl.when((i % tps) == 0)
        def _():
            cre_ref[...] = jnp.broadcast_to(s0re_ref[0], cre_ref.shape) if step == 1 else s0re_ref[0]
            cim_ref[...] = jnp.broadcast_to(s0im_ref[0], cim_ref.shape) if step == 1 else s0im_ref[0]

    half = S5_CHUNK // 2
    glu_tiles = 2 * D_WIDTH // half

    def glu_piece(zs, p):
        zs[p] = _dot(g_ref[...], wglu_ref[:, p * half:(p + 1) * half])
        q = p - glu_tiles // 2
        if q >= 0:
            o_ref[:, q * half:(q + 1) * half] = (zs[q] * jax.nn.sigmoid(zs[p])).astype(o_ref.dtype)

    def scan_tile(with_glu):
        u = u_ref[...]
        lhs = [u.astype(BF16)]
        if step == 1:
            row_in_block = lax.broadcasted_iota(jnp.int32, (tm, 1), 0) & (SUBLANES - 1)
            for k in range(1, S5_FOLD):
                lhs.append(jnp.where(row_in_block >= k, pltpu.roll(u, k, 0), 0.0).astype(BF16))

        def in_map_piece(m, p):
            ucols = slice(m * LANES, (m + 1) * LANES)
            pcols = slice(p * half, (p + 1) * half)
            if step == 1:
                r = _dot(jnp.concatenate([x[:, ucols] for x in lhs], axis=1), bblk_ref[m, :, pcols])
            else:
                r = _dot(lhs[0][:, ucols], bblk_ref[m, 0:LANES, pcols])
            dst = sre_refs[m] if p < 2 else sim_refs[m]
            dst[:, (p % 2) * half:(p % 2 + 1) * half] = r

        yparts = [[None, None] for _ in range(S5_NCHUNK)]

        def out_map_piece(m, part):
            src = sre_refs[m] if part == 0 else sim_refs[m]
            yparts[m][part] = _dot(src[...].astype(BF16), cblk_ref[m, part * S5_CHUNK:(part + 1) * S5_CHUNK, :])

        zs = [None] * glu_tiles
        for p in range(4):
            in_map_piece(0, p)
        for m in range(S5_NCHUNK):
            cols = slice(m * S5_CHUNK, (m + 1) * S5_CHUNK)
            sre_ref, sim_ref = sre_refs[m], sim_refs[m]
            side = []
            if m + 1 < S5_NCHUNK:
                side += [functools.partial(in_map_piece, m + 1, p) for p in range(4)]
            if m >= 1:
                side += [functools.partial(out_map_piece, m - 1, part) for part in range(2)]
            if with_glu:
                side += [functools.partial(glu_piece, zs, m)]
            nsteps = tm // (SUBLANES if step == 1 else step)
            issue_at = {(k * nsteps) // len(side): fn for k, fn in enumerate(side)}
            cr, ci = cre_ref[:, cols], cim_ref[:, cols]
            if step == 1:
                ar, ai, pwr, pwi = [kc_ref[k, :, cols] for k in range(4)]
                for rb in range(tm // SUBLANES):
                    if rb in issue_at:
                        issue_at[rb]()
                    rows = slice(rb * SUBLANES, (rb + 1) * SUBLANES)
                    xr = sre_ref[rows, :]
                    xi = sim_ref[rows, :]
                    sr = pltpu.roll(xr, S5_FOLD, 0)
                    si = pltpu.roll(xi, S5_FOLD, 0)
                    xr, xi = xr + (ar * sr - ai * si), xi + (ar * si + ai * sr)
                    xr, xi = xr + (pwr * cr - pwi * ci), xi + (pwr * ci + pwi * cr)
                    sre_ref[rows, :] = xr
                    sim_ref[rows, :] = xi
                    cr = jnp.broadcast_to(xr[SUBLANES - 1:SUBLANES, :], xr.shape)
                    ci = jnp.broadcast_to(xi[SUBLANES - 1:SUBLANES, :], xi.shape)
            else:
                lr = jnp.broadcast_to(kc_ref[0, 0:1, cols], (crow, S5_CHUNK))
                li = jnp.broadcast_to(kc_ref[1, 0:1, cols], (crow, S5_CHUNK))
                for t in range(tm // step):
                    if t in issue_at:
                        issue_at[t]()
                    rows = slice(t * step, (t + 1) * step)
                    cr, ci = (sre_ref[rows, :] + (lr * cr - li * ci),
                              sim_ref[rows, :] + (lr * ci + li * cr))
                    sre_ref[rows, :] = cr
                    sim_ref[rows, :] = ci
            cre_ref[:, cols] = cr
            cim_ref[:, cols] = ci

        for part in range(2):
            out_map_piece(S5_NCHUNK - 1, part)
        tre_ref[0] = cre_ref[...]
        tim_ref[0] = cim_ref[...]
        y = jnp.concatenate([re + im for re, im in yparts], axis=1) + dskip_ref[...] * u
        g_ref[...] = jax.nn.gelu(y).astype(BF16)

    @pl.when(i == 0)
    def _():
        init_carry()
        scan_tile(False)

    @pl.when((i > 0) & (i < last))
    def _():
        init_carry()
        scan_tile(True)

    @pl.when(i == last)
    def _():
        zs = [None] * glu_tiles
        for p in range(glu_tiles):
            glu_piece(zs, p)


def _s5(proj, s0re, s0im, bblk, cblk, dskip, wglu, kconst, *, step, tps, tm):
    m = proj.shape[0]
    nm = m // tm
    crow = s0re.shape[1] if step > 1 else SUBLANES
    srow = s0re.shape[1]
    scanned = lambda i: jnp.minimum(i, nm - 1)
    return pl.pallas_call(
        functools.partial(_s5_kernel, step=step, tps=tps, tm=tm),
        out_shape=[jax.ShapeDtypeStruct((m, D_WIDTH), BF16),
                   jax.ShapeDtypeStruct((nm, crow, S5_CH), F32),
                   jax.ShapeDtypeStruct((nm, crow, S5_CH), F32)],
        grid=(nm + 1,),
        in_specs=[pl.BlockSpec((tm, D_WIDTH), lambda i: (scanned(i), 1)),
                  pl.BlockSpec((1, srow, S5_CH), lambda i: (scanned(i) // tps, 0, 0)),
                  pl.BlockSpec((1, srow, S5_CH), lambda i: (scanned(i) // tps, 0, 0)),
                  pl.BlockSpec(bblk.shape, lambda i: (0, 0, 0), pipeline_mode=pl.Buffered(1)),
                  pl.BlockSpec(cblk.shape, lambda i: (0, 0, 0), pipeline_mode=pl.Buffered(1)),
                  pl.BlockSpec((1, D_WIDTH), lambda i: (0, 0)),
                  pl.BlockSpec(wglu.shape, lambda i: (0, 0), pipeline_mode=pl.Buffered(1)),
                  pl.BlockSpec(kconst.shape, lambda i: (0, 0, 0), pipeline_mode=pl.Buffered(1))],
        out_specs=[pl.BlockSpec((tm, D_WIDTH), lambda i: (jnp.maximum(i - 1, 0), 0)),
                   pl.BlockSpec((1, crow, S5_CH), lambda i: (scanned(i), 0, 0)),
                   pl.BlockSpec((1, crow, S5_CH), lambda i: (scanned(i), 0, 0))],
        scratch_shapes=[pltpu.VMEM((tm, S5_CHUNK), F32)] * (2 * S5_NCHUNK)
                       + [pltpu.VMEM((crow, S5_CH), F32), pltpu.VMEM((crow, S5_CH), F32),
                          pltpu.VMEM((tm, D_WIDTH), BF16)],
        compiler_params=_params(("arbitrary",)),
        name="s5",
    )(proj, s0re, s0im, bblk, cblk, dskip, wglu, kconst)


def _block_diag(blocks, per):
    n, r, c = blocks.shape
    b = blocks.reshape(n // per, per, r, 1, c)
    on_diag = jnp.arange(per)[:, None, None, None] == jnp.arange(per)[None, None, :, None]
    return jnp.where(on_diag, b, 0.0).reshape(n // per, per * r, per * c)


def _s5_constants(a_re, a_im, log_dt, b_re, b_im, c_re, c_im):
    dt = jnp.exp(log_dt)[:, None]
    zr, zi = a_re * dt, a_im * dt
    mag = jnp.exp(zr)
    lr, li = mag * jnp.cos(zi), mag * jnp.sin(zi)
    den = a_re * a_re + a_im * a_im
    nr, ni = lr - 1.0, li
    kr, ki = (nr * a_re + ni * a_im) / den, (ni * a_re - nr * a_im) / den
    bbr = kr[..., None] * b_re - ki[..., None] * b_im
    bbi = kr[..., None] * b_im + ki[..., None] * b_re
    per = S5_CHUNK // S5_STATE
    blocks = []
    qr, qi = jnp.ones_like(lr), jnp.zeros_like(li)
    for _ in range(S5_FOLD):
        fr = qr[..., None] * bbr - qi[..., None] * bbi
        fi = qr[..., None] * bbi + qi[..., None] * bbr
        blocks.append(jnp.concatenate([_block_diag(jnp.swapaxes(fr, 1, 2), per),
                                       _block_diag(jnp.swapaxes(fi, 1, 2), per)], axis=2))
        qr, qi = qr * lr - qi * li, qr * li + qi * lr
    bblk = jnp.concatenate(blocks, axis=1).astype(BF16)
    cblk = jnp.concatenate([_block_diag(jnp.swapaxes(c_re, 1, 2), per),
                            _block_diag(jnp.swapaxes(-c_im, 1, 2), per)], axis=1).astype(BF16)
    lr, li = lr.reshape(1, S5_CH), li.reshape(1, S5_CH)
    pr, pi = [lr], [li]
    for _ in range(SUBLANES - 1):
        pr, pi = pr + [pr[-1] * lr - pi[-1] * li], pi + [pr[-1] * li + pi[-1] * lr]
    rowid = jnp.arange(SUBLANES)[:, None]

    def masked(p, d):
        return jnp.where(rowid >= d, jnp.broadcast_to(p[d - 1], (SUBLANES, S5_CH)), 0.0)

    k_prompt = jnp.stack([masked(pr, S5_FOLD), masked(pi, S5_FOLD),
                          jnp.concatenate(pr, axis=0), jnp.concatenate(pi, axis=0)])
    k_sample = jnp.stack([jnp.broadcast_to(lr, (SUBLANES, S5_CH)), jnp.broadcast_to(li, (SUBLANES, S5_CH))])
    return bblk, cblk, k_prompt, k_sample


def _time_major(a):
    a = jnp.swapaxes(a, 0, 1)
    return a.reshape((a.shape[0] * a.shape[1],) + a.shape[2:])


def kernel(x_prompt, x_sample, state_gla, state_pool, state_s5_re, state_s5_im, state_ffn_conv, norm_mix_pre, norm_mix_post, norm_ffn_pre, norm_ffn_post, w_in_even, a_w_s, a_b_s, a_v_norm, b_w_gate, b_gate_bias, b_out_norm, w_out_even, w_in_odd, c_map, c_scale, s5_a_re, s5_a_im, s5_log_dt, s5_b_re, s5_b_im, s5_c_re, s5_c_im, s5_d, s5_w_glu, w_out_odd, ffn_w_up, ffn_conv_w, ffn_conv_b, ffn_w_down):
    bp = x_prompt.shape[0]
    nb, ts = x_sample.shape[0], x_sample.shape[1]
    xp = x_prompt.reshape(bp * SEQ, D_MODEL)
    xs = x_sample.reshape(nb * ts, D_MODEL)

    row = lambda v: v.reshape(1, -1)
    n_main = 2 * A_WIDTH + 2 * B_KEY_WIDTH + 2 * B_WIDTH
    w_lr = jnp.pad(w_in_even[0][:, n_main:], ((0, 0), (0, LANES - B_GATE_RANK))).astype(BF16)
    w_gate = jnp.pad(b_w_gate[0], ((0, LANES - B_GATE_RANK), (0, 0))).astype(BF16)
    gate = (w_lr, w_gate, row(b_gate_bias[0]))
    pos = jnp.arange(A_BLOCK)
    causal = (pos[None, :] // CHUNK) <= (pos[:, None] // CHUNK)
    ws_prompt = jnp.where(causal[None], a_w_s[0], 0.0).astype(BF16)
    per = A_BLOCK // ts
    ws_small = jnp.where(causal[None, :ts, :ts], a_w_s[0][:, :ts, :ts], 0.0)
    ws_sample = jnp.einsum('hij,ab->haibj', ws_small, jnp.eye(per, dtype=F32)).reshape(A_HEADS, A_BLOCK, A_BLOCK).astype(BF16)
    bs_prompt = a_b_s[0].T
    bs_sample = jnp.tile(a_b_s[0][:, :ts].T, (per, 1))
    cmap = c_map[0].astype(BF16)
    bblk, cblk, k_prompt, k_sample = _s5_constants(s5_a_re[0], s5_a_im[0], s5_log_dt[0], s5_b_re[0], s5_b_im[0],
                                                   s5_c_re[0], s5_c_im[0])
    wglu = s5_w_glu[0].astype(BF16)

    tm = 512
    tf = 512
    tps_p = SEQ // tm
    ffn_hist_p = SUBLANES
    pool_hist_p = 2 * SUBLANES
    step_s = nb
    tps_s = (nb * ts) // tm
    ffn_hist_s = 2 * step_s
    pool_hist_s = (POOL_BUF + 1) * step_s

    tm_ffn = 1024
    tps_ffn_p = SEQ // tm_ffn
    tps_ffn_s = (nb * ts) // tm_ffn

    ffn_w16 = {}

    def ffn_layer(x, layer, state, *, step, hist, tps, convert):
        weights = (ffn_w_up, ffn_w_down) if convert else ffn_w16[layer]
        res = _ffn(x, norm_ffn_pre[:, None], weights, ffn_conv_w, ffn_conv_b[:, None], norm_ffn_post[:, None],
                   state, layer=layer, step=step, hist=hist, tps=tps, tm=tm_ffn, tf=256 if convert else tf,
                   convert=convert)
        if convert:
            ffn_w16[layer] = tuple(res[2:])
        return res[0], res[1]

    proj, lg, w_in0 = _inproj(xs, row(norm_mix_pre[0]), w_in_even, gate, n=n_main, tm=1024, tn=512, convert=True)
    a_out, a_v = _sgu(proj, row(a_v_norm[0]), ws_sample, bs_sample, nblk=2, emit_av=True)
    b_out, gla_s = _gla(proj, lg, state_gla[0], row(b_out_norm[0]), nseq=nb, t=ts, clen=ts, ngrp=8, chain=False)
    xs, w_out0 = _outproj(a_out, b_out, w_out_even, xs, row(norm_mix_post[0]), tm=tm, convert=True)
    xs = _time_major(xs.reshape(nb, ts, D_MODEL))
    ffn_state = lambda layer: _time_major(state_ffn_conv[layer])[None]
    xs, ffn0_s = ffn_layer(xs, 0, ffn_state(0), step=step_s, hist=ffn_hist_s, tps=tps_ffn_s, convert=True)
    proj, w_in1 = _inproj(xs, row(norm_mix_pre[1]), w_in_odd, n=D_MODEL, tm=1024, tn=512, convert=True)
    pool_state = jnp.pad(_time_major(state_pool[0]), ((step_s, 0), (0, 0)))[None]
    c_out, pool_tail_s = _pool(proj, pool_state, cmap, row(c_scale[0]),
                               step=step_s, hist=pool_hist_s, tps=tps_s, tm=tm, pos0=PAST_LEN)
    d_out, s5re_tail_s, s5im_tail_s = _s5(proj, state_s5_re[0].reshape(1, nb, S5_CH),
                                          state_s5_im[0].reshape(1, nb, S5_CH),
                                          bblk, cblk, row(s5_d[0]), wglu, k_sample, step=step_s, tps=tps_s, tm=tm)
    xs, w_out1 = _outproj(c_out, d_out, w_out_odd, xs, row(norm_mix_post[1]), tm=tm, convert=True)
    xs, ffn1_s = ffn_layer(xs, 1, ffn_state(1), step=step_s, hist=ffn_hist_s, tps=tps_ffn_s, convert=True)

    proj, lg = _inproj(xp, row(norm_mix_pre[0]), w_in0, gate, n=n_main, tm=1024, tn=1024)
    a_out = _sgu(proj, row(a_v_norm[0]), ws_prompt, bs_prompt, nblk=2, emit_av=False)[0]
    b_out, gla_p = _gla(proj, lg, jnp.zeros((bp, B_HEADS, B_KEY_DIM, B_VAL_DIM), F32), row(b_out_norm[0]),
                        nseq=bp, t=SEQ, clen=CHUNK, ngrp=4, chain=True)
    xp = _outproj(a_out, b_out, w_out0, xp, row(norm_mix_post[0]), tm=tm)
    xp, ffn0_p = ffn_layer(xp, 0, jnp.zeros((bp, ffn_hist_p, D_FF), F32), step=1, hist=ffn_hist_p, tps=tps_ffn_p,
                           convert=False)
    proj = _inproj(xp, row(norm_mix_pre[1]), w_in1, n=D_MODEL, tm=1024, tn=1024)
    c_out, pool_tail_p = _pool(proj, jnp.zeros((bp, pool_hist_p, C_WIDTH), F32), cmap, row(c_scale[0]),
                               step=1, hist=pool_hist_p, tps=tps_p, tm=tm, pos0=0)
    zero_state = jnp.zeros((bp, 1, S5_CH), F32)
    d_out, s5re_tail_p, s5im_tail_p = _s5(proj, zero_state, zero_state, bblk, cblk, row(s5_d[0]), wglu, k_prompt,
                                          step=1, tps=tps_p, tm=tm)
    xp = _outproj(c_out, d_out, w_out1, xp, row(norm_mix_post[1]), tm=tm)
    xp, ffn1_p = ffn_layer(xp, 1, jnp.zeros((bp, ffn_hist_p, D_FF), F32), step=1, hist=ffn_hist_p, tps=tps_ffn_p,
                           convert=False)

    last = slice(tps_p - 1, None, tps_p)
    y_prompt = xp.reshape(bp, SEQ, D_MODEL)
    gla_prompt = gla_p[None]
    pool_prompt = pool_tail_p[last, pool_hist_p - POOL_BUF:][None]
    s5_re_prompt = s5re_tail_p[last, 0].reshape(1, bp, S5_GROUPS, S5_STATE)
    s5_im_prompt = s5im_tail_p[last, 0].reshape(1, bp, S5_GROUPS, S5_STATE)
    last_ffn = slice(tps_ffn_p - 1, None, tps_ffn_p)
    ffn_prompt = jnp.stack([ffn0_p[last_ffn, ffn_hist_p - 2:], ffn1_p[last_ffn, ffn_hist_p - 2:]])

    def batch_major(a, nt):
        return jnp.swapaxes(a.reshape(nt, nb, a.shape[-1]), 0, 1)

    y_sample = batch_major(xs, ts)
    gla_sample = gla_s[None]
    av_sample = a_v.reshape(1, nb, ts, A_WIDTH)
    pool_sample = batch_major(pool_tail_s[-1, step_s:], POOL_BUF)[None]
    s5_re_sample = s5re_tail_s[-1].reshape(1, nb, S5_GROUPS, S5_STATE)
    s5_im_sample = s5im_tail_s[-1].reshape(1, nb, S5_GROUPS, S5_STATE)
    ffn_sample = jnp.stack([batch_major(ffn0_s[-1], 2), batch_major(ffn1_s[-1], 2)])

    return (y_prompt, y_sample, gla_prompt, gla_sample, av_sample, pool_prompt, pool_sample,
            s5_re_prompt, s5_im_prompt, s5_re_sample, s5_im_sample, ffn_prompt, ffn_sample)
```

```python
import functools
import math

import jax
import jax.numpy as jnp
from jax import lax
from jax.experimental import pallas as pl
from jax.experimental.pallas import tpu as pltpu

F32 = jnp.float32
BF16 = jnp.bfloat16

D_MODEL = 2048
SEQ = 4096
DEC_BATCH = 32
DEC_SEQ = 32
PAST_LEN = 4096
CHUNK = 64
A_WIDTH = 1024
A_HEADS = 8
A_BLOCK = 128
B_HEADS = 4
B_KEY_DIM = 128
B_KEY_WIDTH = 512
B_VAL_DIM = 256
B_WIDTH = 1024
B_GATE_RANK = 16
B_GATE_TAU = 16.0
C_WIDTH = 1024
C_GROUP_DIM = 256
POOL_WINDOWS = (2, 4, 8, 16)
POOL_BUF = 15
D_WIDTH = 1024
S5_GROUPS = 64
S5_GROUP_DIM = 16
S5_STATE = 64
S5_CH = S5_GROUPS * S5_STATE
D_FF = 5632
EPS = 1e-6

LANES = 128
SUBLANES = 8
VMEM_LIMIT = 56 * 1024 * 1024


def _params(sem):
    return pltpu.CompilerParams(dimension_semantics=sem, vmem_limit_bytes=VMEM_LIMIT)


def _rms(x, g):
    return x * lax.rsqrt(jnp.mean(x * x, axis=-1, keepdims=True) + EPS) * g


def _dot(a, b):
    return jnp.dot(a, b, preferred_element_type=F32)


def _inproj_kernel(x_ref, g_ref, w_ref, *rest, with_gate, convert):
    rest = list(rest)
    if with_gate:
        wlr_ref, wgate_ref, gbias_ref = rest[:3]
        rest = rest[3:]
    o_ref = rest.pop(0)
    if with_gate:
        lg_ref = rest.pop(0)
    if convert:
        w16_ref = rest.pop(0)
    h_ref, = rest
    j = pl.program_id(1)

    @pl.when(j == 0)
    def _():
        hb = _rms(x_ref[...], g_ref[...]).astype(BF16)
        h_ref[...] = hb
        if with_gate:
            glr = _dot(hb, wlr_ref[...])
            z = _dot(glr.astype(BF16), wgate_ref[...]) + gbias_ref[...]
            lg_ref[...] = (jnp.minimum(z, 0.0) - jnp.log(1.0 + jnp.exp(-jnp.abs(z)))) * (1.0 / B_GATE_TAU)

    w = w_ref[...]
    if convert:
        w = w.astype(BF16)
        w16_ref[...] = w
    o_ref[...] = _dot(h_ref[...], w)


def _inproj(x, g, w, gate=None, *, n, tm, tn, convert=False):
    m, d = x.shape
    grid = (m // tm, n // tn)
    one_tile = m == tm
    in_specs = [pl.BlockSpec((tm, d), lambda i, j: (i, 0), pipeline_mode=pl.Buffered(1) if one_tile else None),
                pl.BlockSpec((1, d), lambda i, j: (0, 0)),
                pl.BlockSpec((None, d, tn), lambda i, j: (0, 0, j)) if convert
                else pl.BlockSpec((d, tn), lambda i, j: (0, j))]
    out_shape = [jax.ShapeDtypeStruct((m, n), F32)]
    out_specs = [pl.BlockSpec((tm, tn), lambda i, j: (i, j))]
    args = [x, g, w]
    if gate is not None:
        wlr, wgate, gbias = gate
        in_specs += [pl.BlockSpec(wlr.shape, lambda i, j: (0, 0)),
                     pl.BlockSpec(wgate.shape, lambda i, j: (0, 0)),
                     pl.BlockSpec(gbias.shape, lambda i, j: (0, 0))]
        out_shape.append(jax.ShapeDtypeStruct((m, B_KEY_WIDTH), F32))
        out_specs.append(pl.BlockSpec((tm, B_KEY_WIDTH), lambda i, j: (i, 0)))
        args += [wlr, wgate, gbias]
    if convert:
        out_shape.append(jax.ShapeDtypeStruct((d, n), BF16))
        out_specs.append(pl.BlockSpec((d, tn), lambda i, j: (0, j)))
    res = pl.pallas_call(
        functools.partial(_inproj_kernel, with_gate=gate is not None, convert=convert),
        out_shape=out_shape, grid=grid, in_specs=in_specs, out_specs=out_specs,
        scratch_shapes=[pltpu.VMEM((tm, d), BF16)],
        compiler_params=_params(("parallel", "arbitrary")),
        name=("inproj_gate" if gate is not None else "inproj") + ("_convert" if convert else ""),
    )(*args)
    return res if len(res) > 1 else res[0]


def _sgu_kernel(u_ref, v_ref, gain_ref, w_ref, b_ref, o_ref, *av_ref, nblk):
    for n in range(nblk):
        rows = slice(n * A_BLOCK, (n + 1) * A_BLOCK)
        v = jax.nn.gelu(v_ref[rows, :])
        mu = jnp.mean(v, axis=-1, keepdims=True)
        vc = v - mu
        vn = vc * lax.rsqrt(jnp.mean(vc * vc, axis=-1, keepdims=True) + EPS) * gain_ref[...]
        if av_ref:
            av_ref[0][rows, :] = vn
        vb = vn.astype(BF16)
        for h in range(A_HEADS):
            cols = slice(h * LANES, (h + 1) * LANES)
            s = _dot(w_ref[h], vb[:, cols]) + b_ref[:, h:h + 1]
            o_ref[rows, cols] = (jax.nn.gelu(u_ref[rows, cols]) * s).astype(o_ref.dtype)


def _sgu(proj, gain, w, b, *, nblk, emit_av):
    m = proj.shape[0]
    tm = nblk * A_BLOCK
    n_out = 2 if emit_av else 1
    return pl.pallas_call(
        functools.partial(_sgu_kernel, nblk=nblk),
        out_shape=[jax.ShapeDtypeStruct((m, A_WIDTH), BF16), jax.ShapeDtypeStruct((m, A_WIDTH), F32)][:n_out],
        grid=(m // tm,),
        in_specs=[pl.BlockSpec((tm, A_WIDTH), lambda i: (i, 0)),
                  pl.BlockSpec((tm, A_WIDTH), lambda i: (i, 1)),
                  pl.BlockSpec((1, A_WIDTH), lambda i: (0, 0)),
                  pl.BlockSpec(w.shape, lambda i: (0, 0, 0)),
                  pl.BlockSpec(b.shape, lambda i: (0, 0))],
        out_specs=[pl.BlockSpec((tm, A_WIDTH), lambda i: (i, 0)),
                   pl.BlockSpec((tm, A_WIDTH), lambda i: (i, 0))][:n_out],
        compiler_params=_params(("parallel",)),
        name="sgu",
    )(proj, proj, gain, w, b)


def _gla_kernel(q_ref, k_ref, v_ref, r_ref, lg_ref, s0_ref, og_ref, o_ref, sout_ref, st_ref, *, clen, ngrp, chain):
    c = pl.program_id(1)
    rows_all = ngrp * clen
    shift = int(math.log2(clen))

    if chain:
        @pl.when(c == 0)
        def _():
            for h in range(B_HEADS):
                st_ref[h] = s0_ref[0, h].T

    row_i = lax.broadcasted_iota(jnp.int32, (rows_all, rows_all), 0)
    col_i = lax.broadcasted_iota(jnp.int32, (rows_all, rows_all), 1)
    same_group = lax.shift_right_logical(row_i, shift) == lax.shift_right_logical(col_i, shift)
    tri = ((row_i >= col_i) & same_group).astype(F32)
    cum = jnp.dot(tri, lg_ref[...], precision=lax.Precision.HIGHEST,
                  preferred_element_type=F32)
    tots = [cum[(g + 1) * clen - 1:(g + 1) * clen, :] for g in range(ngrp)]
    tot_rows = jnp.concatenate([jnp.broadcast_to(t, (clen, B_KEY_WIDTH)) for t in tots], axis=0)
    kd = (k_ref[...] * jnp.exp(tot_rows - cum)).astype(BF16)
    qs = (q_ref[...] * (B_KEY_DIM ** -0.5)).astype(BF16)
    vb = v_ref[...].astype(BF16)
    sr = jax.nn.silu(r_ref[...])
    grp = lax.shift_right_logical(lax.broadcasted_iota(jnp.int32, (rows_all, 1), 0), shift)
    zero = jnp.zeros((), BF16)

    def by_group(x):
        return jnp.concatenate([jnp.where(grp == g, x, zero) for g in range(ngrp)], axis=1)

    for h in range(B_HEADS):
        kc = slice(h * B_KEY_DIM, (h + 1) * B_KEY_DIM)
        vc = slice(h * B_VAL_DIM, (h + 1) * B_VAL_DIM)
        upd = lax.dot_general(vb[:, vc], by_group(kd[:, kc]), (((0,), (0,)), ((), ())),
                              preferred_element_type=F32)
        states = []
        st = st_ref[h] if chain else None
        for g in range(ngrp):
            if not chain:
                st = s0_ref[g, h].T
            st = jnp.exp(tots[g][:, kc]) * st + upd[:, g * B_KEY_DIM:(g + 1) * B_KEY_DIM]
            states.append(st.astype(BF16))
            if not chain:
                sout_ref[g, h] = st.T
        if chain:
            st_ref[h] = st
        o = lax.dot_general(by_group(qs[:, kc]), jnp.concatenate(states, axis=1), (((1,), (1,)), ((), ())),
                            preferred_element_type=F32)
        o = o * lax.rsqrt(jnp.mean(o * o, axis=-1, keepdims=True) + EPS)
        o = o * og_ref[:, vc] * sr[:, vc]
        o_ref[:, vc] = o.astype(o_ref.dtype)

    if chain:
        @pl.when(c == pl.num_programs(1) - 1)
        def _():
            for h in range(B_HEADS):
                sout_ref[0, h] = st_ref[h].T


def _gla(proj, lg, s0, og, *, nseq, t, clen, ngrp, chain):
    rows = clen * ngrp
    m = nseq * t
    if chain:
        steps = t // rows
        grid = (nseq, steps)
        rmap = lambda b, c: b * steps + c
        nstate = 1
    else:
        grid = (m // rows, 1)
        rmap = lambda b, c: b
        nstate = ngrp
    blk = lambda width, col: pl.BlockSpec((rows, width), lambda b, c: (rmap(b, c), col))
    state_spec = pl.BlockSpec((nstate, B_HEADS, B_KEY_DIM, B_VAL_DIM), lambda b, c: (b, 0, 0, 0))
    return pl.pallas_call(
        functools.partial(_gla_kernel, clen=clen, ngrp=ngrp, chain=chain),
        out_shape=[jax.ShapeDtypeStruct((m, B_WIDTH), BF16),
                   jax.ShapeDtypeStruct((nseq, B_HEADS, B_KEY_DIM, B_VAL_DIM), F32)],
        grid=grid,
        in_specs=[blk(B_KEY_WIDTH, 4),
                  blk(B_KEY_WIDTH, 5),
                  blk(B_WIDTH, 3),
                  blk(B_WIDTH, 4),
                  blk(B_KEY_WIDTH, 0),
                  state_spec,
                  pl.BlockSpec((1, B_WIDTH), lambda b, c: (0, 0))],
        out_specs=[blk(B_WIDTH, 0), state_spec],
        scratch_shapes=[pltpu.VMEM((B_HEADS, B_VAL_DIM, B_KEY_DIM), F32)],
        compiler_params=_params(("parallel", "arbitrary")),
        name="gla",
    )(proj, proj, proj, proj, lg, s0, og)


def _outproj_kernel(a_ref, b_ref, w_ref, x_ref, g_ref, o_ref, *w16_ref, convert):
    wb_ref = w_ref
    if convert:
        wb_ref, = w16_ref

        @pl.when(pl.program_id(0) == 0)
        def _():
            wb_ref[...] = w_ref[...].astype(BF16)

    ka = a_ref.shape[1]
    y = _dot(a_ref[...], wb_ref[0:ka, :]) + _dot(b_ref[...], wb_ref[ka:, :])
    o_ref[...] = x_ref[...] + _rms(y, g_ref[...])


def _outproj(a, b, w, x, g, *, tm, convert=False):
    m, d = x.shape
    k = a.shape[1] + b.shape[1]
    once = pl.Buffered(1)
    out_shape = [jax.ShapeDtypeStruct((m, d), F32)]
    out_specs = [pl.BlockSpec((tm, d), lambda i: (i, 0))]
    if convert:
        w_spec = pl.BlockSpec((None, k, d), lambda i: (0, 0, 0), pipeline_mode=once)
        out_shape.append(jax.ShapeDtypeStruct((k, d), BF16))
        out_specs.append(pl.BlockSpec((k, d), lambda i: (0, 0), pipeline_mode=once))
    else:
        w_spec = pl.BlockSpec((k, d), lambda i: (0, 0), pipeline_mode=once)
    res = pl.pallas_call(
        functools.partial(_outproj_kernel, convert=convert),
        out_shape=out_shape,
        grid=(m // tm,),
        in_specs=[pl.BlockSpec((tm, a.shape[1]), lambda i: (i, 0)),
                  pl.BlockSpec((tm, b.shape[1]), lambda i: (i, 0)),
                  w_spec,
                  pl.BlockSpec((tm, d), lambda i: (i, 0)),
                  pl.BlockSpec((1, d), lambda i: (0, 0))],
        out_specs=out_specs,
        compiler_params=_params(("arbitrary",)),
        name="outproj_convert" if convert else "outproj",
    )(a, b, w, x, g)
    return res if convert else res[0]


def _ffn_kernel(x_ref, gpre_ref, wg_ref, wv_ref, cw_ref, cb_ref, wd_ref, gpost_ref, st_ref,
                o_ref, tail_ref, *rest, step, hist, tps, tm, nf, convert):
    if convert:
        wg_out, wv_out, wd_out, h_ref, gext_ref, carry_ref, act_a, act_b = rest
    else:
        h_ref, gext_ref, carry_ref, act_a, act_b = rest
    i = pl.program_id(0)
    j = pl.program_id(1)
    first = (i % tps) == 0

    def up_and_gate(act_ref):
        hb = h_ref[...]
        wg, wv = wg_ref[...], wv_ref[...]
        if convert:
            wg, wv = wg.astype(BF16), wv.astype(BF16)
            wg_out[...] = wg
            wv_out[...] = wv
        gate = _dot(hb, wg)
        val = _dot(hb, wv)
        gext_ref[0:hist, :] = jnp.where(first, st_ref[0], carry_ref[j])
        gext_ref[hist:hist + tm, :] = gate
        prev2 = gext_ref[hist - 2 * step:hist - 2 * step + tm, :]
        prev1 = gext_ref[hist - step:hist - step + tm, :]
        conv = cb_ref[...] + cw_ref[0:1, :] * prev2 + cw_ref[1:2, :] * prev1 + cw_ref[2:3, :] * gate
        act_ref[...] = (jax.nn.gelu(conv) * val).astype(BF16)
        tail = gate[tm - hist:, :]
        carry_ref[j] = tail
        tail_ref[0] = tail

    def down(act_ref):
        wd = wd_ref[...]
        if convert:
            wd = wd.astype(BF16)
            wd_out[...] = wd
        o_ref[...] += _dot(act_ref[...], wd)

    @pl.when(j == 0)
    def _():
        h_ref[...] = _rms(x_ref[...], gpre_ref[...]).astype(BF16)
        o_ref[...] = jnp.zeros_like(o_ref)

        @pl.when(i == 0)
        def _():
            carry_ref[...] = jnp.zeros_like(carry_ref)

        up_and_gate(act_a)

    for parity, (src, dst) in enumerate(((act_b, act_a), (act_a, act_b))):
        @pl.when((j > 0) & (j < nf) & (j % 2 == parity))
        def _(src=src, dst=dst):
            down(src)
            up_and_gate(dst)

    @pl.when(j == nf)
    def _():
        down(act_a if (nf - 1) % 2 == 0 else act_b)
        o_ref[...] = x_ref[...] + _rms(o_ref[...], gpost_ref[...])


def _ffn(x, gpre, weights, cw, cb, gpost, state, *, layer, step, hist, tps, tm, tf, convert):
    m, d = x.shape
    nf = D_FF // tf
    nm = m // tm
    up = lambda j: jnp.minimum(j, nf - 1)
    down = lambda j: jnp.maximum(j - 1, 0)
    if convert:
        w_up, w_down = weights
        w_args = (w_up, w_up, w_down)
        w_specs = [pl.BlockSpec((None, d, tf), lambda i, j: (layer, 0, up(j))),
                   pl.BlockSpec((None, d, tf), lambda i, j: (layer, 0, nf + up(j))),
                   pl.BlockSpec((None, tf, d), lambda i, j: (layer, down(j), 0))]
        extra_shapes = [jax.ShapeDtypeStruct((d, D_FF), BF16), jax.ShapeDtypeStruct((d, D_FF), BF16),
                        jax.ShapeDtypeStruct((D_FF, d), BF16)]
        extra_specs = [pl.BlockSpec((d, tf), lambda i, j: (0, up(j))),
                       pl.BlockSpec((d, tf), lambda i, j: (0, up(j))),
                       pl.BlockSpec((tf, d), lambda i, j: (down(j), 0))]
    else:
        w_args = weights
        w_specs = [pl.BlockSpec((d, tf), lambda i, j: (0, up(j))),
                   pl.BlockSpec((d, tf), lambda i, j: (0, up(j))),
                   pl.BlockSpec((tf, d), lambda i, j: (down(j), 0))]
        extra_shapes, extra_specs = [], []
    return pl.pallas_call(
        functools.partial(_ffn_kernel, step=step, hist=hist, tps=tps, tm=tm, nf=nf, convert=convert),
        out_shape=[jax.ShapeDtypeStruct((m, d), F32), jax.ShapeDtypeStruct((nm, hist, D_FF), F32)] + extra_shapes,
        grid=(nm, nf + 1),
        in_specs=[pl.BlockSpec((tm, d), lambda i, j: (i, 0)),
                  pl.BlockSpec((None, 1, d), lambda i, j: (layer, 0, 0)),
                  w_specs[0], w_specs[1],
                  pl.BlockSpec((None, 3, tf), lambda i, j: (layer, 0, up(j))),
                  pl.BlockSpec((None, 1, tf), lambda i, j: (layer, 0, up(j))),
                  w_specs[2],
                  pl.BlockSpec((None, 1, d), lambda i, j: (layer, 0, 0)),
                  pl.BlockSpec((1, hist, tf), lambda i, j: (i // tps, 0, up(j)))],
        out_specs=[pl.BlockSpec((tm, d), lambda i, j: (i, 0), pipeline_mode=pl.Buffered(1)),
                   pl.BlockSpec((1, hist, tf), lambda i, j: (i, 0, up(j)))] + extra_specs,
        scratch_shapes=[pltpu.VMEM((tm, d), BF16),
                        pltpu.VMEM((hist + tm, tf), F32), pltpu.VMEM((nf, hist, tf), F32),
                        pltpu.VMEM((tm, tf), BF16), pltpu.VMEM((tm, tf), BF16)],
        compiler_params=_params(("arbitrary", "arbitrary")),
        name="ffn_convert" if convert else "ffn",
    )(x, gpre, w_args[0], w_args[1], cw, cb, w_args[2], gpost, state)


def _pool_kernel(c_ref, st_ref, cmap_ref, cs_ref, o_ref, tail_ref, ext_ref, carry_ref,
                 *, step, hist, tps, tm, pos0):
    i = pl.program_id(0)
    first = (i % tps) == 0

    @pl.when(first)
    def _():
        ext_ref[0:hist, :] = st_ref[0]

    @pl.when(jnp.logical_not(first))
    def _():
        ext_ref[0:hist, :] = carry_ref[...]

    ext_ref[hist:hist + tm, :] = c_ref[...]
    row = lax.broadcasted_iota(jnp.int32, (tm, 1), 0)
    if step > 1:
        row = lax.shift_right_logical(row, int(math.log2(step)))
    pos = pos0 + (i % tps) * (tm // step) + row
    for g, win in enumerate(POOL_WINDOWS):
        cols = slice(g * C_GROUP_DIM, (g + 1) * C_GROUP_DIM)
        cur = ext_ref[hist:hist + tm, cols]
        tot = cur
        for k in range(1, win):
            tot = tot + ext_ref[hist - k * step:hist - k * step + tm, cols]
        cnt = jnp.minimum(pos + 1, win).astype(F32)
        delta = tot / cnt - cur
        y = _dot(delta.astype(BF16), cmap_ref[g]) * cs_ref[:, cols]
        o_ref[:, cols] = y.astype(o_ref.dtype)
    tail = ext_ref[tm:tm + hist, :]
    carry_ref[...] = tail
    tail_ref[0] = tail


def _pool(proj, state, cmap, cscale, *, step, hist, tps, tm, pos0):
    m = proj.shape[0]
    nm = m // tm
    return pl.pallas_call(
        functools.partial(_pool_kernel, step=step, hist=hist, tps=tps, tm=tm, pos0=pos0),
        out_shape=[jax.ShapeDtypeStruct((m, C_WIDTH), BF16), jax.ShapeDtypeStruct((nm, hist, C_WIDTH), F32)],
        grid=(nm,),
        in_specs=[pl.BlockSpec((tm, C_WIDTH), lambda i: (i, 0)),
                  pl.BlockSpec((1, hist, C_WIDTH), lambda i: (i // tps, 0, 0)),
                  pl.BlockSpec(cmap.shape, lambda i: (0, 0, 0)),
                  pl.BlockSpec((1, C_WIDTH), lambda i: (0, 0))],
        out_specs=[pl.BlockSpec((tm, C_WIDTH), lambda i: (i, 0)),
                   pl.BlockSpec((1, hist, C_WIDTH), lambda i: (i, 0, 0))],
        scratch_shapes=[pltpu.VMEM((hist + tm, C_WIDTH), F32), pltpu.VMEM((hist, C_WIDTH), F32)],
        compiler_params=_params(("arbitrary",)),
        name="pool",
    )(proj, state, cmap, cscale)


S5_CHUNK = 512
S5_NCHUNK = S5_CH // S5_CHUNK
S5_FOLD = SUBLANES // 2


def _s5_kernel(u_ref, s0re_ref, s0im_ref, bblk_ref, cblk_ref, dskip_ref, wglu_ref, kc_ref,
               o_ref, tre_ref, tim_ref, sre_ref, sim_ref, cre_ref, cim_ref, *, step, tps, tm):
    i = pl.program_id(0)
    crow = cre_ref.shape[0]

    @pl.when((i % tps) == 0)
    def _():
        cre_ref[...] = jnp.broadcast_to(s0re_ref[0], cre_ref.shape) if step == 1 else s0re_ref[0]
        cim_ref[...] = jnp.broadcast_to(s0im_ref[0], cim_ref.shape) if step == 1 else s0im_ref[0]

    u = u_ref[...]
    lhs = [u.astype(BF16)]
    if step == 1:
        row_in_block = lax.broadcasted_iota(jnp.int32, (tm, 1), 0) & (SUBLANES - 1)
        for k in range(1, S5_FOLD):
            lhs.append(jnp.where(row_in_block >= k, pltpu.roll(u, k, 0), 0.0).astype(BF16))
    ys = []
    for m in range(S5_NCHUNK):
        cols = slice(m * S5_CHUNK, (m + 1) * S5_CHUNK)
        ucols = slice(m * LANES, (m + 1) * LANES)
        if step == 1:
            r = _dot(jnp.concatenate([x[:, ucols] for x in lhs], axis=1), bblk_ref[m])
        else:
            r = _dot(lhs[0][:, ucols], bblk_ref[m, 0:LANES, :])
        sre_ref[:, cols] = r[:, :S5_CHUNK]
        sim_ref[:, cols] = r[:, S5_CHUNK:]
        cr, ci = cre_ref[:, cols], cim_ref[:, cols]
        if step == 1:
            ar, ai, pwr, pwi = [kc_ref[k, :, cols] for k in range(4)]
            for rb in range(tm // SUBLANES):
                rows = slice(rb * SUBLANES, (rb + 1) * SUBLANES)
                xr = sre_ref[rows, cols]
                xi = sim_ref[rows, cols]
                sr = pltpu.roll(xr, S5_FOLD, 0)
                si = pltpu.roll(xi, S5_FOLD, 0)
                xr, xi = xr + (ar * sr - ai * si), xi + (ar * si + ai * sr)
                xr, xi = xr + (pwr * cr - pwi * ci), xi + (pwr * ci + pwi * cr)
                sre_ref[rows, cols] = xr
                sim_ref[rows, cols] = xi
                cr = jnp.broadcast_to(xr[SUBLANES - 1:SUBLANES, :], xr.shape)
                ci = jnp.broadcast_to(xi[SUBLANES - 1:SUBLANES, :], xi.shape)
        else:
            lr = jnp.broadcast_to(kc_ref[0, 0:1, cols], (crow, S5_CHUNK))
            li = jnp.broadcast_to(kc_ref[1, 0:1, cols], (crow, S5_CHUNK))
            for t in range(tm // step):
                rows = slice(t * step, (t + 1) * step)
                cr, ci = (sre_ref[rows, cols] + (lr * cr - li * ci),
                          sim_ref[rows, cols] + (lr * ci + li * cr))
                sre_ref[rows, cols] = cr
                sim_ref[rows, cols] = ci
        cre_ref[:, cols] = cr
        cim_ref[:, cols] = ci
        ys.append(_dot(sre_ref[:, cols].astype(BF16), cblk_ref[m, 0:S5_CHUNK, :])
                  + _dot(sim_ref[:, cols].astype(BF16), cblk_ref[m, S5_CHUNK:, :]))

    tre_ref[0] = cre_ref[...]
    tim_ref[0] = cim_ref[...]

    y = jnp.concatenate(ys, axis=1) + dskip_ref[...] * u
    z = _dot(jax.nn.gelu(y).astype(BF16), wglu_ref[...])
    o_ref[...] = (z[:, :D_WIDTH] * jax.nn.sigmoid(z[:, D_WIDTH:])).astype(o_ref.dtype)


def _s5(proj, s0re, s0im, bblk, cblk, dskip, wglu, kconst, *, step, tps, tm):
    m = proj.shape[0]
    nm = m // tm
    crow = s0re.shape[1] if step > 1 else SUBLANES
    srow = s0re.shape[1]
    return pl.pallas_call(
        functools.partial(_s5_kernel, step=step, tps=tps, tm=tm),
        out_shape=[jax.ShapeDtypeStruct((m, D_WIDTH), BF16),
                   jax.ShapeDtypeStruct((nm, crow, S5_CH), F32),
                   jax.ShapeDtypeStruct((nm, crow, S5_CH), F32)],
        grid=(nm,),
        in_specs=[pl.BlockSpec((tm, D_WIDTH), lambda i: (i, 1)),
                  pl.BlockSpec((1, srow, S5_CH), lambda i: (i // tps, 0, 0)),
                  pl.BlockSpec((1, srow, S5_CH), lambda i: (i // tps, 0, 0)),
                  pl.BlockSpec(bblk.shape, lambda i: (0, 0, 0), pipeline_mode=pl.Buffered(1)),
                  pl.BlockSpec(cblk.shape, lambda i: (0, 0, 0), pipeline_mode=pl.Buffered(1)),
                  pl.BlockSpec((1, D_WIDTH), lambda i: (0, 0)),
                  pl.BlockSpec(wglu.shape, lambda i: (0, 0), pipeline_mode=pl.Buffered(1)),
                  pl.BlockSpec(kconst.shape, lambda i: (0, 0, 0), pipeline_mode=pl.Buffered(1))],
        out_specs=[pl.BlockSpec((tm, D_WIDTH), lambda i: (i, 0)),
                   pl.BlockSpec((1, crow, S5_CH), lambda i: (i, 0, 0)),
                   pl.BlockSpec((1, crow, S5_CH), lambda i: (i, 0, 0))],
        scratch_shapes=[pltpu.VMEM((tm, S5_CH), F32), pltpu.VMEM((tm, S5_CH), F32),
                        pltpu.VMEM((crow, S5_CH), F32), pltpu.VMEM((crow, S5_CH), F32)],
        compiler_params=_params(("arbitrary",)),
        name="s5",
    )(proj, s0re, s0im, bblk, cblk, dskip, wglu, kconst)


def _block_diag(blocks, per):
    n, r, c = blocks.shape
    b = blocks.reshape(n // per, per, r, 1, c)
    on_diag = jnp.arange(per)[:, None, None, None] == jnp.arange(per)[None, None, :, None]
    return jnp.where(on_diag, b, jnp.zeros((), blocks.dtype)).reshape(n // per, per * r, per * c)


def _s5_constants(a_re, a_im, log_dt, b_re, b_im, c_re, c_im):
    dt = jnp.exp(log_dt)[:, None]
    zr, zi = a_re * dt, a_im * dt
    mag = jnp.exp(zr)
    lr, li = mag * jnp.cos(zi), mag * jnp.sin(zi)
    den = a_re * a_re + a_im * a_im
    nr, ni = lr - 1.0, li
    kr, ki = (nr * a_re + ni * a_im) / den, (ni * a_re - nr * a_im) / den
    bbr = kr[..., None] * b_re - ki[..., None] * b_im
    bbi = kr[..., None] * b_im + ki[..., None] * b_re
    per = S5_CHUNK // S5_STATE
    blocks = []
    qr, qi = jnp.ones_like(lr), jnp.zeros_like(li)
    for _ in range(S5_FOLD):
        fr = qr[..., None] * bbr - qi[..., None] * bbi
        fi = qr[..., None] * bbi + qi[..., None] * bbr
        blocks.append(jnp.concatenate([_block_diag(jnp.swapaxes(fr, 1, 2).astype(BF16), per),
                                       _block_diag(jnp.swapaxes(fi, 1, 2).astype(BF16), per)], axis=2))
        qr, qi = qr * lr - qi * li, qr * li + qi * lr
    bblk = jnp.concatenate(blocks, axis=1)
    cblk = jnp.concatenate([_block_diag(jnp.swapaxes(c_re, 1, 2).astype(BF16), per),
                            _block_diag(jnp.swapaxes(-c_im, 1, 2).astype(BF16), per)], axis=1)
    lr, li = lr.reshape(1, S5_CH), li.reshape(1, S5_CH)
    pr, pi = [lr], [li]
    for _ in range(SUBLANES - 1):
        pr, pi = pr + [pr[-1] * lr - pi[-1] * li], pi + [pr[-1] * li + pi[-1] * lr]
    rowid = jnp.arange(SUBLANES)[:, None]

    def masked(p, d):
        return jnp.where(rowid >= d, jnp.broadcast_to(p[d - 1], (SUBLANES, S5_CH)), 0.0)

    k_prompt = jnp.stack([masked(pr, S5_FOLD), masked(pi, S5_FOLD),
                          jnp.concatenate(pr, axis=0), jnp.concatenate(pi, axis=0)])
    k_sample = jnp.stack([jnp.broadcast_to(lr, (SUBLANES, S5_CH)), jnp.broadcast_to(li, (SUBLANES, S5_CH))])
    return bblk, cblk, k_prompt, k_sample


def _time_major(a):
    a = jnp.swapaxes(a, 0, 1)
    return a.reshape((a.shape[0] * a.shape[1],) + a.shape[2:])


def kernel(x_prompt, x_sample, state_gla, state_pool, state_s5_re, state_s5_im, state_ffn_conv, norm_mix_pre, norm_mix_post, norm_ffn_pre, norm_ffn_post, w_in_even, a_w_s, a_b_s, a_v_norm, b_w_gate, b_gate_bias, b_out_norm, w_out_even, w_in_odd, c_map, c_scale, s5_a_re, s5_a_im, s5_log_dt, s5_b_re, s5_b_im, s5_c_re, s5_c_im, s5_d, s5_w_glu, w_out_odd, ffn_w_up, ffn_conv_w, ffn_conv_b, ffn_w_down):
    bp = x_prompt.shape[0]
    nb, ts = x_sample.shape[0], x_sample.shape[1]
    xp = x_prompt.reshape(bp * SEQ, D_MODEL)
    xs = x_sample.reshape(nb * ts, D_MODEL)

    row = lambda v: v.reshape(1, -1)
    n_main = 2 * A_WIDTH + 2 * B_KEY_WIDTH + 2 * B_WIDTH
    w_in0 = w_in_even[0].astype(BF16)
    w_lr = jnp.pad(w_in0[:, n_main:], ((0, 0), (0, LANES - B_GATE_RANK)))
    w_gate = jnp.pad(b_w_gate[0], ((0, LANES - B_GATE_RANK), (0, 0))).astype(BF16)
    gate = (w_lr, w_gate, row(b_gate_bias[0]))
    pos = jnp.arange(A_BLOCK)
    causal = (pos[None, :] // CHUNK) <= (pos[:, None] // CHUNK)
    ws_prompt = jnp.where(causal[None], a_w_s[0], 0.0).astype(BF16)
    per = A_BLOCK // ts
    ws_small = jnp.where(causal[None, :ts, :ts], a_w_s[0][:, :ts, :ts], 0.0)
    ws_sample = jnp.einsum('hij,ab->haibj', ws_small, jnp.eye(per, dtype=F32)).reshape(A_HEADS, A_BLOCK, A_BLOCK).astype(BF16)
    bs_prompt = a_b_s[0].T
    bs_sample = jnp.tile(a_b_s[0][:, :ts].T, (per, 1))
    cmap = c_map[0].astype(BF16)
    bblk, cblk, k_prompt, k_sample = _s5_constants(s5_a_re[0], s5_a_im[0], s5_log_dt[0], s5_b_re[0], s5_b_im[0],
                                                   s5_c_re[0], s5_c_im[0])
    wglu = s5_w_glu[0].astype(BF16)

    tm = 512
    tf = 512
    tps_p = SEQ // tm
    ffn_hist_p = SUBLANES
    pool_hist_p = 2 * SUBLANES
    step_s = nb
    tps_s = (nb * ts) // tm
    ffn_hist_s = 2 * step_s
    pool_hist_s = (POOL_BUF + 1) * step_s

    tm_ffn = 1024
    tps_ffn_p = SEQ // tm_ffn
    tps_ffn_s = (nb * ts) // tm_ffn

    ffn_w16 = {}

    def ffn_layer(x, layer, state, *, step, hist, tps, convert):
        weights = (ffn_w_up, ffn_w_down) if convert else ffn_w16[layer]
        res = _ffn(x, norm_ffn_pre[:, None], weights, ffn_conv_w, ffn_conv_b[:, None], norm_ffn_post[:, None],
                   state, layer=layer, step=step, hist=hist, tps=tps, tm=tm_ffn, tf=256 if convert else tf,
                   convert=convert)
        if convert:
            ffn_w16[layer] = tuple(res[2:])
        return res[0], res[1]

    proj, lg = _inproj(xs, row(norm_mix_pre[0]), w_in0, gate, n=n_main, tm=1024, tn=1024)
    a_out, a_v = _sgu(proj, row(a_v_norm[0]), ws_sample, bs_sample, nblk=2, emit_av=True)
    b_out, gla_s = _gla(proj, lg, state_gla[0], row(b_out_norm[0]), nseq=nb, t=ts, clen=ts, ngrp=8, chain=False)
    xs, w_out0 = _outproj(a_out, b_out, w_out_even, xs, row(norm_mix_post[0]), tm=tm, convert=True)
    xs = _time_major(xs.reshape(nb, ts, D_MODEL))
    ffn_state = lambda layer: _time_major(state_ffn_conv[layer])[None]
    xs, ffn0_s = ffn_layer(xs, 0, ffn_state(0), step=step_s, hist=ffn_hist_s, tps=tps_ffn_s, convert=True)
    proj, w_in1 = _inproj(xs, row(norm_mix_pre[1]), w_in_odd, n=D_MODEL, tm=1024, tn=512, convert=True)
    pool_state = jnp.pad(_time_major(state_pool[0]), ((step_s, 0), (0, 0)))[None]
    c_out, pool_tail_s = _pool(proj, pool_state, cmap, row(c_scale[0]),
                               step=step_s, hist=pool_hist_s, tps=tps_s, tm=tm, pos0=PAST_LEN)
    d_out, s5re_tail_s, s5im_tail_s = _s5(proj, state_s5_re[0].reshape(1, nb, S5_CH),
                                          state_s5_im[0].reshape(1, nb, S5_CH),
                                          bblk, cblk, row(s5_d[0]), wglu, k_sample, step=step_s, tps=tps_s, tm=tm)
    xs, w_out1 = _outproj(c_out, d_out, w_out_odd, xs, row(norm_mix_post[1]), tm=tm, convert=True)
    xs, ffn1_s = ffn_layer(xs, 1, ffn_state(1), step=step_s, hist=ffn_hist_s, tps=tps_ffn_s, convert=True)

    proj, lg = _inproj(xp, row(norm_mix_pre[0]), w_in0, gate, n=n_main, tm=1024, tn=1280)
    a_out = _sgu(proj, row(a_v_norm[0]), ws_prompt, bs_prompt, nblk=2, emit_av=False)[0]
    b_out, gla_p = _gla(proj, lg, jnp.zeros((bp, B_HEADS, B_KEY_DIM, B_VAL_DIM), F32), row(b_out_norm[0]),
                        nseq=bp, t=SEQ, clen=CHUNK, ngrp=4, chain=True)
    xp = _outproj(a_out, b_out, w_out0, xp, row(norm_mix_post[0]), tm=tm)
    xp, ffn0_p = ffn_layer(xp, 0, jnp.zeros((bp, ffn_hist_p, D_FF), F32), step=1, hist=ffn_hist_p, tps=tps_ffn_p,
                           convert=False)
    proj = _inproj(xp, row(norm_mix_pre[1]), w_in1, n=D_MODEL, tm=1024, tn=1024)
    c_out, pool_tail_p = _pool(proj, jnp.zeros((bp, pool_hist_p, C_WIDTH), F32), cmap, row(c_scale[0]),
                               step=1, hist=pool_hist_p, tps=tps_p, tm=tm, pos0=0)
    zero_state = jnp.zeros((bp, 1, S5_CH), F32)
    d_out, s5re_tail_p, s5im_tail_p = _s5(proj, zero_state, zero_state, bblk, cblk, row(s5_d[0]), wglu, k_prompt,
                                          step=1, tps=tps_p, tm=tm)
    xp = _outproj(c_out, d_out, w_out1, xp, row(norm_mix_post[1]), tm=tm)
    xp, ffn1_p = ffn_layer(xp, 1, jnp.zeros((bp, ffn_hist_p, D_FF), F32), step=1, hist=ffn_hist_p, tps=tps_ffn_p,
                           convert=False)

    last = slice(tps_p - 1, None, tps_p)
    y_prompt = xp.reshape(bp, SEQ, D_MODEL)
    gla_prompt = gla_p[None]
    pool_prompt = pool_tail_p[last, pool_hist_p - POOL_BUF:][None]
    s5_re_prompt = s5re_tail_p[last, 0].reshape(1, bp, S5_GROUPS, S5_STATE)
    s5_im_prompt = s5im_tail_p[last, 0].reshape(1, bp, S5_GROUPS, S5_STATE)
    last_ffn = slice(tps_ffn_p - 1, None, tps_ffn_p)
    ffn_prompt = jnp.stack([ffn0_p[last_ffn, ffn_hist_p - 2:], ffn1_p[last_ffn, ffn_hist_p - 2:]])

    def batch_major(a, nt):
        return jnp.swapaxes(a.reshape(nt, nb, a.shape[-1]), 0, 1)

    y_sample = batch_major(xs, ts)
    gla_sample = gla_s[None]
    av_sample = a_v.reshape(1, nb, ts, A_WIDTH)
    pool_sample = batch_major(pool_tail_s[-1, step_s:], POOL_BUF)[None]
    s5_re_sample = s5re_tail_s[-1].reshape(1, nb, S5_GROUPS, S5_STATE)
    s5_im_sample = s5im_tail_s[-1].reshape(1, nb, S5_GROUPS, S5_STATE)
    ffn_sample = jnp.stack([batch_major(ffn0_s[-1], 2), batch_major(ffn1_s[-1], 2)])

    return (y_prompt, y_sample, gla_prompt, gla_sample, av_sample, pool_prompt, pool_sample,
            s5_re_prompt, s5_im_prompt, s5_re_sample, s5_im_sample, ffn_prompt, ffn_sample)
```

```python
import functools
import math

import jax
import jax.numpy as jnp
from jax import lax
from jax.experimental import pallas as pl
from jax.experimental.pallas import tpu as pltpu

F32 = jnp.float32
BF16 = jnp.bfloat16

D_MODEL = 2048
SEQ = 4096
DEC_BATCH = 32
DEC_SEQ = 32
PAST_LEN = 4096
CHUNK = 64
A_WIDTH = 1024
A_HEADS = 8
A_BLOCK = 128
B_HEADS = 4
B_KEY_DIM = 128
B_KEY_WIDTH = 512
B_VAL_DIM = 256
B_WIDTH = 1024
B_GATE_RANK = 16
B_GATE_TAU = 16.0
C_WIDTH = 1024
C_GROUP_DIM = 256
POOL_WINDOWS = (2, 4, 8, 16)
POOL_BUF = 15
D_WIDTH = 1024
S5_GROUPS = 64
S5_GROUP_DIM = 16
S5_STATE = 64
S5_CH = S5_GROUPS * S5_STATE
D_FF = 5632
EPS = 1e-6

LANES = 128
SUBLANES = 8
VMEM_LIMIT = 56 * 1024 * 1024


def _params(sem):
    return pltpu.CompilerParams(dimension_semantics=sem, vmem_limit_bytes=VMEM_LIMIT)


def _rms(x, g):
    return x * lax.rsqrt(jnp.mean(x * x, axis=-1, keepdims=True) + EPS) * g


def _dot(a, b):
    return jnp.dot(a, b, preferred_element_type=F32)


def _inproj_kernel(x_ref, g_ref, w_ref, *rest, with_gate, convert):
    rest = list(rest)
    if with_gate:
        wlr_ref, wgate_ref, gbias_ref = rest[:3]
        rest = rest[3:]
    o_ref = rest.pop(0)
    if with_gate:
        lg_ref = rest.pop(0)
    if convert:
        w16_ref = rest.pop(0)
    h_ref, = rest
    j = pl.program_id(1)

    @pl.when(j == 0)
    def _():
        hb = _rms(x_ref[...], g_ref[...]).astype(BF16)
        h_ref[...] = hb
        if with_gate:
            glr = _dot(hb, wlr_ref[...])
            z = _dot(glr.astype(BF16), wgate_ref[...]) + gbias_ref[...]
            lg_ref[...] = (jnp.minimum(z, 0.0) - jnp.log(1.0 + jnp.exp(-jnp.abs(z)))) * (1.0 / B_GATE_TAU)

    w = w_ref[...]
    if convert:
        w = w.astype(BF16)
        w16_ref[...] = w
    o_ref[...] = _dot(h_ref[...], w)


def _inproj(x, g, w, gate=None, *, n, tm, tn, convert=False):
    m, d = x.shape
    grid = (m // tm, n // tn)
    one_tile = m == tm
    in_specs = [pl.BlockSpec((tm, d), lambda i, j: (i, 0), pipeline_mode=pl.Buffered(1) if one_tile else None),
                pl.BlockSpec((1, d), lambda i, j: (0, 0)),
                pl.BlockSpec((None, d, tn), lambda i, j: (0, 0, j)) if convert
                else pl.BlockSpec((d, tn), lambda i, j: (0, j), pipeline_mode=pl.Buffered(1) if n == tn else None)]
    out_shape = [jax.ShapeDtypeStruct((m, n), F32)]
    out_specs = [pl.BlockSpec((tm, tn), lambda i, j: (i, j))]
    args = [x, g, w]
    if gate is not None:
        wlr, wgate, gbias = gate
        in_specs += [pl.BlockSpec(wlr.shape, lambda i, j: (0, 0)),
                     pl.BlockSpec(wgate.shape, lambda i, j: (0, 0)),
                     pl.BlockSpec(gbias.shape, lambda i, j: (0, 0))]
        out_shape.append(jax.ShapeDtypeStruct((m, B_KEY_WIDTH), F32))
        out_specs.append(pl.BlockSpec((tm, B_KEY_WIDTH), lambda i, j: (i, 0)))
        args += [wlr, wgate, gbias]
    if convert:
        out_shape.append(jax.ShapeDtypeStruct((d, n), BF16))
        out_specs.append(pl.BlockSpec((d, tn), lambda i, j: (0, j)))
    res = pl.pallas_call(
        functools.partial(_inproj_kernel, with_gate=gate is not None, convert=convert),
        out_shape=out_shape, grid=grid, in_specs=in_specs, out_specs=out_specs,
        scratch_shapes=[pltpu.VMEM((tm, d), BF16)],
        compiler_params=_params(("parallel", "arbitrary")),
        name=("inproj_gate" if gate is not None else "inproj") + ("_convert" if convert else ""),
    )(*args)
    return res if len(res) > 1 else res[0]


def _sgu_kernel(u_ref, v_ref, gain_ref, w_ref, b_ref, o_ref, *av_ref, nblk):
    for n in range(nblk):
        rows = slice(n * A_BLOCK, (n + 1) * A_BLOCK)
        v = jax.nn.gelu(v_ref[rows, :])
        mu = jnp.mean(v, axis=-1, keepdims=True)
        vc = v - mu
        vn = vc * lax.rsqrt(jnp.mean(vc * vc, axis=-1, keepdims=True) + EPS) * gain_ref[...]
        if av_ref:
            av_ref[0][rows, :] = vn
        vb = vn.astype(BF16)
        for h in range(A_HEADS):
            cols = slice(h * LANES, (h + 1) * LANES)
            s = _dot(w_ref[h], vb[:, cols]) + b_ref[:, h:h + 1]
            o_ref[rows, cols] = (jax.nn.gelu(u_ref[rows, cols]) * s).astype(o_ref.dtype)


def _sgu(proj, gain, w, b, *, nblk, emit_av):
    m = proj.shape[0]
    tm = nblk * A_BLOCK
    n_out = 2 if emit_av else 1
    return pl.pallas_call(
        functools.partial(_sgu_kernel, nblk=nblk),
        out_shape=[jax.ShapeDtypeStruct((m, A_WIDTH), BF16), jax.ShapeDtypeStruct((m, A_WIDTH), F32)][:n_out],
        grid=(m // tm,),
        in_specs=[pl.BlockSpec((tm, A_WIDTH), lambda i: (i, 0)),
                  pl.BlockSpec((tm, A_WIDTH), lambda i: (i, 1)),
                  pl.BlockSpec((1, A_WIDTH), lambda i: (0, 0)),
                  pl.BlockSpec(w.shape, lambda i: (0, 0, 0)),
                  pl.BlockSpec(b.shape, lambda i: (0, 0))],
        out_specs=[pl.BlockSpec((tm, A_WIDTH), lambda i: (i, 0)),
                   pl.BlockSpec((tm, A_WIDTH), lambda i: (i, 0))][:n_out],
        compiler_params=_params(("parallel",)),
        name="sgu",
    )(proj, proj, gain, w, b)


def _gla_kernel(q_ref, k_ref, v_ref, r_ref, lg_ref, s0_ref, og_ref, o_ref, sout_ref, st_ref, *, clen, ngrp, chain):
    c = pl.program_id(1)
    rows_all = ngrp * clen
    shift = int(math.log2(clen))

    if chain:
        @pl.when(c == 0)
        def _():
            for h in range(B_HEADS):
                st_ref[h] = s0_ref[0, h].T

    row_i = lax.broadcasted_iota(jnp.int32, (rows_all, rows_all), 0)
    col_i = lax.broadcasted_iota(jnp.int32, (rows_all, rows_all), 1)
    same_group = lax.shift_right_logical(row_i, shift) == lax.shift_right_logical(col_i, shift)
    tri = ((row_i >= col_i) & same_group).astype(F32)
    cum = jnp.dot(tri, lg_ref[...], precision=lax.Precision.HIGHEST,
                  preferred_element_type=F32)
    tots = [cum[(g + 1) * clen - 1:(g + 1) * clen, :] for g in range(ngrp)]
    tot_rows = jnp.concatenate([jnp.broadcast_to(t, (clen, B_KEY_WIDTH)) for t in tots], axis=0)
    kd = (k_ref[...] * jnp.exp(tot_rows - cum)).astype(BF16)
    qs = (q_ref[...] * (B_KEY_DIM ** -0.5)).astype(BF16)
    vb = v_ref[...].astype(BF16)
    sr = jax.nn.silu(r_ref[...])
    grp = lax.shift_right_logical(lax.broadcasted_iota(jnp.int32, (rows_all, 1), 0), shift)
    zero = jnp.zeros((), BF16)

    def by_group(x):
        return jnp.concatenate([jnp.where(grp == g, x, zero) for g in range(ngrp)], axis=1)

    for h in range(B_HEADS):
        kc = slice(h * B_KEY_DIM, (h + 1) * B_KEY_DIM)
        vc = slice(h * B_VAL_DIM, (h + 1) * B_VAL_DIM)
        upd = lax.dot_general(vb[:, vc], by_group(kd[:, kc]), (((0,), (0,)), ((), ())),
                              preferred_element_type=F32)
        states = []
        st = st_ref[h] if chain else None
        for g in range(ngrp):
            if not chain:
                st = s0_ref[g, h].T
            st = jnp.exp(tots[g][:, kc]) * st + upd[:, g * B_KEY_DIM:(g + 1) * B_KEY_DIM]
            states.append(st.astype(BF16))
            if not chain:
                sout_ref[g, h] = st.T
        if chain:
            st_ref[h] = st
        o = lax.dot_general(by_group(qs[:, kc]), jnp.concatenate(states, axis=1), (((1,), (1,)), ((), ())),
                            preferred_element_type=F32)
        o = o * lax.rsqrt(jnp.mean(o * o, axis=-1, keepdims=True) + EPS)
        o = o * og_ref[:, vc] * sr[:, vc]
        o_ref[:, vc] = o.astype(o_ref.dtype)

    if chain:
        @pl.when(c == pl.num_programs(1) - 1)
        def _():
            for h in range(B_HEADS):
                sout_ref[0, h] = st_ref[h].T


def _gla(proj, lg, s0, og, *, nseq, t, clen, ngrp, chain):
    rows = clen * ngrp
    m = nseq * t
    if chain:
        steps = t // rows
        grid = (nseq, steps)
        rmap = lambda b, c: b * steps + c
        nstate = 1
    else:
        grid = (m // rows, 1)
        rmap = lambda b, c: b
        nstate = ngrp
    blk = lambda width, col: pl.BlockSpec((rows, width), lambda b, c: (rmap(b, c), col))
    state_spec = pl.BlockSpec((nstate, B_HEADS, B_KEY_DIM, B_VAL_DIM), lambda b, c: (b, 0, 0, 0))
    return pl.pallas_call(
        functools.partial(_gla_kernel, clen=clen, ngrp=ngrp, chain=chain),
        out_shape=[jax.ShapeDtypeStruct((m, B_WIDTH), BF16),
                   jax.ShapeDtypeStruct((nseq, B_HEADS, B_KEY_DIM, B_VAL_DIM), F32)],
        grid=grid,
        in_specs=[blk(B_KEY_WIDTH, 4),
                  blk(B_KEY_WIDTH, 5),
                  blk(B_WIDTH, 3),
                  blk(B_WIDTH, 4),
                  blk(B_KEY_WIDTH, 0),
                  state_spec,
                  pl.BlockSpec((1, B_WIDTH), lambda b, c: (0, 0))],
        out_specs=[blk(B_WIDTH, 0), state_spec],
        scratch_shapes=[pltpu.VMEM((B_HEADS, B_VAL_DIM, B_KEY_DIM), F32)],
        compiler_params=_params(("parallel", "arbitrary")),
        name="gla",
    )(proj, proj, proj, proj, lg, s0, og)


def _outproj_kernel(a_ref, b_ref, w_ref, x_ref, g_ref, o_ref, *w16_ref, convert):
    wb_ref = w_ref
    if convert:
        wb_ref, = w16_ref

        @pl.when(pl.program_id(0) == 0)
        def _():
            wb_ref[...] = w_ref[...].astype(BF16)

    ka = a_ref.shape[1]
    y = _dot(a_ref[...], wb_ref[0:ka, :]) + _dot(b_ref[...], wb_ref[ka:, :])
    o_ref[...] = x_ref[...] + _rms(y, g_ref[...])


def _outproj(a, b, w, x, g, *, tm, convert=False):
    m, d = x.shape
    k = a.shape[1] + b.shape[1]
    once = pl.Buffered(1)
    out_shape = [jax.ShapeDtypeStruct((m, d), F32)]
    out_specs = [pl.BlockSpec((tm, d), lambda i: (i, 0))]
    if convert:
        w_spec = pl.BlockSpec((None, k, d), lambda i: (0, 0, 0), pipeline_mode=once)
        out_shape.append(jax.ShapeDtypeStruct((k, d), BF16))
        out_specs.append(pl.BlockSpec((k, d), lambda i: (0, 0), pipeline_mode=once))
    else:
        w_spec = pl.BlockSpec((k, d), lambda i: (0, 0), pipeline_mode=once)
    res = pl.pallas_call(
        functools.partial(_outproj_kernel, convert=convert),
        out_shape=out_shape,
        grid=(m // tm,),
        in_specs=[pl.BlockSpec((tm, a.shape[1]), lambda i: (i, 0)),
                  pl.BlockSpec((tm, b.shape[1]), lambda i: (i, 0)),
                  w_spec,
                  pl.BlockSpec((tm, d), lambda i: (i, 0)),
                  pl.BlockSpec((1, d), lambda i: (0, 0))],
        out_specs=out_specs,
        compiler_params=_params(("arbitrary",)),
        name="outproj_convert" if convert else "outproj",
    )(a, b, w, x, g)
    return res if convert else res[0]


def _ffn_kernel(x_ref, gpre_ref, wg_ref, wv_ref, cw_ref, cb_ref, wd_ref, gpost_ref, st_ref,
                o_ref, tail_ref, *rest, step, hist, tps, tm, nf, convert):
    if convert:
        wg_out, wv_out, wd_out, h_ref, gext_ref, carry_ref, act_a, act_b = rest
    else:
        h_ref, gext_ref, carry_ref, act_a, act_b = rest
    i = pl.program_id(0)
    j = pl.program_id(1)
    first = (i % tps) == 0

    def up_and_gate(act_ref):
        hb = h_ref[...]
        wg, wv = wg_ref[...], wv_ref[...]
        if convert:
            wg, wv = wg.astype(BF16), wv.astype(BF16)
            wg_out[...] = wg
            wv_out[...] = wv
        gate = _dot(hb, wg)
        val = _dot(hb, wv)
        gext_ref[0:hist, :] = jnp.where(first, st_ref[0], carry_ref[j])
        gext_ref[hist:hist + tm, :] = gate
        prev2 = gext_ref[hist - 2 * step:hist - 2 * step + tm, :]
        prev1 = gext_ref[hist - step:hist - step + tm, :]
        conv = cb_ref[...] + cw_ref[0:1, :] * prev2 + cw_ref[1:2, :] * prev1 + cw_ref[2:3, :] * gate
        act_ref[...] = (jax.nn.gelu(conv) * val).astype(BF16)
        tail = gate[tm - hist:, :]
        carry_ref[j] = tail
        tail_ref[0] = tail

    def down(act_ref):
        wd = wd_ref[...]
        if convert:
            wd = wd.astype(BF16)
            wd_out[...] = wd
        o_ref[...] += _dot(act_ref[...], wd)

    @pl.when(j == 0)
    def _():
        h_ref[...] = _rms(x_ref[...], gpre_ref[...]).astype(BF16)
        o_ref[...] = jnp.zeros_like(o_ref)

        @pl.when(i == 0)
        def _():
            carry_ref[...] = jnp.zeros_like(carry_ref)

        up_and_gate(act_a)

    for parity, (src, dst) in enumerate(((act_b, act_a), (act_a, act_b))):
        @pl.when((j > 0) & (j < nf) & (j % 2 == parity))
        def _(src=src, dst=dst):
            down(src)
            up_and_gate(dst)

    @pl.when(j == nf)
    def _():
        down(act_a if (nf - 1) % 2 == 0 else act_b)
        o_ref[...] = x_ref[...] + _rms(o_ref[...], gpost_ref[...])


def _ffn(x, gpre, weights, cw, cb, gpost, state, *, layer, step, hist, tps, tm, tf, convert):
    m, d = x.shape
    nf = D_FF // tf
    nm = m // tm
    up = lambda j: jnp.minimum(j, nf - 1)
    down = lambda j: jnp.maximum(j - 1, 0)
    if convert:
        w_up, w_down = weights
        w_args = (w_up, w_up, w_down)
        w_specs = [pl.BlockSpec((None, d, tf), lambda i, j: (layer, 0, up(j))),
                   pl.BlockSpec((None, d, tf), lambda i, j: (layer, 0, nf + up(j))),
                   pl.BlockSpec((None, tf, d), lambda i, j: (layer, down(j), 0))]
        extra_shapes = [jax.ShapeDtypeStruct((d, D_FF), BF16), jax.ShapeDtypeStruct((d, D_FF), BF16),
                        jax.ShapeDtypeStruct((D_FF, d), BF16)]
        extra_specs = [pl.BlockSpec((d, tf), lambda i, j: (0, up(j))),
                       pl.BlockSpec((d, tf), lambda i, j: (0, up(j))),
                       pl.BlockSpec((tf, d), lambda i, j: (down(j), 0))]
    else:
        w_args = weights
        w_specs = [pl.BlockSpec((d, tf), lambda i, j: (0, up(j))),
                   pl.BlockSpec((d, tf), lambda i, j: (0, up(j))),
                   pl.BlockSpec((tf, d), lambda i, j: (down(j), 0))]
        extra_shapes, extra_specs = [], []
    return pl.pallas_call(
        functools.partial(_ffn_kernel, step=step, hist=hist, tps=tps, tm=tm, nf=nf, convert=convert),
        out_shape=[jax.ShapeDtypeStruct((m, d), F32), jax.ShapeDtypeStruct((nm, hist, D_FF), F32)] + extra_shapes,
        grid=(nm, nf + 1),
        in_specs=[pl.BlockSpec((tm, d), lambda i, j: (i, 0)),
                  pl.BlockSpec((None, 1, d), lambda i, j: (layer, 0, 0)),
                  w_specs[0], w_specs[1],
                  pl.BlockSpec((None, 3, tf), lambda i, j: (layer, 0, up(j))),
                  pl.BlockSpec((None, 1, tf), lambda i, j: (layer, 0, up(j))),
                  w_specs[2],
                  pl.BlockSpec((None, 1, d), lambda i, j: (layer, 0, 0)),
                  pl.BlockSpec((1, hist, tf), lambda i, j: (i // tps, 0, up(j)))],
        out_specs=[pl.BlockSpec((tm, d), lambda i, j: (i, 0), pipeline_mode=pl.Buffered(1)),
                   pl.BlockSpec((1, hist, tf), lambda i, j: (i, 0, up(j)))] + extra_specs,
        scratch_shapes=[pltpu.VMEM((tm, d), BF16),
                        pltpu.VMEM((hist + tm, tf), F32), pltpu.VMEM((nf, hist, tf), F32),
                        pltpu.VMEM((tm, tf), BF16), pltpu.VMEM((tm, tf), BF16)],
        compiler_params=_params(("arbitrary", "arbitrary")),
        name="ffn_convert" if convert else "ffn",
    )(x, gpre, w_args[0], w_args[1], cw, cb, w_args[2], gpost, state)


def _pool_kernel(c_ref, st_ref, cmap_ref, cs_ref, o_ref, tail_ref, ext_ref, carry_ref,
                 *, step, hist, tps, tm, pos0):
    i = pl.program_id(0)
    first = (i % tps) == 0

    @pl.when(first)
    def _():
        ext_ref[0:hist, :] = st_ref[0]

    @pl.when(jnp.logical_not(first))
    def _():
        ext_ref[0:hist, :] = carry_ref[...]

    ext_ref[hist:hist + tm, :] = c_ref[...]
    row = lax.broadcasted_iota(jnp.int32, (tm, 1), 0)
    if step > 1:
        row = lax.shift_right_logical(row, int(math.log2(step)))
    pos = pos0 + (i % tps) * (tm // step) + row
    for g, win in enumerate(POOL_WINDOWS):
        cols = slice(g * C_GROUP_DIM, (g + 1) * C_GROUP_DIM)
        cur = ext_ref[hist:hist + tm, cols]
        tot = cur
        for k in range(1, win):
            tot = tot + ext_ref[hist - k * step:hist - k * step + tm, cols]
        cnt = jnp.minimum(pos + 1, win).astype(F32)
        delta = tot / cnt - cur
        y = _dot(delta.astype(BF16), cmap_ref[g]) * cs_ref[:, cols]
        o_ref[:, cols] = y.astype(o_ref.dtype)
    tail = ext_ref[tm:tm + hist, :]
    carry_ref[...] = tail
    tail_ref[0] = tail


def _pool(proj, state, cmap, cscale, *, step, hist, tps, tm, pos0):
    m = proj.shape[0]
    nm = m // tm
    return pl.pallas_call(
        functools.partial(_pool_kernel, step=step, hist=hist, tps=tps, tm=tm, pos0=pos0),
        out_shape=[jax.ShapeDtypeStruct((m, C_WIDTH), BF16), jax.ShapeDtypeStruct((nm, hist, C_WIDTH), F32)],
        grid=(nm,),
        in_specs=[pl.BlockSpec((tm, C_WIDTH), lambda i: (i, 0)),
                  pl.BlockSpec((1, hist, C_WIDTH), lambda i: (i // tps, 0, 0)),
                  pl.BlockSpec(cmap.shape, lambda i: (0, 0, 0)),
                  pl.BlockSpec((1, C_WIDTH), lambda i: (0, 0))],
        out_specs=[pl.BlockSpec((tm, C_WIDTH), lambda i: (i, 0)),
                   pl.BlockSpec((1, hist, C_WIDTH), lambda i: (i, 0, 0))],
        scratch_shapes=[pltpu.VMEM((hist + tm, C_WIDTH), F32), pltpu.VMEM((hist, C_WIDTH), F32)],
        compiler_params=_params(("arbitrary",)),
        name="pool",
    )(proj, state, cmap, cscale)


S5_CHUNK = 512
S5_NCHUNK = S5_CH // S5_CHUNK
S5_FOLD = SUBLANES // 2


def _s5_kernel(u_ref, s0re_ref, s0im_ref, bblk_ref, cblk_ref, dskip_ref, wglu_ref, kc_ref,
               o_ref, tre_ref, tim_ref, sre_ref, sim_ref, cre_ref, cim_ref, *, step, tps, tm):
    i = pl.program_id(0)
    crow = cre_ref.shape[0]

    @pl.when((i % tps) == 0)
    def _():
        cre_ref[...] = jnp.broadcast_to(s0re_ref[0], cre_ref.shape) if step == 1 else s0re_ref[0]
        cim_ref[...] = jnp.broadcast_to(s0im_ref[0], cim_ref.shape) if step == 1 else s0im_ref[0]

    u = u_ref[...]
    lhs = [u.astype(BF16)]
    if step == 1:
        row_in_block = lax.broadcasted_iota(jnp.int32, (tm, 1), 0) & (SUBLANES - 1)
        for k in range(1, S5_FOLD):
            lhs.append(jnp.where(row_in_block >= k, pltpu.roll(u, k, 0), 0.0).astype(BF16))
    ys = []
    for m in range(S5_NCHUNK):
        cols = slice(m * S5_CHUNK, (m + 1) * S5_CHUNK)
        ucols = slice(m * LANES, (m + 1) * LANES)
        if step == 1:
            r = _dot(jnp.concatenate([x[:, ucols] for x in lhs], axis=1), bblk_ref[m])
        else:
            r = _dot(lhs[0][:, ucols], bblk_ref[m, 0:LANES, :])
        sre_ref[:, cols] = r[:, :S5_CHUNK]
        sim_ref[:, cols] = r[:, S5_CHUNK:]
        cr, ci = cre_ref[:, cols], cim_ref[:, cols]
        if step == 1:
            ar, ai, pwr, pwi = [kc_ref[k, :, cols] for k in range(4)]
            for rb in range(tm // SUBLANES):
                rows = slice(rb * SUBLANES, (rb + 1) * SUBLANES)
                xr = sre_ref[rows, cols]
                xi = sim_ref[rows, cols]
                sr = pltpu.roll(xr, S5_FOLD, 0)
                si = pltpu.roll(xi, S5_FOLD, 0)
                xr, xi = xr + (ar * sr - ai * si), xi + (ar * si + ai * sr)
                xr, xi = xr + (pwr * cr - pwi * ci), xi + (pwr * ci + pwi * cr)
                sre_ref[rows, cols] = xr
                sim_ref[rows, cols] = xi
                cr = jnp.broadcast_to(xr[SUBLANES - 1:SUBLANES, :], xr.shape)
                ci = jnp.broadcast_to(xi[SUBLANES - 1:SUBLANES, :], xi.shape)
        else:
            lr = jnp.broadcast_to(kc_ref[0, 0:1, cols], (crow, S5_CHUNK))
            li = jnp.broadcast_to(kc_ref[1, 0:1, cols], (crow, S5_CHUNK))
            for t in range(tm // step):
                rows = slice(t * step, (t + 1) * step)
                cr, ci = (sre_ref[rows, cols] + (lr * cr - li * ci),
                          sim_ref[rows, cols] + (lr * ci + li * cr))
                sre_ref[rows, cols] = cr
                sim_ref[rows, cols] = ci
        cre_ref[:, cols] = cr
        cim_ref[:, cols] = ci
        ys.append(_dot(sre_ref[:, cols].astype(BF16), cblk_ref[m, 0:S5_CHUNK, :])
                  + _dot(sim_ref[:, cols].astype(BF16), cblk_ref[m, S5_CHUNK:, :]))

    tre_ref[0] = cre_ref[...]
    tim_ref[0] = cim_ref[...]

    y = jnp.concatenate(ys, axis=1) + dskip_ref[...] * u
    z = _dot(jax.nn.gelu(y).astype(BF16), wglu_ref[...])
    o_ref[...] = (z[:, :D_WIDTH] * jax.nn.sigmoid(z[:, D_WIDTH:])).astype(o_ref.dtype)


def _s5(proj, s0re, s0im, bblk, cblk, dskip, wglu, kconst, *, step, tps, tm):
    m = proj.shape[0]
    nm = m // tm
    crow = s0re.shape[1] if step > 1 else SUBLANES
    srow = s0re.shape[1]
    return pl.pallas_call(
        functools.partial(_s5_kernel, step=step, tps=tps, tm=tm),
        out_shape=[jax.ShapeDtypeStruct((m, D_WIDTH), BF16),
                   jax.ShapeDtypeStruct((nm, crow, S5_CH), F32),
                   jax.ShapeDtypeStruct((nm, crow, S5_CH), F32)],
        grid=(nm,),
        in_specs=[pl.BlockSpec((tm, D_WIDTH), lambda i: (i, 1)),
                  pl.BlockSpec((1, srow, S5_CH), lambda i: (i // tps, 0, 0)),
                  pl.BlockSpec((1, srow, S5_CH), lambda i: (i // tps, 0, 0)),
                  pl.BlockSpec(bblk.shape, lambda i: (0, 0, 0), pipeline_mode=pl.Buffered(1)),
                  pl.BlockSpec(cblk.shape, lambda i: (0, 0, 0), pipeline_mode=pl.Buffered(1)),
                  pl.BlockSpec((1, D_WIDTH), lambda i: (0, 0)),
                  pl.BlockSpec(wglu.shape, lambda i: (0, 0), pipeline_mode=pl.Buffered(1)),
                  pl.BlockSpec(kconst.shape, lambda i: (0, 0, 0), pipeline_mode=pl.Buffered(1))],
        out_specs=[pl.BlockSpec((tm, D_WIDTH), lambda i: (i, 0)),
                   pl.BlockSpec((1, crow, S5_CH), lambda i: (i, 0, 0)),
                   pl.BlockSpec((1, crow, S5_CH), lambda i: (i, 0, 0))],
        scratch_shapes=[pltpu.VMEM((tm, S5_CH), F32), pltpu.VMEM((tm, S5_CH), F32),
                        pltpu.VMEM((crow, S5_CH), F32), pltpu.VMEM((crow, S5_CH), F32)],
        compiler_params=_params(("arbitrary",)),
        name="s5",
    )(proj, s0re, s0im, bblk, cblk, dskip, wglu, kconst)


def _s5_constants(a_re, a_im, log_dt, b_re, b_im, c_re, c_im):
    per = S5_CHUNK // S5_STATE
    nchunk = S5_GROUPS // per
    on_diag = jnp.arange(per)[:, None] == jnp.arange(per)[None, :]
    zero = jnp.zeros((), BF16)
    dt = jnp.exp(log_dt)[:, None]
    zr, zi = a_re * dt, a_im * dt
    mag = jnp.exp(zr)
    lr, li = mag * jnp.cos(zi), mag * jnp.sin(zi)
    den = a_re * a_re + a_im * a_im
    nr, ni = lr - 1.0, li
    kr, ki = (nr * a_re + ni * a_im) / den, (ni * a_re - nr * a_im) / den
    bbr = kr[..., None] * b_re - ki[..., None] * b_im
    bbi = kr[..., None] * b_im + ki[..., None] * b_re
    qr, qi = [jnp.ones_like(lr)], [jnp.zeros_like(li)]
    for _ in range(S5_FOLD - 1):
        qr, qi = qr + [qr[-1] * lr - qi[-1] * li], qi + [qr[-1] * li + qi[-1] * lr]
    qr, qi = jnp.stack(qr)[..., None], jnp.stack(qi)[..., None]
    f = jnp.stack([qr * bbr - qi * bbi, qr * bbi + qi * bbr])
    f = f.reshape(2, S5_FOLD, nchunk, per, S5_STATE, S5_GROUP_DIM).transpose(2, 1, 3, 5, 0, 4).astype(BF16)
    bblk = jnp.where(on_diag[None, None, :, None, None, :, None], f[:, :, :, :, :, None, :], zero)
    bblk = bblk.reshape(nchunk, S5_FOLD * LANES, 2 * S5_CHUNK)
    c = jnp.stack([c_re, -c_im]).reshape(2, nchunk, per, S5_GROUP_DIM, S5_STATE).transpose(1, 0, 2, 4, 3).astype(BF16)
    cblk = jnp.where(on_diag[None, None, :, None, :, None], c[:, :, :, :, None, :], zero)
    cblk = cblk.reshape(nchunk, 2 * S5_CHUNK, LANES)
    lr, li = lr.reshape(1, S5_CH), li.reshape(1, S5_CH)
    pr, pi = [lr], [li]
    for _ in range(SUBLANES - 1):
        pr, pi = pr + [pr[-1] * lr - pi[-1] * li], pi + [pr[-1] * li + pi[-1] * lr]
    rowid = jnp.arange(SUBLANES)[:, None]

    def masked(p, d):
        return jnp.where(rowid >= d, jnp.broadcast_to(p[d - 1], (SUBLANES, S5_CH)), 0.0)

    k_prompt = jnp.stack([masked(pr, S5_FOLD), masked(pi, S5_FOLD),
                          jnp.concatenate(pr, axis=0), jnp.concatenate(pi, axis=0)])
    k_sample = jnp.stack([jnp.broadcast_to(lr, (SUBLANES, S5_CH)), jnp.broadcast_to(li, (SUBLANES, S5_CH))])
    return bblk, cblk, k_prompt, k_sample


def _time_major(a):
    a = jnp.swapaxes(a, 0, 1)
    return a.reshape((a.shape[0] * a.shape[1],) + a.shape[2:])


def kernel(x_prompt, x_sample, state_gla, state_pool, state_s5_re, state_s5_im, state_ffn_conv, norm_mix_pre, norm_mix_post, norm_ffn_pre, norm_ffn_post, w_in_even, a_w_s, a_b_s, a_v_norm, b_w_gate, b_gate_bias, b_out_norm, w_out_even, w_in_odd, c_map, c_scale, s5_a_re, s5_a_im, s5_log_dt, s5_b_re, s5_b_im, s5_c_re, s5_c_im, s5_d, s5_w_glu, w_out_odd, ffn_w_up, ffn_conv_w, ffn_conv_b, ffn_w_down):
    bp = x_prompt.shape[0]
    nb, ts = x_sample.shape[0], x_sample.shape[1]
    xp = x_prompt.reshape(bp * SEQ, D_MODEL)
    xs = x_sample.reshape(nb * ts, D_MODEL)

    row = lambda v: v.reshape(1, -1)
    n_main = 2 * A_WIDTH + 2 * B_KEY_WIDTH + 2 * B_WIDTH
    w_in0 = w_in_even[0].astype(BF16)
    w_lr = jnp.pad(w_in0[:, n_main:], ((0, 0), (0, LANES - B_GATE_RANK)))
    w_gate = jnp.pad(b_w_gate[0], ((0, LANES - B_GATE_RANK), (0, 0))).astype(BF16)
    gate = (w_lr, w_gate, row(b_gate_bias[0]))
    pos = jnp.arange(A_BLOCK)
    causal = (pos[None, :] // CHUNK) <= (pos[:, None] // CHUNK)
    ws_prompt = jnp.where(causal[None], a_w_s[0], 0.0).astype(BF16)
    per = A_BLOCK // ts
    ws_small = jnp.where(causal[None, :ts, :ts], a_w_s[0][:, :ts, :ts], 0.0)
    ws_sample = jnp.einsum('hij,ab->haibj', ws_small, jnp.eye(per, dtype=F32)).reshape(A_HEADS, A_BLOCK, A_BLOCK).astype(BF16)
    bs_prompt = a_b_s[0].T
    bs_sample = jnp.tile(a_b_s[0][:, :ts].T, (per, 1))
    cmap = c_map[0].astype(BF16)
    bblk, cblk, k_prompt, k_sample = _s5_constants(s5_a_re[0], s5_a_im[0], s5_log_dt[0], s5_b_re[0], s5_b_im[0],
                                                   s5_c_re[0], s5_c_im[0])
    wglu = s5_w_glu[0].astype(BF16)

    tm = 512
    tf = 512
    tps_p = SEQ // tm
    ffn_hist_p = SUBLANES
    pool_hist_p = 2 * SUBLANES
    step_s = nb
    tps_s = (nb * ts) // tm
    ffn_hist_s = 2 * step_s
    pool_hist_s = (POOL_BUF + 1) * step_s

    tm_ffn = 1024
    tps_ffn_p = SEQ // tm_ffn
    tps_ffn_s = (nb * ts) // tm_ffn

    ffn_w16 = {}

    def ffn_layer(x, layer, state, *, step, hist, tps, convert):
        weights = (ffn_w_up, ffn_w_down) if convert else ffn_w16[layer]
        res = _ffn(x, norm_ffn_pre[:, None], weights, ffn_conv_w, ffn_conv_b[:, None], norm_ffn_post[:, None],
                   state, layer=layer, step=step, hist=hist, tps=tps, tm=tm_ffn, tf=256 if convert else tf,
                   convert=convert)
        if convert:
            ffn_w16[layer] = tuple(res[2:])
        return res[0], res[1]

    proj, lg = _inproj(xs, row(norm_mix_pre[0]), w_in0, gate, n=n_main, tm=1024, tn=1024)
    a_out, a_v = _sgu(proj, row(a_v_norm[0]), ws_sample, bs_sample, nblk=2, emit_av=True)
    b_out, gla_s = _gla(proj, lg, state_gla[0], row(b_out_norm[0]), nseq=nb, t=ts, clen=ts, ngrp=8, chain=False)
    xs, w_out0 = _outproj(a_out, b_out, w_out_even, xs, row(norm_mix_post[0]), tm=tm, convert=True)
    xs = _time_major(xs.reshape(nb, ts, D_MODEL))
    ffn_state = lambda layer: _time_major(state_ffn_conv[layer])[None]
    xs, ffn0_s = ffn_layer(xs, 0, ffn_state(0), step=step_s, hist=ffn_hist_s, tps=tps_ffn_s, convert=True)
    proj, w_in1 = _inproj(xs, row(norm_mix_pre[1]), w_in_odd, n=D_MODEL, tm=1024, tn=512, convert=True)
    pool_state = jnp.pad(_time_major(state_pool[0]), ((step_s, 0), (0, 0)))[None]
    c_out, pool_tail_s = _pool(proj, pool_state, cmap, row(c_scale[0]),
                               step=step_s, hist=pool_hist_s, tps=tps_s, tm=tm, pos0=PAST_LEN)
    d_out, s5re_tail_s, s5im_tail_s = _s5(proj, state_s5_re[0].reshape(1, nb, S5_CH),
                                          state_s5_im[0].reshape(1, nb, S5_CH),
                                          bblk, cblk, row(s5_d[0]), wglu, k_sample, step=step_s, tps=tps_s, tm=tm)
    xs, w_out1 = _outproj(c_out, d_out, w_out_odd, xs, row(norm_mix_post[1]), tm=tm, convert=True)
    xs, ffn1_s = ffn_layer(xs, 1, ffn_state(1), step=step_s, hist=ffn_hist_s, tps=tps_ffn_s, convert=True)

    proj, lg = _inproj(xp, row(norm_mix_pre[0]), w_in0, gate, n=n_main, tm=1024, tn=1280)
    a_out = _sgu(proj, row(a_v_norm[0]), ws_prompt, bs_prompt, nblk=4, emit_av=False)[0]
    b_out, gla_p = _gla(proj, lg, jnp.zeros((bp, B_HEADS, B_KEY_DIM, B_VAL_DIM), F32), row(b_out_norm[0]),
                        nseq=bp, t=SEQ, clen=CHUNK, ngrp=4, chain=True)
    xp = _outproj(a_out, b_out, w_out0, xp, row(norm_mix_post[0]), tm=tm)
    xp, ffn0_p = ffn_layer(xp, 0, jnp.zeros((bp, ffn_hist_p, D_FF), F32), step=1, hist=ffn_hist_p,
                           tps=tps_ffn_p, convert=False)
    proj = _inproj(xp, row(norm_mix_pre[1]), w_in1, n=D_MODEL, tm=1024, tn=D_MODEL)
    c_out, pool_tail_p = _pool(proj, jnp.zeros((bp, pool_hist_p, C_WIDTH), F32), cmap, row(c_scale[0]),
                               step=1, hist=pool_hist_p, tps=tps_p, tm=tm, pos0=0)
    zero_state = jnp.zeros((bp, 1, S5_CH), F32)
    d_out, s5re_tail_p, s5im_tail_p = _s5(proj, zero_state, zero_state, bblk, cblk, row(s5_d[0]), wglu, k_prompt,
                                          step=1, tps=tps_p, tm=tm)
    xp = _outproj(c_out, d_out, w_out1, xp, row(norm_mix_post[1]), tm=tm)
    xp, ffn1_p = ffn_layer(xp, 1, jnp.zeros((bp, ffn_hist_p, D_FF), F32), step=1, hist=ffn_hist_p,
                           tps=tps_ffn_p, convert=False)

    last = slice(tps_p - 1, None, tps_p)
    y_prompt = xp.reshape(bp, SEQ, D_MODEL)
    gla_prompt = gla_p[None]
    pool_prompt = pool_tail_p[last, pool_hist_p - POOL_BUF:][None]
    s5_re_prompt = s5re_tail_p[last, 0].reshape(1, bp, S5_GROUPS, S5_STATE)
    s5_im_prompt = s5im_tail_p[last, 0].reshape(1, bp, S5_GROUPS, S5_STATE)
    last_ffn = slice(tps_ffn_p - 1, None, tps_ffn_p)
    ffn_prompt = jnp.stack([ffn0_p[last_ffn, ffn_hist_p - 2:], ffn1_p[last_ffn, ffn_hist_p - 2:]])

    def batch_major(a, nt):
        return jnp.swapaxes(a.reshape(nt, nb, a.shape[-1]), 0, 1)

    y_sample = batch_major(xs, ts)
    gla_sample = gla_s[None]
    av_sample = a_v.reshape(1, nb, ts, A_WIDTH)
    pool_sample = batch_major(pool_tail_s[-1, step_s:], POOL_BUF)[None]
    s5_re_sample = s5re_tail_s[-1].reshape(1, nb, S5_GROUPS, S5_STATE)
    s5_im_sample = s5im_tail_s[-1].reshape(1, nb, S5_GROUPS, S5_STATE)
    ffn_sample = jnp.stack([batch_major(ffn0_s[-1], 2), batch_major(ffn1_s[-1], 2)])

    return (y_prompt, y_sample, gla_prompt, gla_sample, av_sample, pool_prompt, pool_sample,
            s5_re_prompt, s5_im_prompt, s5_re_sample, s5_im_sample, ffn_prompt, ffn_sample)
```

```python
import functools
import math

import jax
import jax.numpy as jnp
from jax import lax
from jax.experimental import pallas as pl
from jax.experimental.pallas import tpu as pltpu

F32 = jnp.float32
BF16 = jnp.bfloat16

D_MODEL = 2048
SEQ = 4096
DEC_BATCH = 32
DEC_SEQ = 32
PAST_LEN = 4096
CHUNK = 64
A_WIDTH = 1024
A_HEADS = 8
A_BLOCK = 128
B_HEADS = 4
B_KEY_DIM = 128
B_KEY_WIDTH = 512
B_VAL_DIM = 256
B_WIDTH = 1024
B_GATE_RANK = 16
B_GATE_TAU = 16.0
C_WIDTH = 1024
C_GROUP_DIM = 256
POOL_WINDOWS = (2, 4, 8, 16)
POOL_BUF = 15
D_WIDTH = 1024
S5_GROUPS = 64
S5_GROUP_DIM = 16
S5_STATE = 64
S5_CH = S5_GROUPS * S5_STATE
D_FF = 5632
EPS = 1e-6

LANES = 128
SUBLANES = 8
VMEM_LIMIT = 56 * 1024 * 1024


def _params(sem):
    return pltpu.CompilerParams(dimension_semantics=sem, vmem_limit_bytes=VMEM_LIMIT)


def _rms(x, g):
    return x * lax.rsqrt(jnp.mean(x * x, axis=-1, keepdims=True) + EPS) * g


def _dot(a, b):
    return jnp.dot(a, b, preferred_element_type=F32)


def _inproj_kernel(x_ref, g_ref, w_ref, *rest, with_gate, convert):
    rest = list(rest)
    if with_gate:
        wlr_ref, wgate_ref, gbias_ref = rest[:3]
        rest = rest[3:]
    o_ref = rest.pop(0)
    if with_gate:
        lg_ref = rest.pop(0)
    if convert:
        w16_ref = rest.pop(0)
    h_ref, = rest
    j = pl.program_id(1)

    @pl.when(j == 0)
    def _():
        hb = _rms(x_ref[...], g_ref[...]).astype(BF16)
        h_ref[...] = hb
        if with_gate:
            glr = _dot(hb, wlr_ref[...])
            z = _dot(glr.astype(BF16), wgate_ref[...]) + gbias_ref[...]
            lg_ref[...] = (jnp.minimum(z, 0.0) - jnp.log(1.0 + jnp.exp(-jnp.abs(z)))) * (1.0 / B_GATE_TAU)

    w = w_ref[...]
    if convert:
        w = w.astype(BF16)
        w16_ref[...] = w
    o_ref[...] = _dot(h_ref[...], w)


def _inproj(x, g, w, gate=None, *, n, tm, tn, convert=False):
    m, d = x.shape
    grid = (m // tm, n // tn)
    one_tile = m == tm
    in_specs = [pl.BlockSpec((tm, d), lambda i, j: (i, 0), pipeline_mode=pl.Buffered(1) if one_tile else None),
                pl.BlockSpec((1, d), lambda i, j: (0, 0)),
                pl.BlockSpec((None, d, tn), lambda i, j: (0, 0, j)) if convert
                else pl.BlockSpec((d, tn), lambda i, j: (0, j), pipeline_mode=pl.Buffered(1) if n == tn else None)]
    out_shape = [jax.ShapeDtypeStruct((m, n), F32)]
    out_specs = [pl.BlockSpec((tm, tn), lambda i, j: (i, j))]
    args = [x, g, w]
    if gate is not None:
        wlr, wgate, gbias = gate
        in_specs += [pl.BlockSpec(wlr.shape, lambda i, j: (0, 0)),
                     pl.BlockSpec(wgate.shape, lambda i, j: (0, 0)),
                     pl.BlockSpec(gbias.shape, lambda i, j: (0, 0))]
        out_shape.append(jax.ShapeDtypeStruct((m, B_KEY_WIDTH), F32))
        out_specs.append(pl.BlockSpec((tm, B_KEY_WIDTH), lambda i, j: (i, 0)))
        args += [wlr, wgate, gbias]
    if convert:
        out_shape.append(jax.ShapeDtypeStruct((d, n), BF16))
        out_specs.append(pl.BlockSpec((d, tn), lambda i, j: (0, j)))
    res = pl.pallas_call(
        functools.partial(_inproj_kernel, with_gate=gate is not None, convert=convert),
        out_shape=out_shape, grid=grid, in_specs=in_specs, out_specs=out_specs,
        scratch_shapes=[pltpu.VMEM((tm, d), BF16)],
        compiler_params=_params(("parallel", "arbitrary")),
        name=("inproj_gate" if gate is not None else "inproj") + ("_convert" if convert else ""),
    )(*args)
    return res if len(res) > 1 else res[0]


def _sgu_kernel(u_ref, v_ref, gain_ref, w_ref, b_ref, o_ref, *av_ref, nblk):
    for n in range(nblk):
        rows = slice(n * A_BLOCK, (n + 1) * A_BLOCK)
        v = jax.nn.gelu(v_ref[rows, :])
        mu = jnp.mean(v, axis=-1, keepdims=True)
        vc = v - mu
        vn = vc * lax.rsqrt(jnp.mean(vc * vc, axis=-1, keepdims=True) + EPS) * gain_ref[...]
        if av_ref:
            av_ref[0][rows, :] = vn
        vb = vn.astype(BF16)
        for h in range(A_HEADS):
            cols = slice(h * LANES, (h + 1) * LANES)
            s = _dot(w_ref[h], vb[:, cols]) + b_ref[:, h:h + 1]
            o_ref[rows, cols] = (jax.nn.gelu(u_ref[rows, cols]) * s).astype(o_ref.dtype)


def _sgu(proj, gain, w, b, *, nblk, emit_av):
    m = proj.shape[0]
    tm = nblk * A_BLOCK
    n_out = 2 if emit_av else 1
    return pl.pallas_call(
        functools.partial(_sgu_kernel, nblk=nblk),
        out_shape=[jax.ShapeDtypeStruct((m, A_WIDTH), BF16), jax.ShapeDtypeStruct((m, A_WIDTH), F32)][:n_out],
        grid=(m // tm,),
        in_specs=[pl.BlockSpec((tm, A_WIDTH), lambda i: (i, 0)),
                  pl.BlockSpec((tm, A_WIDTH), lambda i: (i, 1)),
                  pl.BlockSpec((1, A_WIDTH), lambda i: (0, 0)),
                  pl.BlockSpec(w.shape, lambda i: (0, 0, 0)),
                  pl.BlockSpec(b.shape, lambda i: (0, 0))],
        out_specs=[pl.BlockSpec((tm, A_WIDTH), lambda i: (i, 0)),
                   pl.BlockSpec((tm, A_WIDTH), lambda i: (i, 0))][:n_out],
        compiler_params=_params(("parallel",)),
        name="sgu",
    )(proj, proj, gain, w, b)


def _gla_kernel(q_ref, k_ref, v_ref, r_ref, lg_ref, s0_ref, og_ref, o_ref, sout_ref, st_ref, *, clen, ngrp, chain):
    c = pl.program_id(1)
    rows_all = ngrp * clen
    shift = int(math.log2(clen))

    if chain:
        @pl.when(c == 0)
        def _():
            for h in range(B_HEADS):
                st_ref[h] = s0_ref[0, h].T

    row_i = lax.broadcasted_iota(jnp.int32, (rows_all, rows_all), 0)
    col_i = lax.broadcasted_iota(jnp.int32, (rows_all, rows_all), 1)
    same_group = lax.shift_right_logical(row_i, shift) == lax.shift_right_logical(col_i, shift)
    tri = ((row_i >= col_i) & same_group).astype(F32)
    cum = jnp.dot(tri, lg_ref[...], precision=lax.Precision.HIGHEST,
                  preferred_element_type=F32)
    tots = [cum[(g + 1) * clen - 1:(g + 1) * clen, :] for g in range(ngrp)]
    tot_rows = jnp.concatenate([jnp.broadcast_to(t, (clen, B_KEY_WIDTH)) for t in tots], axis=0)
    kd = (k_ref[...] * jnp.exp(tot_rows - cum)).astype(BF16)
    qs = (q_ref[...] * (B_KEY_DIM ** -0.5)).astype(BF16)
    vb = v_ref[...].astype(BF16)
    sr = jax.nn.silu(r_ref[...])
    grp = lax.shift_right_logical(lax.broadcasted_iota(jnp.int32, (rows_all, 1), 0), shift)
    zero = jnp.zeros((), BF16)

    def by_group(x):
        return jnp.concatenate([jnp.where(grp == g, x, zero) for g in range(ngrp)], axis=1)

    for h in range(B_HEADS):
        kc = slice(h * B_KEY_DIM, (h + 1) * B_KEY_DIM)
        vc = slice(h * B_VAL_DIM, (h + 1) * B_VAL_DIM)
        upd = lax.dot_general(vb[:, vc], by_group(kd[:, kc]), (((0,), (0,)), ((), ())),
                              preferred_element_type=F32)
        states = []
        st = st_ref[h] if chain else None
        for g in range(ngrp):
            if not chain:
                st = s0_ref[g, h].T
            st = jnp.exp(tots[g][:, kc]) * st + upd[:, g * B_KEY_DIM:(g + 1) * B_KEY_DIM]
            states.append(st.astype(BF16))
            if not chain:
                sout_ref[g, h] = st.T
        if chain:
            st_ref[h] = st
        o = lax.dot_general(by_group(qs[:, kc]), jnp.concatenate(states, axis=1), (((1,), (1,)), ((), ())),
                            preferred_element_type=F32)
        o = o * lax.rsqrt(jnp.mean(o * o, axis=-1, keepdims=True) + EPS)
        o = o * og_ref[:, vc] * sr[:, vc]
        o_ref[:, vc] = o.astype(o_ref.dtype)

    if chain:
        @pl.when(c == pl.num_programs(1) - 1)
        def _():
            for h in range(B_HEADS):
                sout_ref[0, h] = st_ref[h].T


def _gla(proj, lg, s0, og, *, nseq, t, clen, ngrp, chain):
    rows = clen * ngrp
    m = nseq * t
    if chain:
        steps = t // rows
        grid = (nseq, steps)
        rmap = lambda b, c: b * steps + c
        nstate = 1
    else:
        grid = (m // rows, 1)
        rmap = lambda b, c: b
        nstate = ngrp
    blk = lambda width, col: pl.BlockSpec((rows, width), lambda b, c: (rmap(b, c), col))
    state_spec = pl.BlockSpec((nstate, B_HEADS, B_KEY_DIM, B_VAL_DIM), lambda b, c: (b, 0, 0, 0))
    return pl.pallas_call(
        functools.partial(_gla_kernel, clen=clen, ngrp=ngrp, chain=chain),
        out_shape=[jax.ShapeDtypeStruct((m, B_WIDTH), BF16),
                   jax.ShapeDtypeStruct((nseq, B_HEADS, B_KEY_DIM, B_VAL_DIM), F32)],
        grid=grid,
        in_specs=[blk(B_KEY_WIDTH, 4),
                  blk(B_KEY_WIDTH, 5),
                  blk(B_WIDTH, 3),
                  blk(B_WIDTH, 4),
                  blk(B_KEY_WIDTH, 0),
                  state_spec,
                  pl.BlockSpec((1, B_WIDTH), lambda b, c: (0, 0))],
        out_specs=[blk(B_WIDTH, 0), state_spec],
        scratch_shapes=[pltpu.VMEM((B_HEADS, B_VAL_DIM, B_KEY_DIM), F32)],
        compiler_params=_params(("parallel", "arbitrary")),
        name="gla",
    )(proj, proj, proj, proj, lg, s0, og)


def _outproj_kernel(a_ref, b_ref, w_ref, x_ref, g_ref, o_ref, *w16_ref, convert):
    wb_ref = w_ref
    if convert:
        wb_ref, = w16_ref

        @pl.when(pl.program_id(0) == 0)
        def _():
            wb_ref[...] = w_ref[...].astype(BF16)

    ka = a_ref.shape[1]
    y = _dot(a_ref[...], wb_ref[0:ka, :]) + _dot(b_ref[...], wb_ref[ka:, :])
    o_ref[...] = x_ref[...] + _rms(y, g_ref[...])


def _outproj(a, b, w, x, g, *, tm, convert=False):
    m, d = x.shape
    k = a.shape[1] + b.shape[1]
    once = pl.Buffered(1)
    out_shape = [jax.ShapeDtypeStruct((m, d), F32)]
    out_specs = [pl.BlockSpec((tm, d), lambda i: (i, 0))]
    if convert:
        w_spec = pl.BlockSpec((None, k, d), lambda i: (0, 0, 0), pipeline_mode=once)
        out_shape.append(jax.ShapeDtypeStruct((k, d), BF16))
        out_specs.append(pl.BlockSpec((k, d), lambda i: (0, 0), pipeline_mode=once))
    else:
        w_spec = pl.BlockSpec((k, d), lambda i: (0, 0), pipeline_mode=once)
    res = pl.pallas_call(
        functools.partial(_outproj_kernel, convert=convert),
        out_shape=out_shape,
        grid=(m // tm,),
        in_specs=[pl.BlockSpec((tm, a.shape[1]), lambda i: (i, 0)),
                  pl.BlockSpec((tm, b.shape[1]), lambda i: (i, 0)),
                  w_spec,
                  pl.BlockSpec((tm, d), lambda i: (i, 0)),
                  pl.BlockSpec((1, d), lambda i: (0, 0))],
        out_specs=out_specs,
        compiler_params=_params(("arbitrary",)),
        name="outproj_convert" if convert else "outproj",
    )(a, b, w, x, g)
    return res if convert else res[0]


def _ffn_kernel(x_ref, gpre_ref, wg_ref, wv_ref, cw_ref, cb_ref, wd_ref, gpost_ref, st_ref,
                o_ref, tail_ref, *rest, step, hist, tps, tm, nf, convert):
    if convert:
        wg_out, wv_out, wd_out, h_ref, gext_ref, carry_ref, act_a, act_b = rest
    else:
        h_ref, gext_ref, carry_ref, act_a, act_b = rest
    i = pl.program_id(0)
    j = pl.program_id(1)
    first = (i % tps) == 0

    def up_and_gate(act_ref):
        hb = h_ref[...]
        wg, wv = wg_ref[...], wv_ref[...]
        if convert:
            wg, wv = wg.astype(BF16), wv.astype(BF16)
            wg_out[...] = wg
            wv_out[...] = wv
        gate = _dot(hb, wg)
        val = _dot(hb, wv)
        cw = cw_ref[j]
        gext_ref[0:hist, :] = jnp.where(first, st_ref[0, j], carry_ref[j])
        gext_ref[hist:hist + tm, :] = gate
        prev2 = gext_ref[hist - 2 * step:hist - 2 * step + tm, :]
        prev1 = gext_ref[hist - step:hist - step + tm, :]
        conv = cb_ref[j] + cw[0:1, :] * prev2 + cw[1:2, :] * prev1 + cw[2:3, :] * gate
        act_ref[...] = (jax.nn.gelu(conv) * val).astype(BF16)
        tail = gate[tm - hist:, :]
        carry_ref[j] = tail
        tail_ref[0, j] = tail

    def down(act_ref):
        wd = wd_ref[...]
        if convert:
            wd = wd.astype(BF16)
            wd_out[...] = wd
        o_ref[...] += _dot(act_ref[...], wd)

    @pl.when(j == 0)
    def _():
        h_ref[...] = _rms(x_ref[...], gpre_ref[...]).astype(BF16)
        o_ref[...] = jnp.zeros_like(o_ref)

        @pl.when(i == 0)
        def _():
            carry_ref[...] = jnp.zeros_like(carry_ref)

        up_and_gate(act_a)

    for parity, (src, dst) in enumerate(((act_b, act_a), (act_a, act_b))):
        @pl.when((j > 0) & (j < nf) & (j % 2 == parity))
        def _(src=src, dst=dst):
            down(src)
            up_and_gate(dst)

    @pl.when(j == nf)
    def _():
        down(act_a if (nf - 1) % 2 == 0 else act_b)
        o_ref[...] = x_ref[...] + _rms(o_ref[...], gpost_ref[...])


def _ffn(x, gpre, weights, cw, cb, gpost, state, *, layer, step, hist, tps, tm, tf, convert):
    m, d = x.shape
    nf = D_FF // tf
    nm = m // tm
    up = lambda j: jnp.minimum(j, nf - 1)
    down = lambda j: jnp.maximum(j - 1, 0)

    def by_tile(a):
        a = a.reshape(a.shape[:-1] + (nf, tf))
        return jnp.swapaxes(a, -2, -3)

    cw, cb, state = by_tile(cw), by_tile(cb), by_tile(state)
    if convert:
        w_up, w_down = weights
        w_args = (w_up, w_up, w_down)
        w_specs = [pl.BlockSpec((None, d, tf), lambda i, j: (layer, 0, up(j))),
                   pl.BlockSpec((None, d, tf), lambda i, j: (layer, 0, nf + up(j))),
                   pl.BlockSpec((None, tf, d), lambda i, j: (layer, down(j), 0))]
        extra_shapes = [jax.ShapeDtypeStruct((d, D_FF), BF16), jax.ShapeDtypeStruct((d, D_FF), BF16),
                        jax.ShapeDtypeStruct((D_FF, d), BF16)]
        extra_specs = [pl.BlockSpec((d, tf), lambda i, j: (0, up(j))),
                       pl.BlockSpec((d, tf), lambda i, j: (0, up(j))),
                       pl.BlockSpec((tf, d), lambda i, j: (down(j), 0))]
    else:
        w_args = weights
        w_specs = [pl.BlockSpec((d, tf), lambda i, j: (0, up(j))),
                   pl.BlockSpec((d, tf), lambda i, j: (0, up(j))),
                   pl.BlockSpec((tf, d), lambda i, j: (down(j), 0))]
        extra_shapes, extra_specs = [], []
    res = pl.pallas_call(
        functools.partial(_ffn_kernel, step=step, hist=hist, tps=tps, tm=tm, nf=nf, convert=convert),
        out_shape=[jax.ShapeDtypeStruct((m, d), F32), jax.ShapeDtypeStruct((nm, nf, hist, tf), F32)] + extra_shapes,
        grid=(nm, nf + 1),
        in_specs=[pl.BlockSpec((tm, d), lambda i, j: (i, 0)),
                  pl.BlockSpec((None, 1, d), lambda i, j: (layer, 0, 0)),
                  w_specs[0], w_specs[1],
                  pl.BlockSpec((None, nf, 3, tf), lambda i, j: (layer, 0, 0, 0)),
                  pl.BlockSpec((None, nf, 1, tf), lambda i, j: (layer, 0, 0, 0)),
                  w_specs[2],
                  pl.BlockSpec((None, 1, d), lambda i, j: (layer, 0, 0)),
                  pl.BlockSpec((1, nf, hist, tf), lambda i, j: (i // tps, 0, 0, 0))],
        out_specs=[pl.BlockSpec((tm, d), lambda i, j: (i, 0), pipeline_mode=pl.Buffered(1)),
                   pl.BlockSpec((1, nf, hist, tf), lambda i, j: (i, 0, 0, 0))] + extra_specs,
        scratch_shapes=[pltpu.VMEM((tm, d), BF16),
                        pltpu.VMEM((hist + tm, tf), F32), pltpu.VMEM((nf, hist, tf), F32),
                        pltpu.VMEM((tm, tf), BF16), pltpu.VMEM((tm, tf), BF16)],
        compiler_params=_params(("arbitrary", "arbitrary")),
        name="ffn_convert" if convert else "ffn",
    )(x, gpre, w_args[0], w_args[1], cw, cb, w_args[2], gpost, state)
    res = list(res)
    res[1] = jnp.swapaxes(res[1], 1, 2).reshape(nm, hist, D_FF)
    return res


def _pool_kernel(c_ref, st_ref, cmap_ref, cs_ref, o_ref, tail_ref, ext_ref, carry_ref,
                 *, step, hist, tps, tm, pos0):
    i = pl.program_id(0)
    first = (i % tps) == 0

    @pl.when(first)
    def _():
        ext_ref[0:hist, :] = st_ref[0]

    @pl.when(jnp.logical_not(first))
    def _():
        ext_ref[0:hist, :] = carry_ref[...]

    ext_ref[hist:hist + tm, :] = c_ref[...]
    row = lax.broadcasted_iota(jnp.int32, (tm, 1), 0)
    if step > 1:
        row = lax.shift_right_logical(row, int(math.log2(step)))
    pos = pos0 + (i % tps) * (tm // step) + row
    for g, win in enumerate(POOL_WINDOWS):
        cols = slice(g * C_GROUP_DIM, (g + 1) * C_GROUP_DIM)
        cur = ext_ref[hist:hist + tm, cols]
        tot = cur
        for k in range(1, win):
            tot = tot + ext_ref[hist - k * step:hist - k * step + tm, cols]
        cnt = jnp.minimum(pos + 1, win).astype(F32)
        delta = tot / cnt - cur
        y = _dot(delta.astype(BF16), cmap_ref[g]) * cs_ref[:, cols]
        o_ref[:, cols] = y.astype(o_ref.dtype)
    tail = ext_ref[tm:tm + hist, :]
    carry_ref[...] = tail
    tail_ref[0] = tail


def _pool(proj, state, cmap, cscale, *, step, hist, tps, tm, pos0):
    m = proj.shape[0]
    nm = m // tm
    return pl.pallas_call(
        functools.partial(_pool_kernel, step=step, hist=hist, tps=tps, tm=tm, pos0=pos0),
        out_shape=[jax.ShapeDtypeStruct((m, C_WIDTH), BF16), jax.ShapeDtypeStruct((nm, hist, C_WIDTH), F32)],
        grid=(nm,),
        in_specs=[pl.BlockSpec((tm, C_WIDTH), lambda i: (i, 0)),
                  pl.BlockSpec((1, hist, C_WIDTH), lambda i: (i // tps, 0, 0)),
                  pl.BlockSpec(cmap.shape, lambda i: (0, 0, 0)),
                  pl.BlockSpec((1, C_WIDTH), lambda i: (0, 0))],
        out_specs=[pl.BlockSpec((tm, C_WIDTH), lambda i: (i, 0)),
                   pl.BlockSpec((1, hist, C_WIDTH), lambda i: (i, 0, 0))],
        scratch_shapes=[pltpu.VMEM((hist + tm, C_WIDTH), F32), pltpu.VMEM((hist, C_WIDTH), F32)],
        compiler_params=_params(("arbitrary",)),
        name="pool",
    )(proj, state, cmap, cscale)


S5_CHUNK = 512
S5_NCHUNK = S5_CH // S5_CHUNK
S5_FOLD = SUBLANES // 2


def _s5_kernel(u_ref, s0re_ref, s0im_ref, bblk_ref, cblk_ref, dskip_ref, wglu_ref, kc_ref,
               o_ref, tre_ref, tim_ref, sre_ref, sim_ref, cre_ref, cim_ref, *, step, tps, tm):
    i = pl.program_id(0)
    crow = cre_ref.shape[0]

    @pl.when((i % tps) == 0)
    def _():
        cre_ref[...] = jnp.broadcast_to(s0re_ref[0], cre_ref.shape) if step == 1 else s0re_ref[0]
        cim_ref[...] = jnp.broadcast_to(s0im_ref[0], cim_ref.shape) if step == 1 else s0im_ref[0]

    u = u_ref[...]
    lhs = [u.astype(BF16)]
    if step == 1:
        row_in_block = lax.broadcasted_iota(jnp.int32, (tm, 1), 0) & (SUBLANES - 1)
        for k in range(1, S5_FOLD):
            lhs.append(jnp.where(row_in_block >= k, pltpu.roll(u, k, 0), 0.0).astype(BF16))
    ys = []
    for m in range(S5_NCHUNK):
        cols = slice(m * S5_CHUNK, (m + 1) * S5_CHUNK)
        ucols = slice(m * LANES, (m + 1) * LANES)
        if step == 1:
            r = _dot(jnp.concatenate([x[:, ucols] for x in lhs], axis=1), bblk_ref[m])
        else:
            r = _dot(lhs[0][:, ucols], bblk_ref[m, 0:LANES, :])
        sre_ref[:, cols] = r[:, :S5_CHUNK]
        sim_ref[:, cols] = r[:, S5_CHUNK:]
        cr, ci = cre_ref[:, cols], cim_ref[:, cols]
        if step == 1:
            ar, ai, pwr, pwi = [kc_ref[k, :, cols] for k in range(4)]
            for rb in range(tm // SUBLANES):
                rows = slice(rb * SUBLANES, (rb + 1) * SUBLANES)
                xr = sre_ref[rows, cols]
                xi = sim_ref[rows, cols]
                sr = pltpu.roll(xr, S5_FOLD, 0)
                si = pltpu.roll(xi, S5_FOLD, 0)
                xr, xi = xr + (ar * sr - ai * si), xi + (ar * si + ai * sr)
                xr, xi = xr + (pwr * cr - pwi * ci), xi + (pwr * ci + pwi * cr)
                sre_ref[rows, cols] = xr
                sim_ref[rows, cols] = xi
                cr = jnp.broadcast_to(xr[SUBLANES - 1:SUBLANES, :], xr.shape)
                ci = jnp.broadcast_to(xi[SUBLANES - 1:SUBLANES, :], xi.shape)
        else:
            lr = jnp.broadcast_to(kc_ref[0, 0:1, cols], (crow, S5_CHUNK))
            li = jnp.broadcast_to(kc_ref[1, 0:1, cols], (crow, S5_CHUNK))
            for t in range(tm // step):
                rows = slice(t * step, (t + 1) * step)
                cr, ci = (sre_ref[rows, cols] + (lr * cr - li * ci),
                          sim_ref[rows, cols] + (lr * ci + li * cr))
                sre_ref[rows, cols] = cr
                sim_ref[rows, cols] = ci
        cre_ref[:, cols] = cr
        cim_ref[:, cols] = ci
        ys.append(_dot(sre_ref[:, cols].astype(BF16), cblk_ref[m, 0:S5_CHUNK, :])
                  + _dot(sim_ref[:, cols].astype(BF16), cblk_ref[m, S5_CHUNK:, :]))

    tre_ref[0] = cre_ref[...]
    tim_ref[0] = cim_ref[...]

    y = jnp.concatenate(ys, axis=1) + dskip_ref[...] * u
    z = _dot(jax.nn.gelu(y).astype(BF16), wglu_ref[...])
    o_ref[...] = (z[:, :D_WIDTH] * jax.nn.sigmoid(z[:, D_WIDTH:])).astype(o_ref.dtype)


def _s5(proj, s0re, s0im, bblk, cblk, dskip, wglu, kconst, *, step, tps, tm):
    m = proj.shape[0]
    nm = m // tm
    crow = s0re.shape[1] if step > 1 else SUBLANES
    srow = s0re.shape[1]
    return pl.pallas_call(
        functools.partial(_s5_kernel, step=step, tps=tps, tm=tm),
        out_shape=[jax.ShapeDtypeStruct((m, D_WIDTH), BF16),
                   jax.ShapeDtypeStruct((nm, crow, S5_CH), F32),
                   jax.ShapeDtypeStruct((nm, crow, S5_CH), F32)],
        grid=(nm,),
        in_specs=[pl.BlockSpec((tm, D_WIDTH), lambda i: (i, 1)),
                  pl.BlockSpec((1, srow, S5_CH), lambda i: (i // tps, 0, 0)),
                  pl.BlockSpec((1, srow, S5_CH), lambda i: (i // tps, 0, 0)),
                  pl.BlockSpec(bblk.shape, lambda i: (0, 0, 0), pipeline_mode=pl.Buffered(1)),
                  pl.BlockSpec(cblk.shape, lambda i: (0, 0, 0), pipeline_mode=pl.Buffered(1)),
                  pl.BlockSpec((1, D_WIDTH), lambda i: (0, 0)),
                  pl.BlockSpec(wglu.shape, lambda i: (0, 0), pipeline_mode=pl.Buffered(1)),
                  pl.BlockSpec(kconst.shape, lambda i: (0, 0, 0), pipeline_mode=pl.Buffered(1))],
        out_specs=[pl.BlockSpec((tm, D_WIDTH), lambda i: (i, 0)),
                   pl.BlockSpec((1, crow, S5_CH), lambda i: (i, 0, 0)),
                   pl.BlockSpec((1, crow, S5_CH), lambda i: (i, 0, 0))],
        scratch_shapes=[pltpu.VMEM((tm, S5_CH), F32), pltpu.VMEM((tm, S5_CH), F32),
                        pltpu.VMEM((crow, S5_CH), F32), pltpu.VMEM((crow, S5_CH), F32)],
        compiler_params=_params(("arbitrary",)),
        name="s5",
    )(proj, s0re, s0im, bblk, cblk, dskip, wglu, kconst)


def _s5_constants(a_re, a_im, log_dt, b_re, b_im, c_re, c_im):
    per = S5_CHUNK // S5_STATE
    nchunk = S5_GROUPS // per
    on_diag = jnp.arange(per)[:, None] == jnp.arange(per)[None, :]
    zero = jnp.zeros((), BF16)
    dt = jnp.exp(log_dt)[:, None]
    zr, zi = a_re * dt, a_im * dt
    mag = jnp.exp(zr)
    lr, li = mag * jnp.cos(zi), mag * jnp.sin(zi)
    den = a_re * a_re + a_im * a_im
    nr, ni = lr - 1.0, li
    kr, ki = (nr * a_re + ni * a_im) / den, (ni * a_re - nr * a_im) / den
    bbr = kr[..., None] * b_re - ki[..., None] * b_im
    bbi = kr[..., None] * b_im + ki[..., None] * b_re
    qr, qi = [jnp.ones_like(lr)], [jnp.zeros_like(li)]
    for _ in range(S5_FOLD - 1):
        qr, qi = qr + [qr[-1] * lr - qi[-1] * li], qi + [qr[-1] * li + qi[-1] * lr]
    qr, qi = jnp.stack(qr)[..., None], jnp.stack(qi)[..., None]
    f = jnp.stack([qr * bbr - qi * bbi, qr * bbi + qi * bbr])
    f = f.reshape(2, S5_FOLD, nchunk, per, S5_STATE, S5_GROUP_DIM).transpose(2, 1, 3, 5, 0, 4).astype(BF16)
    f = f.reshape(nchunk, S5_FOLD * LANES, 2 * S5_STATE)
    rows = S5_FOLD * LANES
    src = jnp.arange(2 * S5_STATE)[:, None]
    dst = jnp.arange(2 * S5_CHUNK)[None, :]
    spread = ((src // S5_STATE == dst // S5_CHUNK) & (src % S5_STATE == dst % S5_STATE)).astype(BF16)
    row_group = (jnp.arange(rows) % LANES) // S5_GROUP_DIM
    col_group = (jnp.arange(2 * S5_CHUNK) % S5_CHUNK) // S5_STATE
    bblk = jnp.dot(f.reshape(nchunk * rows, 2 * S5_STATE), spread, preferred_element_type=BF16)
    bblk = jnp.where(row_group[:, None] == col_group[None, :], bblk.reshape(nchunk, rows, 2 * S5_CHUNK), zero)
    c = jnp.stack([c_re, -c_im]).reshape(2, nchunk, per, S5_GROUP_DIM, S5_STATE).transpose(1, 0, 2, 4, 3).astype(BF16)
    cblk = jnp.where(on_diag[None, None, :, None, :, None], c[:, :, :, :, None, :], zero)
    cblk = cblk.reshape(nchunk, 2 * S5_CHUNK, LANES)
    lr, li = lr.reshape(1, S5_CH), li.reshape(1, S5_CH)
    pr, pi = [lr], [li]
    for _ in range(SUBLANES - 1):
        pr, pi = pr + [pr[-1] * lr - pi[-1] * li], pi + [pr[-1] * li + pi[-1] * lr]
    rowid = jnp.arange(SUBLANES)[:, None]

    def masked(p, d):
        return jnp.where(rowid >= d, jnp.broadcast_to(p[d - 1], (SUBLANES, S5_CH)), 0.0)

    k_prompt = jnp.stack([masked(pr, S5_FOLD), masked(pi, S5_FOLD),
                          jnp.concatenate(pr, axis=0), jnp.concatenate(pi, axis=0)])
    k_sample = jnp.stack([jnp.broadcast_to(lr, (SUBLANES, S5_CH)), jnp.broadcast_to(li, (SUBLANES, S5_CH))])
    return bblk, cblk, k_prompt, k_sample


def _time_major(a):
    a = jnp.swapaxes(a, 0, 1)
    return a.reshape((a.shape[0] * a.shape[1],) + a.shape[2:])


def kernel(x_prompt, x_sample, state_gla, state_pool, state_s5_re, state_s5_im, state_ffn_conv, norm_mix_pre, norm_mix_post, norm_ffn_pre, norm_ffn_post, w_in_even, a_w_s, a_b_s, a_v_norm, b_w_gate, b_gate_bias, b_out_norm, w_out_even, w_in_odd, c_map, c_scale, s5_a_re, s5_a_im, s5_log_dt, s5_b_re, s5_b_im, s5_c_re, s5_c_im, s5_d, s5_w_glu, w_out_odd, ffn_w_up, ffn_conv_w, ffn_conv_b, ffn_w_down):
    bp = x_prompt.shape[0]
    nb, ts = x_sample.shape[0], x_sample.shape[1]
    xp = x_prompt.reshape(bp * SEQ, D_MODEL)
    xs = x_sample.reshape(nb * ts, D_MODEL)

    row = lambda v: v.reshape(1, -1)
    n_main = 2 * A_WIDTH + 2 * B_KEY_WIDTH + 2 * B_WIDTH
    w_in0 = w_in_even[0].astype(BF16)
    w_lr = jnp.pad(w_in0[:, n_main:], ((0, 0), (0, LANES - B_GATE_RANK)))
    w_gate = jnp.pad(b_w_gate[0], ((0, LANES - B_GATE_RANK), (0, 0))).astype(BF16)
    gate = (w_lr, w_gate, row(b_gate_bias[0]))
    pos = jnp.arange(A_BLOCK)
    causal = (pos[None, :] // CHUNK) <= (pos[:, None] // CHUNK)
    ws_prompt = jnp.where(causal[None], a_w_s[0], 0.0).astype(BF16)
    per = A_BLOCK // ts
    ws_small = jnp.where(causal[None, :ts, :ts], a_w_s[0][:, :ts, :ts], 0.0)
    ws_sample = jnp.einsum('hij,ab->haibj', ws_small, jnp.eye(per, dtype=F32)).reshape(A_HEADS, A_BLOCK, A_BLOCK).astype(BF16)
    bs_prompt = a_b_s[0].T
    bs_sample = jnp.tile(a_b_s[0][:, :ts].T, (per, 1))
    cmap = c_map[0].astype(BF16)
    bblk, cblk, k_prompt, k_sample = _s5_constants(s5_a_re[0], s5_a_im[0], s5_log_dt[0], s5_b_re[0], s5_b_im[0],
                                                   s5_c_re[0], s5_c_im[0])
    wglu = s5_w_glu[0].astype(BF16)

    tm = 512
    tf = 512
    tps_p = SEQ // tm
    ffn_hist_p = SUBLANES
    pool_hist_p = 2 * SUBLANES
    step_s = nb
    tps_s = (nb * ts) // tm
    ffn_hist_s = 2 * step_s
    pool_hist_s = (POOL_BUF + 1) * step_s

    tm_ffn = 1024
    tps_ffn_p = SEQ // tm_ffn
    tps_ffn_s = (nb * ts) // tm_ffn

    ffn_w16 = {}

    def ffn_layer(x, layer, state, *, step, hist, tps, convert):
        weights = (ffn_w_up, ffn_w_down) if convert else ffn_w16[layer]
        res = _ffn(x, norm_ffn_pre[:, None], weights, ffn_conv_w, ffn_conv_b[:, None], norm_ffn_post[:, None],
                   state, layer=layer, step=step, hist=hist, tps=tps, tm=tm_ffn, tf=256 if convert else tf,
                   convert=convert)
        if convert:
            ffn_w16[layer] = tuple(res[2:])
        return res[0], res[1]

    proj, lg = _inproj(xs, row(norm_mix_pre[0]), w_in0, gate, n=n_main, tm=1024, tn=1024)
    a_out, a_v = _sgu(proj, row(a_v_norm[0]), ws_sample, bs_sample, nblk=2, emit_av=True)
    b_out, gla_s = _gla(proj, lg, state_gla[0], row(b_out_norm[0]), nseq=nb, t=ts, clen=ts, ngrp=8, chain=False)
    xs, w_out0 = _outproj(a_out, b_out, w_out_even, xs, row(norm_mix_post[0]), tm=tm, convert=True)
    xs = _time_major(xs.reshape(nb, ts, D_MODEL))
    ffn_state = lambda layer: _time_major(state_ffn_conv[layer])[None]
    xs, ffn0_s = ffn_layer(xs, 0, ffn_state(0), step=step_s, hist=ffn_hist_s, tps=tps_ffn_s, convert=True)
    proj, w_in1 = _inproj(xs, row(norm_mix_pre[1]), w_in_odd, n=D_MODEL, tm=1024, tn=512, convert=True)
    pool_state = jnp.pad(_time_major(state_pool[0]), ((step_s, 0), (0, 0)))[None]
    c_out, pool_tail_s = _pool(proj, pool_state, cmap, row(c_scale[0]),
                               step=step_s, hist=pool_hist_s, tps=tps_s, tm=tm, pos0=PAST_LEN)
    d_out, s5re_tail_s, s5im_tail_s = _s5(proj, state_s5_re[0].reshape(1, nb, S5_CH),
                                          state_s5_im[0].reshape(1, nb, S5_CH),
                                          bblk, cblk, row(s5_d[0]), wglu, k_sample, step=step_s, tps=tps_s, tm=tm)
    xs, w_out1 = _outproj(c_out, d_out, w_out_odd, xs, row(norm_mix_post[1]), tm=tm, convert=True)
    xs, ffn1_s = ffn_layer(xs, 1, ffn_state(1), step=step_s, hist=ffn_hist_s, tps=tps_ffn_s, convert=True)

    proj, lg = _inproj(xp, row(norm_mix_pre[0]), w_in0, gate, n=n_main, tm=1024, tn=1280)
    a_out = _sgu(proj, row(a_v_norm[0]), ws_prompt, bs_prompt, nblk=4, emit_av=False)[0]
    b_out, gla_p = _gla(proj, lg, jnp.zeros((bp, B_HEADS, B_KEY_DIM, B_VAL_DIM), F32), row(b_out_norm[0]),
                        nseq=bp, t=SEQ, clen=CHUNK, ngrp=4, chain=True)
    xp = _outproj(a_out, b_out, w_out0, xp, row(norm_mix_post[0]), tm=tm)
    xp, ffn0_p = ffn_layer(xp, 0, jnp.zeros((bp, ffn_hist_p, D_FF), F32), step=1, hist=ffn_hist_p,
                           tps=tps_ffn_p, convert=False)
    proj = _inproj(xp, row(norm_mix_pre[1]), w_in1, n=D_MODEL, tm=1024, tn=D_MODEL)
    c_out, pool_tail_p = _pool(proj, jnp.zeros((bp, pool_hist_p, C_WIDTH), F32), cmap, row(c_scale[0]),
                               step=1, hist=pool_hist_p, tps=tps_p, tm=tm, pos0=0)
    zero_state = jnp.zeros((bp, 1, S5_CH), F32)
    d_out, s5re_tail_p, s5im_tail_p = _s5(proj, zero_state, zero_state, bblk, cblk, row(s5_d[0]), wglu, k_prompt,
                                          step=1, tps=tps_p, tm=tm)
    xp = _outproj(c_out, d_out, w_out1, xp, row(norm_mix_post[1]), tm=tm)
    xp, ffn1_p = ffn_layer(xp, 1, jnp.zeros((bp, ffn_hist_p, D_FF), F32), step=1, hist=ffn_hist_p,
                           tps=tps_ffn_p, convert=False)

    last = slice(tps_p - 1, None, tps_p)
    y_prompt = xp.reshape(bp, SEQ, D_MODEL)
    gla_prompt = gla_p[None]
    pool_prompt = pool_tail_p[last, pool_hist_p - POOL_BUF:][None]
    s5_re_prompt = s5re_tail_p[last, 0].reshape(1, bp, S5_GROUPS, S5_STATE)
    s5_im_prompt = s5im_tail_p[last, 0].reshape(1, bp, S5_GROUPS, S5_STATE)
    last_ffn = slice(tps_ffn_p - 1, None, tps_ffn_p)
    ffn_prompt = jnp.stack([ffn0_p[last_ffn, ffn_hist_p - 2:], ffn1_p[last_ffn, ffn_hist_p - 2:]])

    def batch_major(a, nt):
        return jnp.swapaxes(a.reshape(nt, nb, a.shape[-1]), 0, 1)

    y_sample = batch_major(xs, ts)
    gla_sample = gla_s[None]
    av_sample = a_v.reshape(1, nb, ts, A_WIDTH)
    pool_sample = batch_major(pool_tail_s[-1, step_s:], POOL_BUF)[None]
    s5_re_sample = s5re_tail_s[-1].reshape(1, nb, S5_GROUPS, S5_STATE)
    s5_im_sample = s5im_tail_s[-1].reshape(1, nb, S5_GROUPS, S5_STATE)
    ffn_sample = jnp.stack([batch_major(ffn0_s[-1], 2), batch_major(ffn1_s[-1], 2)])

    return (y_prompt, y_sample, gla_prompt, gla_sample, av_sample, pool_prompt, pool_sample,
            s5_re_prompt, s5_im_prompt, s5_re_sample, s5_im_sample, ffn_prompt, ffn_sample)
```

```python
import functools
import math

import jax
import jax.numpy as jnp
from jax import lax
from jax.experimental import pallas as pl
from jax.experimental.pallas import tpu as pltpu

F32 = jnp.float32
BF16 = jnp.bfloat16

D_MODEL = 2048
SEQ = 4096
DEC_BATCH = 32
DEC_SEQ = 32
PAST_LEN = 4096
CHUNK = 64
A_WIDTH = 1024
A_HEADS = 8
A_BLOCK = 128
B_HEADS = 4
B_KEY_DIM = 128
B_KEY_WIDTH = 512
B_VAL_DIM = 256
B_WIDTH = 1024
B_GATE_RANK = 16
B_GATE_TAU = 16.0
C_WIDTH = 1024
C_GROUP_DIM = 256
POOL_WINDOWS = (2, 4, 8, 16)
POOL_BUF = 15
D_WIDTH = 1024
S5_GROUPS = 64
S5_GROUP_DIM = 16
S5_STATE = 64
S5_CH = S5_GROUPS * S5_STATE
D_FF = 5632
EPS = 1e-6

LANES = 128
SUBLANES = 8
VMEM_LIMIT = 56 * 1024 * 1024


def _params(sem):
    return pltpu.CompilerParams(dimension_semantics=sem, vmem_limit_bytes=VMEM_LIMIT)


def _rms(x, g):
    return x * lax.rsqrt(jnp.mean(x * x, axis=-1, keepdims=True) + EPS) * g


def _dot(a, b):
    return jnp.dot(a, b, preferred_element_type=F32)


def _inproj_kernel(x_ref, g_ref, w_ref, *rest, with_gate, convert):
    rest = list(rest)
    if with_gate:
        wlr_ref, wgate_ref, gbias_ref = rest[:3]
        rest = rest[3:]
    o_ref = rest.pop(0)
    if with_gate:
        lg_ref = rest.pop(0)
    if convert:
        w16_ref = rest.pop(0)
    h_ref, = rest
    j = pl.program_id(1)

    def project(hb):
        w = w_ref[...]
        if convert:
            w = w.astype(BF16)
            w16_ref[...] = w
        o_ref[...] = _dot(hb, w)

    @pl.when(j == 0)
    def _():
        hb = _rms(x_ref[...], g_ref[...]).astype(BF16)
        h_ref[...] = hb
        project(hb)
        if with_gate:
            glr = _dot(hb, wlr_ref[...])
            z = _dot(glr.astype(BF16), wgate_ref[...]) + gbias_ref[...]
            lg_ref[...] = (jnp.minimum(z, 0.0) - jnp.log(1.0 + jnp.exp(-jnp.abs(z)))) * (1.0 / B_GATE_TAU)

    @pl.when(j > 0)
    def _():
        project(h_ref[...])


def _inproj(x, g, w, gate=None, *, n, tm, tn, convert=False):
    m, d = x.shape
    grid = (m // tm, n // tn)
    one_tile = m == tm
    in_specs = [pl.BlockSpec((tm, d), lambda i, j: (i, 0), pipeline_mode=pl.Buffered(1) if one_tile else None),
                pl.BlockSpec((1, d), lambda i, j: (0, 0)),
                pl.BlockSpec((None, d, tn), lambda i, j: (0, 0, j)) if convert
                else pl.BlockSpec((d, tn), lambda i, j: (0, j), pipeline_mode=pl.Buffered(1) if n == tn else None)]
    out_shape = [jax.ShapeDtypeStruct((m, n), F32)]
    out_specs = [pl.BlockSpec((tm, tn), lambda i, j: (i, j))]
    args = [x, g, w]
    if gate is not None:
        wlr, wgate, gbias = gate
        in_specs += [pl.BlockSpec(wlr.shape, lambda i, j: (0, 0)),
                     pl.BlockSpec(wgate.shape, lambda i, j: (0, 0)),
                     pl.BlockSpec(gbias.shape, lambda i, j: (0, 0))]
        out_shape.append(jax.ShapeDtypeStruct((m, B_KEY_WIDTH), F32))
        out_specs.append(pl.BlockSpec((tm, B_KEY_WIDTH), lambda i, j: (i, 0)))
        args += [wlr, wgate, gbias]
    if convert:
        out_shape.append(jax.ShapeDtypeStruct((d, n), BF16))
        out_specs.append(pl.BlockSpec((d, tn), lambda i, j: (0, j)))
    res = pl.pallas_call(
        functools.partial(_inproj_kernel, with_gate=gate is not None, convert=convert),
        out_shape=out_shape, grid=grid, in_specs=in_specs, out_specs=out_specs,
        scratch_shapes=[pltpu.VMEM((tm, d), BF16)],
        compiler_params=_params(("parallel", "arbitrary")),
        name=("inproj_gate" if gate is not None else "inproj") + ("_convert" if convert else ""),
    )(*args)
    return res if len(res) > 1 else res[0]


def _sgu_kernel(u_ref, v_ref, gain_ref, w_ref, b_ref, o_ref, *av_ref, nblk):
    for n in range(nblk):
        rows = slice(n * A_BLOCK, (n + 1) * A_BLOCK)
        v = jax.nn.gelu(v_ref[rows, :])
        mu = jnp.mean(v, axis=-1, keepdims=True)
        vc = v - mu
        vn = vc * lax.rsqrt(jnp.mean(vc * vc, axis=-1, keepdims=True) + EPS) * gain_ref[...]
        if av_ref:
            av_ref[0][rows, :] = vn
        vb = vn.astype(BF16)
        for h in range(A_HEADS):
            cols = slice(h * LANES, (h + 1) * LANES)
            s = _dot(w_ref[h], vb[:, cols]) + b_ref[:, h:h + 1]
            o_ref[rows, cols] = (jax.nn.gelu(u_ref[rows, cols]) * s).astype(o_ref.dtype)


def _sgu(proj, gain, w, b, *, nblk, emit_av):
    m = proj.shape[0]
    tm = nblk * A_BLOCK
    n_out = 2 if emit_av else 1
    return pl.pallas_call(
        functools.partial(_sgu_kernel, nblk=nblk),
        out_shape=[jax.ShapeDtypeStruct((m, A_WIDTH), BF16), jax.ShapeDtypeStruct((m, A_WIDTH), F32)][:n_out],
        grid=(m // tm,),
        in_specs=[pl.BlockSpec((tm, A_WIDTH), lambda i: (i, 0)),
                  pl.BlockSpec((tm, A_WIDTH), lambda i: (i, 1)),
                  pl.BlockSpec((1, A_WIDTH), lambda i: (0, 0)),
                  pl.BlockSpec(w.shape, lambda i: (0, 0, 0)),
                  pl.BlockSpec(b.shape, lambda i: (0, 0))],
        out_specs=[pl.BlockSpec((tm, A_WIDTH), lambda i: (i, 0)),
                   pl.BlockSpec((tm, A_WIDTH), lambda i: (i, 0))][:n_out],
        compiler_params=_params(("parallel",)),
        name="sgu",
    )(proj, proj, gain, w, b)


def _gla_kernel(q_ref, k_ref, v_ref, r_ref, lg_ref, s0_ref, og_ref, o_ref, sout_ref, st_ref, *, clen, ngrp, chain):
    c = pl.program_id(1)
    rows_all = ngrp * clen
    shift = int(math.log2(clen))

    if chain:
        @pl.when(c == 0)
        def _():
            for h in range(B_HEADS):
                st_ref[h] = s0_ref[0, h].T

    row_i = lax.broadcasted_iota(jnp.int32, (rows_all, rows_all), 0)
    col_i = lax.broadcasted_iota(jnp.int32, (rows_all, rows_all), 1)
    same_group = lax.shift_right_logical(row_i, shift) == lax.shift_right_logical(col_i, shift)
    tri = ((row_i >= col_i) & same_group).astype(F32)
    cum = jnp.dot(tri, lg_ref[...], precision=lax.Precision.HIGHEST,
                  preferred_element_type=F32)
    tots = [cum[(g + 1) * clen - 1:(g + 1) * clen, :] for g in range(ngrp)]
    tot_rows = jnp.concatenate([jnp.broadcast_to(t, (clen, B_KEY_WIDTH)) for t in tots], axis=0)
    kd = (k_ref[...] * jnp.exp(tot_rows - cum)).astype(BF16)
    qs = (q_ref[...] * (B_KEY_DIM ** -0.5)).astype(BF16)
    vb = v_ref[...].astype(BF16)
    sr = jax.nn.silu(r_ref[...])
    grp = lax.shift_right_logical(lax.broadcasted_iota(jnp.int32, (rows_all, 1), 0), shift)
    zero = jnp.zeros((), BF16)

    def by_group(x):
        return jnp.concatenate([jnp.where(grp == g, x, zero) for g in range(ngrp)], axis=1)

    for h in range(B_HEADS):
        kc = slice(h * B_KEY_DIM, (h + 1) * B_KEY_DIM)
        vc = slice(h * B_VAL_DIM, (h + 1) * B_VAL_DIM)
        upd = lax.dot_general(vb[:, vc], by_group(kd[:, kc]), (((0,), (0,)), ((), ())),
                              preferred_element_type=F32)
        states = []
        st = st_ref[h] if chain else None
        for g in range(ngrp):
            if not chain:
                st = s0_ref[g, h].T
            st = jnp.exp(tots[g][:, kc]) * st + upd[:, g * B_KEY_DIM:(g + 1) * B_KEY_DIM]
            states.append(st.astype(BF16))
            if not chain:
                sout_ref[g, h] = st.T
        if chain:
            st_ref[h] = st
        o = lax.dot_general(by_group(qs[:, kc]), jnp.concatenate(states, axis=1), (((1,), (1,)), ((), ())),
                            preferred_element_type=F32)
        o = o * lax.rsqrt(jnp.mean(o * o, axis=-1, keepdims=True) + EPS)
        o = o * og_ref[:, vc] * sr[:, vc]
        o_ref[:, vc] = o.astype(o_ref.dtype)

    if chain:
        @pl.when(c == pl.num_programs(1) - 1)
        def _():
            for h in range(B_HEADS):
                sout_ref[0, h] = st_ref[h].T


def _gla(proj, lg, s0, og, *, nseq, t, clen, ngrp, chain):
    rows = clen * ngrp
    m = nseq * t
    if chain:
        steps = t // rows
        grid = (nseq, steps)
        rmap = lambda b, c: b * steps + c
        nstate = 1
    else:
        grid = (m // rows, 1)
        rmap = lambda b, c: b
        nstate = ngrp
    blk = lambda width, col: pl.BlockSpec((rows, width), lambda b, c: (rmap(b, c), col))
    state_spec = pl.BlockSpec((nstate, B_HEADS, B_KEY_DIM, B_VAL_DIM), lambda b, c: (b, 0, 0, 0))
    return pl.pallas_call(
        functools.partial(_gla_kernel, clen=clen, ngrp=ngrp, chain=chain),
        out_shape=[jax.ShapeDtypeStruct((m, B_WIDTH), BF16),
                   jax.ShapeDtypeStruct((nseq, B_HEADS, B_KEY_DIM, B_VAL_DIM), F32)],
        grid=grid,
        in_specs=[blk(B_KEY_WIDTH, 4),
                  blk(B_KEY_WIDTH, 5),
                  blk(B_WIDTH, 3),
                  blk(B_WIDTH, 4),
                  blk(B_KEY_WIDTH, 0),
                  state_spec,
                  pl.BlockSpec((1, B_WIDTH), lambda b, c: (0, 0))],
        out_specs=[blk(B_WIDTH, 0), state_spec],
        scratch_shapes=[pltpu.VMEM((B_HEADS, B_VAL_DIM, B_KEY_DIM), F32)],
        compiler_params=_params(("parallel", "arbitrary")),
        name="gla",
    )(proj, proj, proj, proj, lg, s0, og)


def _outproj_kernel(a_ref, b_ref, w_ref, x_ref, g_ref, o_ref, *w16_ref, convert):
    wb_ref = w_ref
    if convert:
        wb_ref, = w16_ref

        @pl.when(pl.program_id(0) == 0)
        def _():
            wb_ref[...] = w_ref[...].astype(BF16)

    ka = a_ref.shape[1]
    y = _dot(a_ref[...], wb_ref[0:ka, :]) + _dot(b_ref[...], wb_ref[ka:, :])
    o_ref[...] = x_ref[...] + _rms(y, g_ref[...])


def _outproj(a, b, w, x, g, *, tm, convert=False):
    m, d = x.shape
    k = a.shape[1] + b.shape[1]
    once = pl.Buffered(1)
    out_shape = [jax.ShapeDtypeStruct((m, d), F32)]
    out_specs = [pl.BlockSpec((tm, d), lambda i: (i, 0))]
    if convert:
        w_spec = pl.BlockSpec((None, k, d), lambda i: (0, 0, 0), pipeline_mode=once)
        out_shape.append(jax.ShapeDtypeStruct((k, d), BF16))
        out_specs.append(pl.BlockSpec((k, d), lambda i: (0, 0), pipeline_mode=once))
    else:
        w_spec = pl.BlockSpec((k, d), lambda i: (0, 0), pipeline_mode=once)
    res = pl.pallas_call(
        functools.partial(_outproj_kernel, convert=convert),
        out_shape=out_shape,
        grid=(m // tm,),
        in_specs=[pl.BlockSpec((tm, a.shape[1]), lambda i: (i, 0)),
                  pl.BlockSpec((tm, b.shape[1]), lambda i: (i, 0)),
                  w_spec,
                  pl.BlockSpec((tm, d), lambda i: (i, 0)),
                  pl.BlockSpec((1, d), lambda i: (0, 0))],
        out_specs=out_specs,
        compiler_params=_params(("arbitrary",)),
        name="outproj_convert" if convert else "outproj",
    )(a, b, w, x, g)
    return res if convert else res[0]


def _ffn_kernel(x_ref, gpre_ref, wg_ref, wv_ref, cw_ref, cb_ref, wd_ref, gpost_ref, st_ref,
                o_ref, tail_ref, *rest, step, hist, tps, tm, nf, convert):
    if convert:
        wg_out, wv_out, wd_out, h_ref, gext_ref, carry_ref, act_a, act_b = rest
    else:
        h_ref, gext_ref, carry_ref, act_a, act_b = rest
    i = pl.program_id(0)
    j = pl.program_id(1)
    first = (i % tps) == 0

    def up_and_gate(act_ref, hb=None):
        if hb is None:
            hb = h_ref[...]
        wg, wv = wg_ref[...], wv_ref[...]
        if convert:
            wg, wv = wg.astype(BF16), wv.astype(BF16)
            wg_out[...] = wg
            wv_out[...] = wv
        gate = _dot(hb, wg)
        val = _dot(hb, wv)
        cw = cw_ref[j]
        gext_ref[0:hist, :] = jnp.where(first, st_ref[0, j], carry_ref[j])
        gext_ref[hist:hist + tm, :] = gate
        prev2 = gext_ref[hist - 2 * step:hist - 2 * step + tm, :]
        prev1 = gext_ref[hist - step:hist - step + tm, :]
        conv = cb_ref[j] + cw[0:1, :] * prev2 + cw[1:2, :] * prev1 + cw[2:3, :] * gate
        act_ref[...] = (jax.nn.gelu(conv) * val).astype(BF16)
        tail = gate[tm - hist:, :]
        carry_ref[j] = tail
        tail_ref[0, j] = tail

    def down(act_ref):
        wd = wd_ref[...]
        if convert:
            wd = wd.astype(BF16)
            wd_out[...] = wd
        o_ref[...] += _dot(act_ref[...], wd)

    @pl.when(j == 0)
    def _():
        @pl.when(i == 0)
        def _():
            carry_ref[...] = jnp.zeros_like(carry_ref)

        o_ref[...] = jnp.zeros_like(o_ref)
        hb = _rms(x_ref[...], gpre_ref[...]).astype(BF16)
        h_ref[...] = hb
        up_and_gate(act_a, hb)

    for parity, (src, dst) in enumerate(((act_b, act_a), (act_a, act_b))):
        @pl.when((j > 0) & (j < nf) & (j % 2 == parity))
        def _(src=src, dst=dst):
            down(src)
            up_and_gate(dst)

    @pl.when(j == nf)
    def _():
        down(act_a if (nf - 1) % 2 == 0 else act_b)
        o_ref[...] = x_ref[...] + _rms(o_ref[...], gpost_ref[...])


def _ffn(x, gpre, weights, cw, cb, gpost, state, *, layer, step, hist, tps, tm, tf, convert):
    m, d = x.shape
    nf = D_FF // tf
    nm = m // tm
    up = lambda j: jnp.minimum(j, nf - 1)
    down = lambda j: jnp.maximum(j - 1, 0)

    def by_tile(a):
        a = a.reshape(a.shape[:-1] + (nf, tf))
        return jnp.swapaxes(a, -2, -3)

    cw, cb, state = by_tile(cw), by_tile(cb), by_tile(state)
    if convert:
        w_up, w_down = weights
        w_args = (w_up, w_up, w_down)
        w_specs = [pl.BlockSpec((None, d, tf), lambda i, j: (layer, 0, up(j))),
                   pl.BlockSpec((None, d, tf), lambda i, j: (layer, 0, nf + up(j))),
                   pl.BlockSpec((None, tf, d), lambda i, j: (layer, down(j), 0))]
        extra_shapes = [jax.ShapeDtypeStruct((d, D_FF), BF16), jax.ShapeDtypeStruct((d, D_FF), BF16),
                        jax.ShapeDtypeStruct((D_FF, d), BF16)]
        extra_specs = [pl.BlockSpec((d, tf), lambda i, j: (0, up(j))),
                       pl.BlockSpec((d, tf), lambda i, j: (0, up(j))),
                       pl.BlockSpec((tf, d), lambda i, j: (down(j), 0))]
    else:
        w_args = weights
        w_specs = [pl.BlockSpec((d, tf), lambda i, j: (0, up(j))),
                   pl.BlockSpec((d, tf), lambda i, j: (0, up(j))),
                   pl.BlockSpec((tf, d), lambda i, j: (down(j), 0))]
        extra_shapes, extra_specs = [], []
    res = pl.pallas_call(
        functools.partial(_ffn_kernel, step=step, hist=hist, tps=tps, tm=tm, nf=nf, convert=convert),
        out_shape=[jax.ShapeDtypeStruct((m, d), F32), jax.ShapeDtypeStruct((nm, nf, hist, tf), F32)] + extra_shapes,
        grid=(nm, nf + 1),
        in_specs=[pl.BlockSpec((tm, d), lambda i, j: (i, 0)),
                  pl.BlockSpec((None, 1, d), lambda i, j: (layer, 0, 0)),
                  w_specs[0], w_specs[1],
                  pl.BlockSpec((None, nf, 3, tf), lambda i, j: (layer, 0, 0, 0)),
                  pl.BlockSpec((None, nf, 1, tf), lambda i, j: (layer, 0, 0, 0)),
                  w_specs[2],
                  pl.BlockSpec((None, 1, d), lambda i, j: (layer, 0, 0)),
                  pl.BlockSpec((1, nf, hist, tf), lambda i, j: (i // tps, 0, 0, 0))],
        out_specs=[pl.BlockSpec((tm, d), lambda i, j: (i, 0), pipeline_mode=pl.Buffered(1)),
                   pl.BlockSpec((1, nf, hist, tf), lambda i, j: (i, 0, 0, 0))] + extra_specs,
        scratch_shapes=[pltpu.VMEM((tm, d), BF16),
                        pltpu.VMEM((hist + tm, tf), F32), pltpu.VMEM((nf, hist, tf), F32),
                        pltpu.VMEM((tm, tf), BF16), pltpu.VMEM((tm, tf), BF16)],
        compiler_params=_params(("arbitrary", "arbitrary")),
        name="ffn_convert" if convert else "ffn",
    )(x, gpre, w_args[0], w_args[1], cw, cb, w_args[2], gpost, state)
    res = list(res)
    res[1] = jnp.swapaxes(res[1], 1, 2).reshape(nm, hist, D_FF)
    return res


def _pool_kernel(c_ref, st_ref, cmap_ref, cs_ref, o_ref, tail_ref, ext_ref, carry_ref,
                 *, step, hist, tps, tm, pos0):
    i = pl.program_id(0)
    first = (i % tps) == 0

    @pl.when(first)
    def _():
        ext_ref[0:hist, :] = st_ref[0]

    @pl.when(jnp.logical_not(first))
    def _():
        ext_ref[0:hist, :] = carry_ref[...]

    ext_ref[hist:hist + tm, :] = c_ref[...]
    row = lax.broadcasted_iota(jnp.int32, (tm, 1), 0)
    if step > 1:
        row = lax.shift_right_logical(row, int(math.log2(step)))
    pos = pos0 + (i % tps) * (tm // step) + row
    for g, win in enumerate(POOL_WINDOWS):
        cols = slice(g * C_GROUP_DIM, (g + 1) * C_GROUP_DIM)
        acc = ext_ref[:, cols]
        d = 1
        while d < win:
            acc = acc + pltpu.roll(acc, d * step, 0)
            d *= 2
        tot = acc[hist:, :]
        cur = ext_ref[hist:hist + tm, cols]
        cnt = jnp.minimum(pos + 1, win).astype(F32)
        delta = tot / cnt - cur
        y = _dot(delta.astype(BF16), cmap_ref[g]) * cs_ref[:, cols]
        o_ref[:, cols] = y.astype(o_ref.dtype)
    tail = ext_ref[tm:tm + hist, :]
    carry_ref[...] = tail
    tail_ref[0] = tail


def _pool(proj, state, cmap, cscale, *, step, hist, tps, tm, pos0):
    m = proj.shape[0]
    nm = m // tm
    return pl.pallas_call(
        functools.partial(_pool_kernel, step=step, hist=hist, tps=tps, tm=tm, pos0=pos0),
        out_shape=[jax.ShapeDtypeStruct((m, C_WIDTH), BF16), jax.ShapeDtypeStruct((nm, hist, C_WIDTH), F32)],
        grid=(nm,),
        in_specs=[pl.BlockSpec((tm, C_WIDTH), lambda i: (i, 0)),
                  pl.BlockSpec((1, hist, C_WIDTH), lambda i: (i // tps, 0, 0)),
                  pl.BlockSpec(cmap.shape, lambda i: (0, 0, 0)),
                  pl.BlockSpec((1, C_WIDTH), lambda i: (0, 0))],
        out_specs=[pl.BlockSpec((tm, C_WIDTH), lambda i: (i, 0)),
                   pl.BlockSpec((1, hist, C_WIDTH), lambda i: (i, 0, 0))],
        scratch_shapes=[pltpu.VMEM((hist + tm, C_WIDTH), F32), pltpu.VMEM((hist, C_WIDTH), F32)],
        compiler_params=_params(("arbitrary",)),
        name="pool",
    )(proj, state, cmap, cscale)


S5_CHUNK = 512
S5_NCHUNK = S5_CH // S5_CHUNK
S5_FOLD = SUBLANES // 2


def _s5_kernel(u_ref, s0re_ref, s0im_ref, bblk_ref, cblk_ref, dskip_ref, wglu_ref, kc_ref,
               o_ref, tre_ref, tim_ref, cre_ref, cim_ref, *, step, tps, tm):
    i = pl.program_id(0)
    crow = cre_ref.shape[0]

    @pl.when((i % tps) == 0)
    def _():
        cre_ref[...] = jnp.broadcast_to(s0re_ref[0], cre_ref.shape) if step == 1 else s0re_ref[0]
        cim_ref[...] = jnp.broadcast_to(s0im_ref[0], cim_ref.shape) if step == 1 else s0im_ref[0]

    u = u_ref[...]
    lhs = [u.astype(BF16)]
    if step == 1:
        row_in_block = lax.broadcasted_iota(jnp.int32, (tm, 1), 0) & (SUBLANES - 1)
        for k in range(1, S5_FOLD):
            lhs.append(jnp.where(row_in_block >= k, pltpu.roll(u, k, 0), 0.0).astype(BF16))
    ys = []
    for m in range(S5_NCHUNK):
        cols = slice(m * S5_CHUNK, (m + 1) * S5_CHUNK)
        ucols = slice(m * LANES, (m + 1) * LANES)
        if step == 1:
            r = _dot(jnp.concatenate([x[:, ucols] for x in lhs], axis=1), bblk_ref[m])
        else:
            r = _dot(lhs[0][:, ucols], bblk_ref[m, 0:LANES, :])
        cr, ci = cre_ref[:, cols], cim_ref[:, cols]
        sre, sim = [], []
        if step == 1:
            ar, ai, pwr, pwi = [kc_ref[k, :, cols] for k in range(4)]
            for rb in range(tm // SUBLANES):
                rows = slice(rb * SUBLANES, (rb + 1) * SUBLANES)
                xr = r[rows, :S5_CHUNK]
                xi = r[rows, S5_CHUNK:]
                sr = pltpu.roll(xr, S5_FOLD, 0)
                si = pltpu.roll(xi, S5_FOLD, 0)
                xr, xi = xr + (ar * sr - ai * si), xi + (ar * si + ai * sr)
                xr, xi = xr + (pwr * cr - pwi * ci), xi + (pwr * ci + pwi * cr)
                sre.append(xr)
                sim.append(xi)
                cr = jnp.broadcast_to(xr[SUBLANES - 1:SUBLANES, :], xr.shape)
                ci = jnp.broadcast_to(xi[SUBLANES - 1:SUBLANES, :], xi.shape)
        else:
            lr = jnp.broadcast_to(kc_ref[0, 0:1, cols], (crow, S5_CHUNK))
            li = jnp.broadcast_to(kc_ref[1, 0:1, cols], (crow, S5_CHUNK))
            for t in range(tm // step):
                rows = slice(t * step, (t + 1) * step)
                cr, ci = (r[rows, :S5_CHUNK] + (lr * cr - li * ci),
                          r[rows, S5_CHUNK:] + (lr * ci + li * cr))
                sre.append(cr)
                sim.append(ci)
        cre_ref[:, cols] = cr
        cim_ref[:, cols] = ci
        ys.append(_dot(jnp.concatenate(sre, axis=0).astype(BF16), cblk_ref[m, 0:S5_CHUNK, :])
                  + _dot(jnp.concatenate(sim, axis=0).astype(BF16), cblk_ref[m, S5_CHUNK:, :]))

    tre_ref[0] = cre_ref[...]
    tim_ref[0] = cim_ref[...]

    y = jnp.concatenate(ys, axis=1) + dskip_ref[...] * u
    z = _dot(jax.nn.gelu(y).astype(BF16), wglu_ref[...])
    o_ref[...] = (z[:, :D_WIDTH] * jax.nn.sigmoid(z[:, D_WIDTH:])).astype(o_ref.dtype)


def _s5(proj, s0re, s0im, bblk, cblk, dskip, wglu, kconst, *, step, tps, tm):
    m = proj.shape[0]
    nm = m // tm
    crow = s0re.shape[1] if step > 1 else SUBLANES
    srow = s0re.shape[1]
    return pl.pallas_call(
        functools.partial(_s5_kernel, step=step, tps=tps, tm=tm),
        out_shape=[jax.ShapeDtypeStruct((m, D_WIDTH), BF16),
                   jax.ShapeDtypeStruct((nm, crow, S5_CH), F32),
                   jax.ShapeDtypeStruct((nm, crow, S5_CH), F32)],
        grid=(nm,),
        in_specs=[pl.BlockSpec((tm, D_WIDTH), lambda i: (i, 1)),
                  pl.BlockSpec((1, srow, S5_CH), lambda i: (i // tps, 0, 0)),
                  pl.BlockSpec((1, srow, S5_CH), lambda i: (i // tps, 0, 0)),
                  pl.BlockSpec(bblk.shape, lambda i: (0, 0, 0), pipeline_mode=pl.Buffered(1)),
                  pl.BlockSpec(cblk.shape, lambda i: (0, 0, 0), pipeline_mode=pl.Buffered(1)),
                  pl.BlockSpec((1, D_WIDTH), lambda i: (0, 0)),
                  pl.BlockSpec(wglu.shape, lambda i: (0, 0), pipeline_mode=pl.Buffered(1)),
                  pl.BlockSpec(kconst.shape, lambda i: (0, 0, 0), pipeline_mode=pl.Buffered(1))],
        out_specs=[pl.BlockSpec((tm, D_WIDTH), lambda i: (i, 0)),
                   pl.BlockSpec((1, crow, S5_CH), lambda i: (i, 0, 0)),
                   pl.BlockSpec((1, crow, S5_CH), lambda i: (i, 0, 0))],
        scratch_shapes=[pltpu.VMEM((crow, S5_CH), F32), pltpu.VMEM((crow, S5_CH), F32)],
        compiler_params=_params(("arbitrary",)),
        name="s5",
    )(proj, s0re, s0im, bblk, cblk, dskip, wglu, kconst)


def _s5_constants(a_re, a_im, log_dt, b_re, b_im, c_re, c_im):
    per = S5_CHUNK // S5_STATE
    nchunk = S5_GROUPS // per
    on_diag = jnp.arange(per)[:, None] == jnp.arange(per)[None, :]
    zero = jnp.zeros((), BF16)
    dt = jnp.exp(log_dt)[:, None]
    zr, zi = a_re * dt, a_im * dt
    mag = jnp.exp(zr)
    lr, li = mag * jnp.cos(zi), mag * jnp.sin(zi)
    den = a_re * a_re + a_im * a_im
    nr, ni = lr - 1.0, li
    kr, ki = (nr * a_re + ni * a_im) / den, (ni * a_re - nr * a_im) / den
    bbr = kr[..., None] * b_re - ki[..., None] * b_im
    bbi = kr[..., None] * b_im + ki[..., None] * b_re
    qr, qi = [jnp.ones_like(lr)], [jnp.zeros_like(li)]
    for _ in range(S5_FOLD - 1):
        qr, qi = qr + [qr[-1] * lr - qi[-1] * li], qi + [qr[-1] * li + qi[-1] * lr]
    qr, qi = jnp.stack(qr)[..., None], jnp.stack(qi)[..., None]
    f = jnp.stack([qr * bbr - qi * bbi, qr * bbi + qi * bbr])
    f = f.reshape(2, S5_FOLD, nchunk, per, S5_STATE, S5_GROUP_DIM).transpose(2, 1, 3, 5, 0, 4).astype(BF16)
    f = f.reshape(nchunk, S5_FOLD * LANES, 2 * S5_STATE)
    rows = S5_FOLD * LANES
    src = jnp.arange(2 * S5_STATE)[:, None]
    dst = jnp.arange(2 * S5_CHUNK)[None, :]
    spread = ((src // S5_STATE == dst // S5_CHUNK) & (src % S5_STATE == dst % S5_STATE)).astype(BF16)
    row_group = (jnp.arange(rows) % LANES) // S5_GROUP_DIM
    col_group = (jnp.arange(2 * S5_CHUNK) % S5_CHUNK) // S5_STATE
    bblk = jnp.dot(f.reshape(nchunk * rows, 2 * S5_STATE), spread, preferred_element_type=BF16)
    bblk = jnp.where(row_group[:, None] == col_group[None, :], bblk.reshape(nchunk, rows, 2 * S5_CHUNK), zero)
    c = jnp.stack([c_re, -c_im]).reshape(2, nchunk, per, S5_GROUP_DIM, S5_STATE).transpose(1, 0, 2, 4, 3).astype(BF16)
    cblk = jnp.where(on_diag[None, None, :, None, :, None], c[:, :, :, :, None, :], zero)
    cblk = cblk.reshape(nchunk, 2 * S5_CHUNK, LANES)
    lr, li = lr.reshape(1, S5_CH), li.reshape(1, S5_CH)
    pr, pi = [lr], [li]
    for _ in range(SUBLANES - 1):
        pr, pi = pr + [pr[-1] * lr - pi[-1] * li], pi + [pr[-1] * li + pi[-1] * lr]
    rowid = jnp.arange(SUBLANES)[:, None]

    def masked(p, d):
        return jnp.where(rowid >= d, jnp.broadcast_to(p[d - 1], (SUBLANES, S5_CH)), 0.0)

    k_prompt = jnp.stack([masked(pr, S5_FOLD), masked(pi, S5_FOLD),
                          jnp.concatenate(pr, axis=0), jnp.concatenate(pi, axis=0)])
    k_sample = jnp.stack([jnp.broadcast_to(lr, (SUBLANES, S5_CH)), jnp.broadcast_to(li, (SUBLANES, S5_CH))])
    return bblk, cblk, k_prompt, k_sample


def _time_major(a):
    a = jnp.swapaxes(a, 0, 1)
    return a.reshape((a.shape[0] * a.shape[1],) + a.shape[2:])


def kernel(x_prompt, x_sample, state_gla, state_pool, state_s5_re, state_s5_im, state_ffn_conv, norm_mix_pre, norm_mix_post, norm_ffn_pre, norm_ffn_post, w_in_even, a_w_s, a_b_s, a_v_norm, b_w_gate, b_gate_bias, b_out_norm, w_out_even, w_in_odd, c_map, c_scale, s5_a_re, s5_a_im, s5_log_dt, s5_b_re, s5_b_im, s5_c_re, s5_c_im, s5_d, s5_w_glu, w_out_odd, ffn_w_up, ffn_conv_w, ffn_conv_b, ffn_w_down):
    bp = x_prompt.shape[0]
    nb, ts = x_sample.shape[0], x_sample.shape[1]
    xp = x_prompt.reshape(bp * SEQ, D_MODEL)
    xs = x_sample.reshape(nb * ts, D_MODEL)

    row = lambda v: v.reshape(1, -1)
    n_main = 2 * A_WIDTH + 2 * B_KEY_WIDTH + 2 * B_WIDTH
    w_in0 = w_in_even[0].astype(BF16)
    w_lr = jnp.pad(w_in0[:, n_main:], ((0, 0), (0, LANES - B_GATE_RANK)))
    w_gate = jnp.pad(b_w_gate[0], ((0, LANES - B_GATE_RANK), (0, 0))).astype(BF16)
    gate = (w_lr, w_gate, row(b_gate_bias[0]))
    pos = jnp.arange(A_BLOCK)
    causal = (pos[None, :] // CHUNK) <= (pos[:, None] // CHUNK)
    ws_prompt = jnp.where(causal[None], a_w_s[0], 0.0).astype(BF16)
    per = A_BLOCK // ts
    ws_small = jnp.where(causal[None, :ts, :ts], a_w_s[0][:, :ts, :ts], 0.0)
    ws_sample = jnp.einsum('hij,ab->haibj', ws_small, jnp.eye(per, dtype=F32)).reshape(A_HEADS, A_BLOCK, A_BLOCK).astype(BF16)
    bs_prompt = a_b_s[0].T
    bs_sample = jnp.tile(a_b_s[0][:, :ts].T, (per, 1))
    cmap = c_map[0].astype(BF16)
    bblk, cblk, k_prompt, k_sample = _s5_constants(s5_a_re[0], s5_a_im[0], s5_log_dt[0], s5_b_re[0], s5_b_im[0],
                                                   s5_c_re[0], s5_c_im[0])
    wglu = s5_w_glu[0].astype(BF16)

    tm = 512
    tf = 512
    tps_p = SEQ // tm
    ffn_hist_p = SUBLANES
    pool_hist_p = 2 * SUBLANES
    step_s = nb
    tps_s = (nb * ts) // tm
    ffn_hist_s = 2 * step_s
    pool_hist_s = (POOL_BUF + 1) * step_s

    tm_ffn = 1024
    tps_ffn_p = SEQ // tm_ffn
    tps_ffn_s = (nb * ts) // tm_ffn

    ffn_w16 = {}

    def ffn_layer(x, layer, state, *, step, hist, tps, convert):
        weights = (ffn_w_up, ffn_w_down) if convert else ffn_w16[layer]
        res = _ffn(x, norm_ffn_pre[:, None], weights, ffn_conv_w, ffn_conv_b[:, None], norm_ffn_post[:, None],
                   state, layer=layer, step=step, hist=hist, tps=tps, tm=tm_ffn, tf=256 if convert else tf,
                   convert=convert)
        if convert:
            ffn_w16[layer] = tuple(res[2:])
        return res[0], res[1]

    proj, lg = _inproj(xs, row(norm_mix_pre[0]), w_in0, gate, n=n_main, tm=1024, tn=1024)
    a_out, a_v = _sgu(proj, row(a_v_norm[0]), ws_sample, bs_sample, nblk=2, emit_av=True)
    b_out, gla_s = _gla(proj, lg, state_gla[0], row(b_out_norm[0]), nseq=nb, t=ts, clen=ts, ngrp=8, chain=False)
    xs, w_out0 = _outproj(a_out, b_out, w_out_even, xs, row(norm_mix_post[0]), tm=tm, convert=True)
    xs = _time_major(xs.reshape(nb, ts, D_MODEL))
    ffn_state = lambda layer: _time_major(state_ffn_conv[layer])[None]
    xs, ffn0_s = ffn_layer(xs, 0, ffn_state(0), step=step_s, hist=ffn_hist_s, tps=tps_ffn_s, convert=True)
    proj, w_in1 = _inproj(xs, row(norm_mix_pre[1]), w_in_odd, n=D_MODEL, tm=1024, tn=512, convert=True)
    pool_state = jnp.pad(_time_major(state_pool[0]), ((step_s, 0), (0, 0)))[None]
    c_out, pool_tail_s = _pool(proj, pool_state, cmap, row(c_scale[0]),
                               step=step_s, hist=pool_hist_s, tps=tps_s, tm=tm, pos0=PAST_LEN)
    d_out, s5re_tail_s, s5im_tail_s = _s5(proj, state_s5_re[0].reshape(1, nb, S5_CH),
                                          state_s5_im[0].reshape(1, nb, S5_CH),
                                          bblk, cblk, row(s5_d[0]), wglu, k_sample, step=step_s, tps=tps_s, tm=tm)
    xs, w_out1 = _outproj(c_out, d_out, w_out_odd, xs, row(norm_mix_post[1]), tm=tm, convert=True)
    xs, ffn1_s = ffn_layer(xs, 1, ffn_state(1), step=step_s, hist=ffn_hist_s, tps=tps_ffn_s, convert=True)

    proj, lg = _inproj(xp, row(norm_mix_pre[0]), w_in0, gate, n=n_main, tm=1024, tn=1280)
    a_out = _sgu(proj, row(a_v_norm[0]), ws_prompt, bs_prompt, nblk=4, emit_av=False)[0]
    b_out, gla_p = _gla(proj, lg, jnp.zeros((bp, B_HEADS, B_KEY_DIM, B_VAL_DIM), F32), row(b_out_norm[0]),
                        nseq=bp, t=SEQ, clen=CHUNK, ngrp=4, chain=True)
    xp = _outproj(a_out, b_out, w_out0, xp, row(norm_mix_post[0]), tm=tm)
    xp, ffn0_p = ffn_layer(xp, 0, jnp.zeros((bp, ffn_hist_p, D_FF), F32), step=1, hist=ffn_hist_p,
                           tps=tps_ffn_p, convert=False)
    proj = _inproj(xp, row(norm_mix_pre[1]), w_in1, n=D_MODEL, tm=1024, tn=D_MODEL)
    c_out, pool_tail_p = _pool(proj, jnp.zeros((bp, pool_hist_p, C_WIDTH), F32), cmap, row(c_scale[0]),
                               step=1, hist=pool_hist_p, tps=tps_p, tm=tm, pos0=0)
    zero_state = jnp.zeros((bp, 1, S5_CH), F32)
    d_out, s5re_tail_p, s5im_tail_p = _s5(proj, zero_state, zero_state, bblk, cblk, row(s5_d[0]), wglu, k_prompt,
                                          step=1, tps=tps_ffn_p, tm=tm_ffn)
    xp = _outproj(c_out, d_out, w_out1, xp, row(norm_mix_post[1]), tm=tm)
    xp, ffn1_p = ffn_layer(xp, 1, jnp.zeros((bp, ffn_hist_p, D_FF), F32), step=1, hist=ffn_hist_p,
                           tps=tps_ffn_p, convert=False)

    last = slice(tps_p - 1, None, tps_p)
    y_prompt = xp.reshape(bp, SEQ, D_MODEL)
    gla_prompt = gla_p[None]
    pool_prompt = pool_tail_p[last, pool_hist_p - POOL_BUF:][None]
    last_ffn = slice(tps_ffn_p - 1, None, tps_ffn_p)
    s5_re_prompt = s5re_tail_p[last_ffn, 0].reshape(1, bp, S5_GROUPS, S5_STATE)
    s5_im_prompt = s5im_tail_p[last_ffn, 0].reshape(1, bp, S5_GROUPS, S5_STATE)
    ffn_prompt = jnp.stack([ffn0_p[last_ffn, ffn_hist_p - 2:], ffn1_p[last_ffn, ffn_hist_p - 2:]])

    def batch_major(a, nt):
        return jnp.swapaxes(a.reshape(nt, nb, a.shape[-1]), 0, 1)

    y_sample = batch_major(xs, ts)
    gla_sample = gla_s[None]
    av_sample = a_v.reshape(1, nb, ts, A_WIDTH)
    pool_sample = batch_major(pool_tail_s[-1, step_s:], POOL_BUF)[None]
    s5_re_sample = s5re_tail_s[-1].reshape(1, nb, S5_GROUPS, S5_STATE)
    s5_im_sample = s5im_tail_s[-1].reshape(1, nb, S5_GROUPS, S5_STATE)
    ffn_sample = jnp.stack([batch_major(ffn0_s[-1], 2), batch_major(ffn1_s[-1], 2)])

    return (y_prompt, y_sample, gla_prompt, gla_sample, av_sample, pool_prompt, pool_sample,
            s5_re_prompt, s5_im_prompt, s5_re_sample, s5_im_sample, ffn_prompt, ffn_sample)
```

```python
import functools
import math

import jax
import jax.numpy as jnp
from jax import lax
from jax.experimental import pallas as pl
from jax.experimental.pallas import tpu as pltpu

F32 = jnp.float32
BF16 = jnp.bfloat16

D_MODEL = 2048
SEQ = 4096
DEC_BATCH = 32
DEC_SEQ = 32
PAST_LEN = 4096
CHUNK = 64
A_WIDTH = 1024
A_HEADS = 8
A_BLOCK = 128
B_HEADS = 4
B_KEY_DIM = 128
B_KEY_WIDTH = 512
B_VAL_DIM = 256
B_WIDTH = 1024
B_GATE_RANK = 16
B_GATE_TAU = 16.0
C_WIDTH = 1024
C_GROUP_DIM = 256
POOL_WINDOWS = (2, 4, 8, 16)
POOL_BUF = 15
D_WIDTH = 1024
S5_GROUPS = 64
S5_GROUP_DIM = 16
S5_STATE = 64
S5_CH = S5_GROUPS * S5_STATE
D_FF = 5632
EPS = 1e-6

LANES = 128
SUBLANES = 8
VMEM_LIMIT = 56 * 1024 * 1024


def _params(sem):
    return pltpu.CompilerParams(dimension_semantics=sem, vmem_limit_bytes=VMEM_LIMIT)


def _rms(x, g):
    return x * lax.rsqrt(jnp.mean(x * x, axis=-1, keepdims=True) + EPS) * g


def _dot(a, b):
    return jnp.dot(a, b, preferred_element_type=F32)


def _inproj_kernel(x_ref, g_ref, w_ref, *rest, with_gate, convert):
    rest = list(rest)
    if with_gate:
        wlr_ref, wgate_ref, gbias_ref = rest[:3]
        rest = rest[3:]
    o_ref = rest.pop(0)
    if with_gate:
        lg_ref = rest.pop(0)
    if convert:
        w16_ref = rest.pop(0)
    h_ref, = rest
    j = pl.program_id(1)

    def project(hb):
        w = w_ref[...]
        if convert:
            w = w.astype(BF16)
            w16_ref[...] = w
        o_ref[...] = _dot(hb, w)

    @pl.when(j == 0)
    def _():
        hb = _rms(x_ref[...], g_ref[...]).astype(BF16)
        h_ref[...] = hb
        project(hb)
        if with_gate:
            glr = _dot(hb, wlr_ref[...])
            z = _dot(glr.astype(BF16), wgate_ref[...]) + gbias_ref[...]
            lg_ref[...] = (jnp.minimum(z, 0.0) - jnp.log(1.0 + jnp.exp(-jnp.abs(z)))) * (1.0 / B_GATE_TAU)

    @pl.when(j > 0)
    def _():
        project(h_ref[...])


def _inproj(x, g, w, gate=None, *, n, tm, tn, convert=False):
    m, d = x.shape
    grid = (m // tm, n // tn)
    one_tile = m == tm
    in_specs = [pl.BlockSpec((tm, d), lambda i, j: (i, 0), pipeline_mode=pl.Buffered(1) if one_tile else None),
                pl.BlockSpec((1, d), lambda i, j: (0, 0)),
                pl.BlockSpec((None, d, tn), lambda i, j: (0, 0, j)) if convert
                else pl.BlockSpec((d, tn), lambda i, j: (0, j), pipeline_mode=pl.Buffered(1) if n == tn else None)]
    out_shape = [jax.ShapeDtypeStruct((m, n), F32)]
    out_specs = [pl.BlockSpec((tm, tn), lambda i, j: (i, j))]
    args = [x, g, w]
    if gate is not None:
        wlr, wgate, gbias = gate
        in_specs += [pl.BlockSpec(wlr.shape, lambda i, j: (0, 0)),
                     pl.BlockSpec(wgate.shape, lambda i, j: (0, 0)),
                     pl.BlockSpec(gbias.shape, lambda i, j: (0, 0))]
        out_shape.append(jax.ShapeDtypeStruct((m, B_KEY_WIDTH), F32))
        out_specs.append(pl.BlockSpec((tm, B_KEY_WIDTH), lambda i, j: (i, 0)))
        args += [wlr, wgate, gbias]
    if convert:
        out_shape.append(jax.ShapeDtypeStruct((d, n), BF16))
        out_specs.append(pl.BlockSpec((d, tn), lambda i, j: (0, j)))
    res = pl.pallas_call(
        functools.partial(_inproj_kernel, with_gate=gate is not None, convert=convert),
        out_shape=out_shape, grid=grid, in_specs=in_specs, out_specs=out_specs,
        scratch_shapes=[pltpu.VMEM((tm, d), BF16)],
        compiler_params=_params(("parallel", "arbitrary")),
        name=("inproj_gate" if gate is not None else "inproj") + ("_convert" if convert else ""),
    )(*args)
    return res if len(res) > 1 else res[0]


def _sgu_kernel(u_ref, v_ref, gain_ref, w_ref, b_ref, o_ref, *av_ref, nblk):
    for n in range(nblk):
        rows = slice(n * A_BLOCK, (n + 1) * A_BLOCK)
        v = jax.nn.gelu(v_ref[rows, :])
        mu = jnp.mean(v, axis=-1, keepdims=True)
        vc = v - mu
        vn = vc * lax.rsqrt(jnp.mean(vc * vc, axis=-1, keepdims=True) + EPS) * gain_ref[...]
        if av_ref:
            av_ref[0][rows, :] = vn
        vb = vn.astype(BF16)
        for h in range(A_HEADS):
            cols = slice(h * LANES, (h + 1) * LANES)
            s = _dot(w_ref[h], vb[:, cols]) + b_ref[:, h:h + 1]
            o_ref[rows, cols] = (jax.nn.gelu(u_ref[rows, cols]) * s).astype(o_ref.dtype)


def _sgu(proj, gain, w, b, *, nblk, emit_av):
    m = proj.shape[0]
    tm = nblk * A_BLOCK
    n_out = 2 if emit_av else 1
    return pl.pallas_call(
        functools.partial(_sgu_kernel, nblk=nblk),
        out_shape=[jax.ShapeDtypeStruct((m, A_WIDTH), BF16), jax.ShapeDtypeStruct((m, A_WIDTH), F32)][:n_out],
        grid=(m // tm,),
        in_specs=[pl.BlockSpec((tm, A_WIDTH), lambda i: (i, 0)),
                  pl.BlockSpec((tm, A_WIDTH), lambda i: (i, 1)),
                  pl.BlockSpec((1, A_WIDTH), lambda i: (0, 0)),
                  pl.BlockSpec(w.shape, lambda i: (0, 0, 0)),
                  pl.BlockSpec(b.shape, lambda i: (0, 0))],
        out_specs=[pl.BlockSpec((tm, A_WIDTH), lambda i: (i, 0)),
                   pl.BlockSpec((tm, A_WIDTH), lambda i: (i, 0))][:n_out],
        compiler_params=_params(("parallel",)),
        name="sgu",
    )(proj, proj, gain, w, b)


def _gla_kernel(q_ref, k_ref, v_ref, r_ref, lg_ref, s0_ref, og_ref, o_ref, sout_ref, st_ref, *, clen, ngrp, chain):
    c = pl.program_id(1)
    rows_all = ngrp * clen
    shift = int(math.log2(clen))

    if chain:
        @pl.when(c == 0)
        def _():
            for h in range(B_HEADS):
                st_ref[h] = s0_ref[0, h].T

    row_i = lax.broadcasted_iota(jnp.int32, (rows_all, rows_all), 0)
    col_i = lax.broadcasted_iota(jnp.int32, (rows_all, rows_all), 1)
    same_group = lax.shift_right_logical(row_i, shift) == lax.shift_right_logical(col_i, shift)
    tri = ((row_i >= col_i) & same_group).astype(F32)
    cum = jnp.dot(tri, lg_ref[...], precision=lax.Precision.HIGHEST,
                  preferred_element_type=F32)
    tots = [cum[(g + 1) * clen - 1:(g + 1) * clen, :] for g in range(ngrp)]
    tot_rows = jnp.concatenate([jnp.broadcast_to(t, (clen, B_KEY_WIDTH)) for t in tots], axis=0)
    kd = (k_ref[...] * jnp.exp(tot_rows - cum)).astype(BF16)
    qs = (q_ref[...] * (B_KEY_DIM ** -0.5)).astype(BF16)
    vb = v_ref[...].astype(BF16)
    sr = jax.nn.silu(r_ref[...])
    grp = lax.shift_right_logical(lax.broadcasted_iota(jnp.int32, (rows_all, 1), 0), shift)
    zero = jnp.zeros((), BF16)

    def by_group(x):
        return jnp.concatenate([jnp.where(grp == g, x, zero) for g in range(ngrp)], axis=1)

    for h in range(B_HEADS):
        kc = slice(h * B_KEY_DIM, (h + 1) * B_KEY_DIM)
        vc = slice(h * B_VAL_DIM, (h + 1) * B_VAL_DIM)
        upd = lax.dot_general(vb[:, vc], by_group(kd[:, kc]), (((0,), (0,)), ((), ())),
                              preferred_element_type=F32)
        states = []
        st = st_ref[h] if chain else None
        for g in range(ngrp):
            if not chain:
                st = s0_ref[g, h].T
            st = jnp.exp(tots[g][:, kc]) * st + upd[:, g * B_KEY_DIM:(g + 1) * B_KEY_DIM]
            states.append(st.astype(BF16))
            if not chain:
                sout_ref[g, h] = st.T
        if chain:
            st_ref[h] = st
        o = lax.dot_general(by_group(qs[:, kc]), jnp.concatenate(states, axis=1), (((1,), (1,)), ((), ())),
                            preferred_element_type=F32)
        o = o * lax.rsqrt(jnp.mean(o * o, axis=-1, keepdims=True) + EPS)
        o = o * og_ref[:, vc] * sr[:, vc]
        o_ref[:, vc] = o.astype(o_ref.dtype)

    if chain:
        @pl.when(c == pl.num_programs(1) - 1)
        def _():
            for h in range(B_HEADS):
                sout_ref[0, h] = st_ref[h].T


def _gla(proj, lg, s0, og, *, nseq, t, clen, ngrp, chain):
    rows = clen * ngrp
    m = nseq * t
    if chain:
        steps = t // rows
        grid = (nseq, steps)
        rmap = lambda b, c: b * steps + c
        nstate = 1
    else:
        grid = (m // rows, 1)
        rmap = lambda b, c: b
        nstate = ngrp
    blk = lambda width, col: pl.BlockSpec((rows, width), lambda b, c: (rmap(b, c), col))
    state_spec = pl.BlockSpec((nstate, B_HEADS, B_KEY_DIM, B_VAL_DIM), lambda b, c: (b, 0, 0, 0))
    return pl.pallas_call(
        functools.partial(_gla_kernel, clen=clen, ngrp=ngrp, chain=chain),
        out_shape=[jax.ShapeDtypeStruct((m, B_WIDTH), BF16),
                   jax.ShapeDtypeStruct((nseq, B_HEADS, B_KEY_DIM, B_VAL_DIM), F32)],
        grid=grid,
        in_specs=[blk(B_KEY_WIDTH, 4),
                  blk(B_KEY_WIDTH, 5),
                  blk(B_WIDTH, 3),
                  blk(B_WIDTH, 4),
                  blk(B_KEY_WIDTH, 0),
                  state_spec,
                  pl.BlockSpec((1, B_WIDTH), lambda b, c: (0, 0))],
        out_specs=[blk(B_WIDTH, 0), state_spec],
        scratch_shapes=[pltpu.VMEM((B_HEADS, B_VAL_DIM, B_KEY_DIM), F32)],
        compiler_params=_params(("parallel", "arbitrary")),
        name="gla",
    )(proj, proj, proj, proj, lg, s0, og)


def _outproj_kernel(a_ref, b_ref, w_ref, x_ref, g_ref, o_ref, *w16_ref, convert):
    wb_ref = w_ref
    if convert:
        wb_ref, = w16_ref

        @pl.when(pl.program_id(0) == 0)
        def _():
            wb_ref[...] = w_ref[...].astype(BF16)

    ka = a_ref.shape[1]
    y = _dot(a_ref[...], wb_ref[0:ka, :]) + _dot(b_ref[...], wb_ref[ka:, :])
    o_ref[...] = x_ref[...] + _rms(y, g_ref[...])


def _outproj(a, b, w, x, g, *, tm, convert=False):
    m, d = x.shape
    k = a.shape[1] + b.shape[1]
    once = pl.Buffered(1)
    out_shape = [jax.ShapeDtypeStruct((m, d), F32)]
    out_specs = [pl.BlockSpec((tm, d), lambda i: (i, 0))]
    if convert:
        w_spec = pl.BlockSpec((None, k, d), lambda i: (0, 0, 0), pipeline_mode=once)
        out_shape.append(jax.ShapeDtypeStruct((k, d), BF16))
        out_specs.append(pl.BlockSpec((k, d), lambda i: (0, 0), pipeline_mode=once))
    else:
        w_spec = pl.BlockSpec((k, d), lambda i: (0, 0), pipeline_mode=once)
    res = pl.pallas_call(
        functools.partial(_outproj_kernel, convert=convert),
        out_shape=out_shape,
        grid=(m // tm,),
        in_specs=[pl.BlockSpec((tm, a.shape[1]), lambda i: (i, 0)),
                  pl.BlockSpec((tm, b.shape[1]), lambda i: (i, 0)),
                  w_spec,
                  pl.BlockSpec((tm, d), lambda i: (i, 0)),
                  pl.BlockSpec((1, d), lambda i: (0, 0))],
        out_specs=out_specs,
        compiler_params=_params(("arbitrary",)),
        name="outproj_convert" if convert else "outproj",
    )(a, b, w, x, g)
    return res if convert else res[0]


def _ffn_kernel(x_ref, gpre_ref, wg_ref, wv_ref, cw_ref, cb_ref, wd_ref, gpost_ref, st_ref,
                o_ref, tail_ref, *rest, step, hist, tps, tm, nf, convert):
    if convert:
        wg_out, wv_out, wd_out, h_ref, gext_ref, carry_ref, act_a, act_b = rest
    else:
        h_ref, gext_ref, carry_ref, act_a, act_b = rest
    i = pl.program_id(0)
    j = pl.program_id(1)
    first = (i % tps) == 0

    def up_and_gate(act_ref, hb=None):
        if hb is None:
            hb = h_ref[...]
        wg, wv = wg_ref[...], wv_ref[...]
        if convert:
            wg, wv = wg.astype(BF16), wv.astype(BF16)
            wg_out[...] = wg
            wv_out[...] = wv
        gate = _dot(hb, wg)
        val = _dot(hb, wv)
        cw = cw_ref[j]
        gext_ref[0:hist, :] = jnp.where(first, st_ref[0, j], carry_ref[j])
        gext_ref[hist:hist + tm, :] = gate
        prev2 = gext_ref[hist - 2 * step:hist - 2 * step + tm, :]
        prev1 = gext_ref[hist - step:hist - step + tm, :]
        conv = cb_ref[j] + cw[0:1, :] * prev2 + cw[1:2, :] * prev1 + cw[2:3, :] * gate
        act_ref[...] = (jax.nn.gelu(conv) * val).astype(BF16)
        tail = gate[tm - hist:, :]
        carry_ref[j] = tail
        tail_ref[0, j] = tail

    def down(act_ref):
        wd = wd_ref[...]
        if convert:
            wd = wd.astype(BF16)
            wd_out[...] = wd
        o_ref[...] += _dot(act_ref[...], wd)

    @pl.when(j == 0)
    def _():
        @pl.when(i == 0)
        def _():
            carry_ref[...] = jnp.zeros_like(carry_ref)

        o_ref[...] = jnp.zeros_like(o_ref)
        hb = _rms(x_ref[...], gpre_ref[...]).astype(BF16)
        h_ref[...] = hb
        up_and_gate(act_a, hb)

    for parity, (src, dst) in enumerate(((act_b, act_a), (act_a, act_b))):
        @pl.when((j > 0) & (j < nf) & (j % 2 == parity))
        def _(src=src, dst=dst):
            down(src)
            up_and_gate(dst)

    @pl.when(j == nf)
    def _():
        down(act_a if (nf - 1) % 2 == 0 else act_b)
        o_ref[...] = x_ref[...] + _rms(o_ref[...], gpost_ref[...])


def _ffn(x, gpre, weights, cw, cb, gpost, state, *, layer, step, hist, tps, tm, tf, convert):
    m, d = x.shape
    nf = D_FF // tf
    nm = m // tm
    up = lambda j: jnp.minimum(j, nf - 1)
    down = lambda j: jnp.maximum(j - 1, 0)

    def by_tile(a):
        a = a.reshape(a.shape[:-1] + (nf, tf))
        return jnp.swapaxes(a, -2, -3)

    cw, cb, state = by_tile(cw), by_tile(cb), by_tile(state)
    if convert:
        w_up, w_down = weights
        w_args = (w_up, w_up, w_down)
        w_specs = [pl.BlockSpec((None, d, tf), lambda i, j: (layer, 0, up(j))),
                   pl.BlockSpec((None, d, tf), lambda i, j: (layer, 0, nf + up(j))),
                   pl.BlockSpec((None, tf, d), lambda i, j: (layer, down(j), 0))]
        extra_shapes = [jax.ShapeDtypeStruct((d, D_FF), BF16), jax.ShapeDtypeStruct((d, D_FF), BF16),
                        jax.ShapeDtypeStruct((D_FF, d), BF16)]
        extra_specs = [pl.BlockSpec((d, tf), lambda i, j: (0, up(j))),
                       pl.BlockSpec((d, tf), lambda i, j: (0, up(j))),
                       pl.BlockSpec((tf, d), lambda i, j: (down(j), 0))]
    else:
        w_args = weights
        w_specs = [pl.BlockSpec((d, tf), lambda i, j: (0, up(j))),
                   pl.BlockSpec((d, tf), lambda i, j: (0, up(j))),
                   pl.BlockSpec((tf, d), lambda i, j: (down(j), 0))]
        extra_shapes, extra_specs = [], []
    res = pl.pallas_call(
        functools.partial(_ffn_kernel, step=step, hist=hist, tps=tps, tm=tm, nf=nf, convert=convert),
        out_shape=[jax.ShapeDtypeStruct((m, d), F32), jax.ShapeDtypeStruct((nm, nf, hist, tf), F32)] + extra_shapes,
        grid=(nm, nf + 1),
        in_specs=[pl.BlockSpec((tm, d), lambda i, j: (i, 0)),
                  pl.BlockSpec((None, 1, d), lambda i, j: (layer, 0, 0)),
                  w_specs[0], w_specs[1],
                  pl.BlockSpec((None, nf, 3, tf), lambda i, j: (layer, 0, 0, 0)),
                  pl.BlockSpec((None, nf, 1, tf), lambda i, j: (layer, 0, 0, 0)),
                  w_specs[2],
                  pl.BlockSpec((None, 1, d), lambda i, j: (layer, 0, 0)),
                  pl.BlockSpec((1, nf, hist, tf), lambda i, j: (i // tps, 0, 0, 0))],
        out_specs=[pl.BlockSpec((tm, d), lambda i, j: (i, 0), pipeline_mode=pl.Buffered(1)),
                   pl.BlockSpec((1, nf, hist, tf), lambda i, j: (i, 0, 0, 0))] + extra_specs,
        scratch_shapes=[pltpu.VMEM((tm, d), BF16),
                        pltpu.VMEM((hist + tm, tf), F32), pltpu.VMEM((nf, hist, tf), F32),
                        pltpu.VMEM((tm, tf), BF16), pltpu.VMEM((tm, tf), BF16)],
        compiler_params=_params(("arbitrary", "arbitrary")),
        name="ffn_convert" if convert else "ffn",
    )(x, gpre, w_args[0], w_args[1], cw, cb, w_args[2], gpost, state)
    res = list(res)
    res[1] = jnp.swapaxes(res[1], 1, 2).reshape(nm, hist, D_FF)
    return res


def _pool_kernel(c_ref, st_ref, cmap_ref, cs_ref, o_ref, tail_ref, ext_ref, carry_ref,
                 *, step, hist, tps, tm, pos0):
    i = pl.program_id(0)
    first = (i % tps) == 0

    @pl.when(first)
    def _():
        ext_ref[0:hist, :] = st_ref[0]

    @pl.when(jnp.logical_not(first))
    def _():
        ext_ref[0:hist, :] = carry_ref[...]

    ext_ref[hist:hist + tm, :] = c_ref[...]
    row = lax.broadcasted_iota(jnp.int32, (tm, 1), 0)
    if step > 1:
        row = lax.shift_right_logical(row, int(math.log2(step)))
    pos = pos0 + (i % tps) * (tm // step) + row
    for g, win in enumerate(POOL_WINDOWS):
        cols = slice(g * C_GROUP_DIM, (g + 1) * C_GROUP_DIM)
        acc = ext_ref[:, cols]
        d = 1
        while d < win:
            acc = acc + pltpu.roll(acc, d * step, 0)
            d *= 2
        tot = acc[hist:, :]
        cur = ext_ref[hist:hist + tm, cols]
        cnt = jnp.minimum(pos + 1, win).astype(F32)
        delta = tot / cnt - cur
        y = _dot(delta.astype(BF16), cmap_ref[g]) * cs_ref[:, cols]
        o_ref[:, cols] = y.astype(o_ref.dtype)
    tail = ext_ref[tm:tm + hist, :]
    carry_ref[...] = tail
    tail_ref[0] = tail


def _pool(proj, state, cmap, cscale, *, step, hist, tps, tm, pos0):
    m = proj.shape[0]
    nm = m // tm
    assert all(w & (w - 1) == 0 for w in POOL_WINDOWS) and hist >= (max(POOL_WINDOWS) - 1) * step and tm >= hist
    return pl.pallas_call(
        functools.partial(_pool_kernel, step=step, hist=hist, tps=tps, tm=tm, pos0=pos0),
        out_shape=[jax.ShapeDtypeStruct((m, C_WIDTH), BF16), jax.ShapeDtypeStruct((nm, hist, C_WIDTH), F32)],
        grid=(nm,),
        in_specs=[pl.BlockSpec((tm, C_WIDTH), lambda i: (i, 0)),
                  pl.BlockSpec((1, hist, C_WIDTH), lambda i: (i // tps, 0, 0)),
                  pl.BlockSpec(cmap.shape, lambda i: (0, 0, 0)),
                  pl.BlockSpec((1, C_WIDTH), lambda i: (0, 0))],
        out_specs=[pl.BlockSpec((tm, C_WIDTH), lambda i: (i, 0)),
                   pl.BlockSpec((1, hist, C_WIDTH), lambda i: (i, 0, 0))],
        scratch_shapes=[pltpu.VMEM((hist + tm, C_WIDTH), F32), pltpu.VMEM((hist, C_WIDTH), F32)],
        compiler_params=_params(("arbitrary",)),
        name="pool",
    )(proj, state, cmap, cscale)


S5_CHUNK = 512
S5_NCHUNK = S5_CH // S5_CHUNK
S5_FOLD = SUBLANES // 2


def _s5_kernel(u_ref, s0re_ref, s0im_ref, bblk_ref, cblk_ref, dskip_ref, wglu_ref, kc_ref,
               o_ref, tre_ref, tim_ref, cre_ref, cim_ref, *, step, tps, tm):
    i = pl.program_id(0)
    crow = cre_ref.shape[0]

    @pl.when((i % tps) == 0)
    def _():
        cre_ref[...] = jnp.broadcast_to(s0re_ref[0], cre_ref.shape) if step == 1 else s0re_ref[0]
        cim_ref[...] = jnp.broadcast_to(s0im_ref[0], cim_ref.shape) if step == 1 else s0im_ref[0]

    u = u_ref[...]
    lhs = [u.astype(BF16)]
    if step == 1:
        row_in_block = lax.broadcasted_iota(jnp.int32, (tm, 1), 0) & (SUBLANES - 1)
        for k in range(1, S5_FOLD):
            lhs.append(jnp.where(row_in_block >= k, pltpu.roll(u, k, 0), 0.0).astype(BF16))
    ys = []
    for m in range(S5_NCHUNK):
        cols = slice(m * S5_CHUNK, (m + 1) * S5_CHUNK)
        ucols = slice(m * LANES, (m + 1) * LANES)
        if step == 1:
            r = _dot(jnp.concatenate([x[:, ucols] for x in lhs], axis=1), bblk_ref[m])
        else:
            r = _dot(lhs[0][:, ucols], bblk_ref[m, 0:LANES, :])
        cr, ci = cre_ref[:, cols], cim_ref[:, cols]
        sre, sim = [], []
        if step == 1:
            ar, ai, pwr, pwi = [kc_ref[k, :, cols] for k in range(4)]
            for rb in range(tm // SUBLANES):
                rows = slice(rb * SUBLANES, (rb + 1) * SUBLANES)
                xr = r[rows, :S5_CHUNK]
                xi = r[rows, S5_CHUNK:]
                sr = pltpu.roll(xr, S5_FOLD, 0)
                si = pltpu.roll(xi, S5_FOLD, 0)
                xr, xi = xr + (ar * sr - ai * si), xi + (ar * si + ai * sr)
                xr, xi = xr + (pwr * cr - pwi * ci), xi + (pwr * ci + pwi * cr)
                sre.append(xr)
                sim.append(xi)
                cr = jnp.broadcast_to(xr[SUBLANES - 1:SUBLANES, :], xr.shape)
                ci = jnp.broadcast_to(xi[SUBLANES - 1:SUBLANES, :], xi.shape)
        else:
            lr = jnp.broadcast_to(kc_ref[0, 0:1, cols], (crow, S5_CHUNK))
            li = jnp.broadcast_to(kc_ref[1, 0:1, cols], (crow, S5_CHUNK))
            for t in range(tm // step):
                rows = slice(t * step, (t + 1) * step)
                cr, ci = (r[rows, :S5_CHUNK] + (lr * cr - li * ci),
                          r[rows, S5_CHUNK:] + (lr * ci + li * cr))
                sre.append(cr)
                sim.append(ci)
        cre_ref[:, cols] = cr
        cim_ref[:, cols] = ci
        ys.append(_dot(jnp.concatenate(sre, axis=0).astype(BF16), cblk_ref[m, 0:S5_CHUNK, :])
                  + _dot(jnp.concatenate(sim, axis=0).astype(BF16), cblk_ref[m, S5_CHUNK:, :]))

    tre_ref[0] = cre_ref[...]
    tim_ref[0] = cim_ref[...]

    y = jnp.concatenate(ys, axis=1) + dskip_ref[...] * u
    z = _dot(jax.nn.gelu(y).astype(BF16), wglu_ref[...])
    o_ref[...] = (z[:, :D_WIDTH] * jax.nn.sigmoid(z[:, D_WIDTH:])).astype(o_ref.dtype)


def _s5(proj, s0re, s0im, bblk, cblk, dskip, wglu, kconst, *, step, tps, tm):
    m = proj.shape[0]
    nm = m // tm
    crow = s0re.shape[1] if step > 1 else SUBLANES
    srow = s0re.shape[1]
    return pl.pallas_call(
        functools.partial(_s5_kernel, step=step, tps=tps, tm=tm),
        out_shape=[jax.ShapeDtypeStruct((m, D_WIDTH), BF16),
                   jax.ShapeDtypeStruct((nm, crow, S5_CH), F32),
                   jax.ShapeDtypeStruct((nm, crow, S5_CH), F32)],
        grid=(nm,),
        in_specs=[pl.BlockSpec((tm, D_WIDTH), lambda i: (i, 1)),
                  pl.BlockSpec((1, srow, S5_CH), lambda i: (i // tps, 0, 0)),
                  pl.BlockSpec((1, srow, S5_CH), lambda i: (i // tps, 0, 0)),
                  pl.BlockSpec(bblk.shape, lambda i: (0, 0, 0), pipeline_mode=pl.Buffered(1)),
                  pl.BlockSpec(cblk.shape, lambda i: (0, 0, 0), pipeline_mode=pl.Buffered(1)),
                  pl.BlockSpec((1, D_WIDTH), lambda i: (0, 0)),
                  pl.BlockSpec(wglu.shape, lambda i: (0, 0), pipeline_mode=pl.Buffered(1)),
                  pl.BlockSpec(kconst.shape, lambda i: (0, 0, 0), pipeline_mode=pl.Buffered(1))],
        out_specs=[pl.BlockSpec((tm, D_WIDTH), lambda i: (i, 0)),
                   pl.BlockSpec((1, crow, S5_CH), lambda i: (i, 0, 0)),
                   pl.BlockSpec((1, crow, S5_CH), lambda i: (i, 0, 0))],
        scratch_shapes=[pltpu.VMEM((crow, S5_CH), F32), pltpu.VMEM((crow, S5_CH), F32)],
        compiler_params=_params(("arbitrary",)),
        name="s5",
    )(proj, s0re, s0im, bblk, cblk, dskip, wglu, kconst)


def _s5_constants(a_re, a_im, log_dt, b_re, b_im, c_re, c_im):
    per = S5_CHUNK // S5_STATE
    nchunk = S5_GROUPS // per
    on_diag = jnp.arange(per)[:, None] == jnp.arange(per)[None, :]
    zero = jnp.zeros((), BF16)
    dt = jnp.exp(log_dt)[:, None]
    zr, zi = a_re * dt, a_im * dt
    mag = jnp.exp(zr)
    lr, li = mag * jnp.cos(zi), mag * jnp.sin(zi)
    den = a_re * a_re + a_im * a_im
    nr, ni = lr - 1.0, li
    kr, ki = (nr * a_re + ni * a_im) / den, (ni * a_re - nr * a_im) / den
    bbr = kr[..., None] * b_re - ki[..., None] * b_im
    bbi = kr[..., None] * b_im + ki[..., None] * b_re
    qr, qi = [jnp.ones_like(lr)], [jnp.zeros_like(li)]
    for _ in range(S5_FOLD - 1):
        qr, qi = qr + [qr[-1] * lr - qi[-1] * li], qi + [qr[-1] * li + qi[-1] * lr]
    qr, qi = jnp.stack(qr)[..., None], jnp.stack(qi)[..., None]
    f = jnp.stack([qr * bbr - qi * bbi, qr * bbi + qi * bbr])
    f = f.reshape(2, S5_FOLD, nchunk, per, S5_STATE, S5_GROUP_DIM).transpose(2, 1, 3, 5, 0, 4).astype(BF16)
    f = f.reshape(nchunk, S5_FOLD * LANES, 2 * S5_STATE)
    rows = S5_FOLD * LANES
    src = jnp.arange(2 * S5_STATE)[:, None]
    dst = jnp.arange(2 * S5_CHUNK)[None, :]
    spread = ((src // S5_STATE == dst // S5_CHUNK) & (src % S5_STATE == dst % S5_STATE)).astype(BF16)
    row_group = (jnp.arange(rows) % LANES) // S5_GROUP_DIM
    col_group = (jnp.arange(2 * S5_CHUNK) % S5_CHUNK) // S5_STATE
    bblk = jnp.dot(f.reshape(nchunk * rows, 2 * S5_STATE), spread, preferred_element_type=BF16)
    bblk = jnp.where(row_group[:, None] == col_group[None, :], bblk.reshape(nchunk, rows, 2 * S5_CHUNK), zero)
    c = jnp.stack([c_re, -c_im]).reshape(2, nchunk, per, S5_GROUP_DIM, S5_STATE).transpose(1, 0, 2, 4, 3).astype(BF16)
    cblk = jnp.where(on_diag[None, None, :, None, :, None], c[:, :, :, :, None, :], zero)
    cblk = cblk.reshape(nchunk, 2 * S5_CHUNK, LANES)
    lr, li = lr.reshape(1, S5_CH), li.reshape(1, S5_CH)
    pr, pi = [lr], [li]
    for _ in range(SUBLANES - 1):
        pr, pi = pr + [pr[-1] * lr - pi[-1] * li], pi + [pr[-1] * li + pi[-1] * lr]
    rowid = jnp.arange(SUBLANES)[:, None]

    def masked(p, d):
        return jnp.where(rowid >= d, jnp.broadcast_to(p[d - 1], (SUBLANES, S5_CH)), 0.0)

    k_prompt = jnp.stack([masked(pr, S5_FOLD), masked(pi, S5_FOLD),
                          jnp.concatenate(pr, axis=0), jnp.concatenate(pi, axis=0)])
    k_sample = jnp.stack([jnp.broadcast_to(lr, (SUBLANES, S5_CH)), jnp.broadcast_to(li, (SUBLANES, S5_CH))])
    return bblk, cblk, k_prompt, k_sample


def _time_major(a):
    a = jnp.swapaxes(a, 0, 1)
    return a.reshape((a.shape[0] * a.shape[1],) + a.shape[2:])


def kernel(x_prompt, x_sample, state_gla, state_pool, state_s5_re, state_s5_im, state_ffn_conv, norm_mix_pre, norm_mix_post, norm_ffn_pre, norm_ffn_post, w_in_even, a_w_s, a_b_s, a_v_norm, b_w_gate, b_gate_bias, b_out_norm, w_out_even, w_in_odd, c_map, c_scale, s5_a_re, s5_a_im, s5_log_dt, s5_b_re, s5_b_im, s5_c_re, s5_c_im, s5_d, s5_w_glu, w_out_odd, ffn_w_up, ffn_conv_w, ffn_conv_b, ffn_w_down):
    bp = x_prompt.shape[0]
    nb, ts = x_sample.shape[0], x_sample.shape[1]
    xp = x_prompt.reshape(bp * SEQ, D_MODEL)
    xs = x_sample.reshape(nb * ts, D_MODEL)

    row = lambda v: v.reshape(1, -1)
    n_main = 2 * A_WIDTH + 2 * B_KEY_WIDTH + 2 * B_WIDTH
    w_in0 = w_in_even[0].astype(BF16)
    w_lr = jnp.pad(w_in0[:, n_main:], ((0, 0), (0, LANES - B_GATE_RANK)))
    w_gate = jnp.pad(b_w_gate[0], ((0, LANES - B_GATE_RANK), (0, 0))).astype(BF16)
    gate = (w_lr, w_gate, row(b_gate_bias[0]))
    pos = jnp.arange(A_BLOCK)
    causal = (pos[None, :] // CHUNK) <= (pos[:, None] // CHUNK)
    ws_prompt = jnp.where(causal[None], a_w_s[0], 0.0).astype(BF16)
    per = A_BLOCK // ts
    ws_small = jnp.where(causal[None, :ts, :ts], a_w_s[0][:, :ts, :ts], 0.0)
    ws_sample = jnp.einsum('hij,ab->haibj', ws_small, jnp.eye(per, dtype=F32)).reshape(A_HEADS, A_BLOCK, A_BLOCK).astype(BF16)
    bs_prompt = a_b_s[0].T
    bs_sample = jnp.tile(a_b_s[0][:, :ts].T, (per, 1))
    cmap = c_map[0].astype(BF16)
    bblk, cblk, k_prompt, k_sample = _s5_constants(s5_a_re[0], s5_a_im[0], s5_log_dt[0], s5_b_re[0], s5_b_im[0],
                                                   s5_c_re[0], s5_c_im[0])
    wglu = s5_w_glu[0].astype(BF16)

    tm = 512
    tf = 512
    tps_p = SEQ // tm
    ffn_hist_p = SUBLANES
    pool_hist_p = 2 * SUBLANES
    step_s = nb
    tps_s = (nb * ts) // tm
    ffn_hist_s = 2 * step_s
    pool_hist_s = (POOL_BUF + 1) * step_s

    tm_ffn = 1024
    tps_ffn_p = SEQ // tm_ffn
    tps_ffn_s = (nb * ts) // tm_ffn

    ffn_w16 = {}

    def ffn_layer(x, layer, state, *, step, hist, tps, convert):
        weights = (ffn_w_up, ffn_w_down) if convert else ffn_w16[layer]
        res = _ffn(x, norm_ffn_pre[:, None], weights, ffn_conv_w, ffn_conv_b[:, None], norm_ffn_post[:, None],
                   state, layer=layer, step=step, hist=hist, tps=tps, tm=tm_ffn, tf=256 if convert else tf,
                   convert=convert)
        if convert:
            ffn_w16[layer] = tuple(res[2:])
        return res[0], res[1]

    proj, lg = _inproj(xs, row(norm_mix_pre[0]), w_in0, gate, n=n_main, tm=1024, tn=1024)
    a_out, a_v = _sgu(proj, row(a_v_norm[0]), ws_sample, bs_sample, nblk=2, emit_av=True)
    b_out, gla_s = _gla(proj, lg, state_gla[0], row(b_out_norm[0]), nseq=nb, t=ts, clen=ts, ngrp=8, chain=False)
    xs, w_out0 = _outproj(a_out, b_out, w_out_even, xs, row(norm_mix_post[0]), tm=tm, convert=True)
    xs = _time_major(xs.reshape(nb, ts, D_MODEL))
    ffn_state = lambda layer: _time_major(state_ffn_conv[layer])[None]
    xs, ffn0_s = ffn_layer(xs, 0, ffn_state(0), step=step_s, hist=ffn_hist_s, tps=tps_ffn_s, convert=True)
    proj, w_in1 = _inproj(xs, row(norm_mix_pre[1]), w_in_odd, n=D_MODEL, tm=1024, tn=512, convert=True)
    pool_state = jnp.pad(_time_major(state_pool[0]), ((step_s, 0), (0, 0)))[None]
    c_out, pool_tail_s = _pool(proj, pool_state, cmap, row(c_scale[0]),
                               step=step_s, hist=pool_hist_s, tps=tps_s, tm=tm, pos0=PAST_LEN)
    d_out, s5re_tail_s, s5im_tail_s = _s5(proj, state_s5_re[0].reshape(1, nb, S5_CH),
                                          state_s5_im[0].reshape(1, nb, S5_CH),
                                          bblk, cblk, row(s5_d[0]), wglu, k_sample, step=step_s, tps=tps_s, tm=tm)
    xs, w_out1 = _outproj(c_out, d_out, w_out_odd, xs, row(norm_mix_post[1]), tm=tm, convert=True)
    xs, ffn1_s = ffn_layer(xs, 1, ffn_state(1), step=step_s, hist=ffn_hist_s, tps=tps_ffn_s, convert=True)

    proj, lg = _inproj(xp, row(norm_mix_pre[0]), w_in0, gate, n=n_main, tm=1024, tn=1280)
    a_out = _sgu(proj, row(a_v_norm[0]), ws_prompt, bs_prompt, nblk=4, emit_av=False)[0]
    b_out, gla_p = _gla(proj, lg, jnp.zeros((bp, B_HEADS, B_KEY_DIM, B_VAL_DIM), F32), row(b_out_norm[0]),
                        nseq=bp, t=SEQ, clen=CHUNK, ngrp=4, chain=True)
    xp = _outproj(a_out, b_out, w_out0, xp, row(norm_mix_post[0]), tm=tm)
    xp, ffn0_p = ffn_layer(xp, 0, jnp.zeros((bp, ffn_hist_p, D_FF), F32), step=1, hist=ffn_hist_p,
                           tps=tps_ffn_p, convert=False)
    proj = _inproj(xp, row(norm_mix_pre[1]), w_in1, n=D_MODEL, tm=1024, tn=D_MODEL)
    c_out, pool_tail_p = _pool(proj, jnp.zeros((bp, pool_hist_p, C_WIDTH), F32), cmap, row(c_scale[0]),
                               step=1, hist=pool_hist_p, tps=tps_p, tm=tm, pos0=0)
    zero_state = jnp.zeros((bp, 1, S5_CH), F32)
    d_out, s5re_tail_p, s5im_tail_p = _s5(proj, zero_state, zero_state, bblk, cblk, row(s5_d[0]), wglu, k_prompt,
                                          step=1, tps=tps_ffn_p, tm=tm_ffn)
    xp = _outproj(c_out, d_out, w_out1, xp, row(norm_mix_post[1]), tm=tm)
    xp, ffn1_p = ffn_layer(xp, 1, jnp.zeros((bp, ffn_hist_p, D_FF), F32), step=1, hist=ffn_hist_p,
                           tps=tps_ffn_p, convert=False)

    last = slice(tps_p - 1, None, tps_p)
    y_prompt = xp.reshape(bp, SEQ, D_MODEL)
    gla_prompt = gla_p[None]
    pool_prompt = pool_tail_p[last, pool_hist_p - POOL_BUF:][None]
    last_ffn = slice(tps_ffn_p - 1, None, tps_ffn_p)
    s5_re_prompt = s5re_tail_p[last_ffn, 0].reshape(1, bp, S5_GROUPS, S5_STATE)
    s5_im_prompt = s5im_tail_p[last_ffn, 0].reshape(1, bp, S5_GROUPS, S5_STATE)
    ffn_prompt = jnp.stack([ffn0_p[last_ffn, ffn_hist_p - 2:], ffn1_p[last_ffn, ffn_hist_p - 2:]])

    def batch_major(a, nt):
        return jnp.swapaxes(a.reshape(nt, nb, a.shape[-1]), 0, 1)

    y_sample = batch_major(xs, ts)
    gla_sample = gla_s[None]
    av_sample = a_v.reshape(1, nb, ts, A_WIDTH)
    pool_sample = batch_major(pool_tail_s[-1, step_s:], POOL_BUF)[None]
    s5_re_sample = s5re_tail_s[-1].reshape(1, nb, S5_GROUPS, S5_STATE)
    s5_im_sample = s5im_tail_s[-1].reshape(1, nb, S5_GROUPS, S5_STATE)
    ffn_sample = jnp.stack([batch_major(ffn0_s[-1], 2), batch_major(ffn1_s[-1], 2)])

    return (y_prompt, y_sample, gla_prompt, gla_sample, av_sample, pool_prompt, pool_sample,
            s5_re_prompt, s5_im_prompt, s5_re_sample, s5_im_sample, ffn_prompt, ffn_sample)
```

```python
import functools
import math

import jax
import jax.numpy as jnp
from jax import lax
from jax.experimental import pallas as pl
from jax.experimental.pallas import tpu as pltpu

F32 = jnp.float32
BF16 = jnp.bfloat16

D_MODEL = 2048
SEQ = 4096
DEC_BATCH = 32
DEC_SEQ = 32
PAST_LEN = 4096
CHUNK = 64
A_WIDTH = 1024
A_HEADS = 8
A_BLOCK = 128
B_HEADS = 4
B_KEY_DIM = 128
B_KEY_WIDTH = 512
B_VAL_DIM = 256
B_WIDTH = 1024
B_GATE_RANK = 16
B_GATE_TAU = 16.0
C_WIDTH = 1024
C_GROUP_DIM = 256
POOL_WINDOWS = (2, 4, 8, 16)
POOL_BUF = 15
D_WIDTH = 1024
S5_GROUPS = 64
S5_GROUP_DIM = 16
S5_STATE = 64
S5_CH = S5_GROUPS * S5_STATE
D_FF = 5632
EPS = 1e-6

LANES = 128
SUBLANES = 8
VMEM_LIMIT = 56 * 1024 * 1024


def _params(sem):
    return pltpu.CompilerParams(dimension_semantics=sem, vmem_limit_bytes=VMEM_LIMIT)


def _rms(x, g):
    return x * lax.rsqrt(jnp.mean(x * x, axis=-1, keepdims=True) + EPS) * g


def _dot(a, b):
    return jnp.dot(a, b, preferred_element_type=F32)


def _inproj_kernel(x_ref, g_ref, w_ref, *rest, with_gate, convert):
    rest = list(rest)
    if with_gate:
        wlr_ref, wgate_ref, gbias_ref = rest[:3]
        rest = rest[3:]
    o_ref = rest.pop(0)
    if with_gate:
        lg_ref = rest.pop(0)
    if convert:
        w16_ref = rest.pop(0)
    h_ref, = rest
    j = pl.program_id(1)

    def project(hb):
        w = w_ref[...]
        if convert:
            w = w.astype(BF16)
            w16_ref[...] = w
        o_ref[...] = _dot(hb, w)

    @pl.when(j == 0)
    def _():
        hb = _rms(x_ref[...], g_ref[...]).astype(BF16)
        h_ref[...] = hb
        project(hb)
        if with_gate:
            glr = _dot(hb, wlr_ref[...])
            z = _dot(glr.astype(BF16), wgate_ref[...]) + gbias_ref[...]
            lg_ref[...] = (jnp.minimum(z, 0.0) - jnp.log(1.0 + jnp.exp(-jnp.abs(z)))) * (1.0 / B_GATE_TAU)

    @pl.when(j > 0)
    def _():
        project(h_ref[...])


def _inproj(x, g, w, gate=None, *, n, tm, tn, convert=False):
    m, d = x.shape
    grid = (m // tm, n // tn)
    one_tile = m == tm
    in_specs = [pl.BlockSpec((tm, d), lambda i, j: (i, 0), pipeline_mode=pl.Buffered(1) if one_tile else None),
                pl.BlockSpec((1, d), lambda i, j: (0, 0)),
                pl.BlockSpec((None, d, tn), lambda i, j: (0, 0, j)) if convert
                else pl.BlockSpec((d, tn), lambda i, j: (0, j), pipeline_mode=pl.Buffered(1) if n == tn else None)]
    out_shape = [jax.ShapeDtypeStruct((m, n), F32)]
    out_specs = [pl.BlockSpec((tm, tn), lambda i, j: (i, j))]
    args = [x, g, w]
    if gate is not None:
        wlr, wgate, gbias = gate
        in_specs += [pl.BlockSpec(wlr.shape, lambda i, j: (0, 0)),
                     pl.BlockSpec(wgate.shape, lambda i, j: (0, 0)),
                     pl.BlockSpec(gbias.shape, lambda i, j: (0, 0))]
        out_shape.append(jax.ShapeDtypeStruct((m, B_KEY_WIDTH), F32))
        out_specs.append(pl.BlockSpec((tm, B_KEY_WIDTH), lambda i, j: (i, 0)))
        args += [wlr, wgate, gbias]
    if convert:
        out_shape.append(jax.ShapeDtypeStruct((d, n), BF16))
        out_specs.append(pl.BlockSpec((d, tn), lambda i, j: (0, j)))
    res = pl.pallas_call(
        functools.partial(_inproj_kernel, with_gate=gate is not None, convert=convert),
        out_shape=out_shape, grid=grid, in_specs=in_specs, out_specs=out_specs,
        scratch_shapes=[pltpu.VMEM((tm, d), BF16)],
        compiler_params=_params(("parallel", "arbitrary")),
        name=("inproj_gate" if gate is not None else "inproj") + ("_convert" if convert else ""),
    )(*args)
    return res if len(res) > 1 else res[0]


def _inproj_sgu_kernel(x_ref, g_ref, w_ref, wlr_ref, wgate_ref, gbias_ref, gain_ref, ws_ref, bs_ref,
                       o_ref, lg_ref, a_ref, *rest):
    av_ref = rest[0] if len(rest) == 3 else None
    h_ref, ug_ref = rest[-2:]
    j = pl.program_id(1)
    tm = x_ref.shape[0]

    @pl.when(j == 0)
    def _():
        hb = _rms(x_ref[...], g_ref[...]).astype(BF16)
        h_ref[...] = hb
        ug_ref[...] = jax.nn.gelu(_dot(hb, w_ref[...]))
        glr = _dot(hb, wlr_ref[...])
        z = _dot(glr.astype(BF16), wgate_ref[...]) + gbias_ref[...]
        lg_ref[...] = (jnp.minimum(z, 0.0) - jnp.log(1.0 + jnp.exp(-jnp.abs(z)))) * (1.0 / B_GATE_TAU)

    @pl.when(j == 1)
    def _():
        v_all = jax.nn.gelu(_dot(h_ref[...], w_ref[...]))
        for n in range(tm // A_BLOCK):
            rows = slice(n * A_BLOCK, (n + 1) * A_BLOCK)
            v = v_all[rows, :]
            mu = jnp.mean(v, axis=-1, keepdims=True)
            vc = v - mu
            vn = vc * lax.rsqrt(jnp.mean(vc * vc, axis=-1, keepdims=True) + EPS) * gain_ref[...]
            if av_ref is not None:
                av_ref[rows, :] = vn
            vb = vn.astype(BF16)
            for h in range(A_HEADS):
                cols = slice(h * LANES, (h + 1) * LANES)
                s = _dot(ws_ref[h], vb[:, cols]) + bs_ref[:, h:h + 1]
                a_ref[rows, cols] = (ug_ref[rows, cols] * s).astype(a_ref.dtype)

    @pl.when(j >= 2)
    def _():
        o_ref[...] = _dot(h_ref[...], w_ref[...])


def _inproj_sgu(x, g, w, gate, gain, ws, bs, *, tm, emit_av):
    m, d = x.shape
    tn = A_WIDTH
    n_rest = 2 * B_KEY_WIDTH + 2 * B_WIDTH
    wlr, wgate, gbias = gate
    const = lambda a: pl.BlockSpec(a.shape, lambda i, j: (0,) * a.ndim)
    rows = lambda width: pl.BlockSpec((tm, width), lambda i, j: (i, 0))
    out_shape = [jax.ShapeDtypeStruct((m, n_rest), F32), jax.ShapeDtypeStruct((m, B_KEY_WIDTH), F32),
                 jax.ShapeDtypeStruct((m, A_WIDTH), BF16)]
    out_specs = [pl.BlockSpec((tm, tn), lambda i, j: (i, jnp.maximum(j - 2, 0))), rows(B_KEY_WIDTH), rows(A_WIDTH)]
    if emit_av:
        out_shape.append(jax.ShapeDtypeStruct((m, A_WIDTH), F32))
        out_specs.append(rows(A_WIDTH))
    return pl.pallas_call(
        _inproj_sgu_kernel,
        out_shape=out_shape,
        grid=(m // tm, 2 + n_rest // tn),
        in_specs=[pl.BlockSpec((tm, d), lambda i, j: (i, 0), pipeline_mode=pl.Buffered(1)),
                  const(g),
                  pl.BlockSpec((d, tn), lambda i, j: (0, j)),
                  const(wlr), const(wgate), const(gbias), const(gain), const(ws), const(bs)],
        out_specs=out_specs,
        scratch_shapes=[pltpu.VMEM((tm, d), BF16), pltpu.VMEM((tm, A_WIDTH), F32)],
        compiler_params=_params(("parallel", "arbitrary")),
        name="inproj_sgu",
    )(x, g, w, wlr, wgate, gbias, gain, ws, bs)


def _gla_kernel(q_ref, k_ref, v_ref, r_ref, lg_ref, s0_ref, og_ref, o_ref, sout_ref, st_ref, *, clen, ngrp, chain):
    c = pl.program_id(1)
    rows_all = ngrp * clen
    shift = int(math.log2(clen))

    if chain:
        @pl.when(c == 0)
        def _():
            for h in range(B_HEADS):
                st_ref[h] = s0_ref[0, h].T

    row_i = lax.broadcasted_iota(jnp.int32, (rows_all, rows_all), 0)
    col_i = lax.broadcasted_iota(jnp.int32, (rows_all, rows_all), 1)
    same_group = lax.shift_right_logical(row_i, shift) == lax.shift_right_logical(col_i, shift)
    tri = ((row_i >= col_i) & same_group).astype(F32)
    cum = jnp.dot(tri, lg_ref[...], precision=lax.Precision.HIGHEST,
                  preferred_element_type=F32)
    tots = [cum[(g + 1) * clen - 1:(g + 1) * clen, :] for g in range(ngrp)]
    tot_rows = jnp.concatenate([jnp.broadcast_to(t, (clen, B_KEY_WIDTH)) for t in tots], axis=0)
    kd = (k_ref[...] * jnp.exp(tot_rows - cum)).astype(BF16)
    qs = (q_ref[...] * (B_KEY_DIM ** -0.5)).astype(BF16)
    vb = v_ref[...].astype(BF16)
    sr = jax.nn.silu(r_ref[...])
    grp = lax.shift_right_logical(lax.broadcasted_iota(jnp.int32, (rows_all, 1), 0), shift)
    zero = jnp.zeros((), BF16)

    def by_group(x):
        return jnp.concatenate([jnp.where(grp == g, x, zero) for g in range(ngrp)], axis=1)

    for h in range(B_HEADS):
        kc = slice(h * B_KEY_DIM, (h + 1) * B_KEY_DIM)
        vc = slice(h * B_VAL_DIM, (h + 1) * B_VAL_DIM)
        upd = lax.dot_general(vb[:, vc], by_group(kd[:, kc]), (((0,), (0,)), ((), ())),
                              preferred_element_type=F32)
        states = []
        st = st_ref[h] if chain else None
        for g in range(ngrp):
            if not chain:
                st = s0_ref[g, h].T
            st = jnp.exp(tots[g][:, kc]) * st + upd[:, g * B_KEY_DIM:(g + 1) * B_KEY_DIM]
            states.append(st.astype(BF16))
            if not chain:
                sout_ref[g, h] = st.T
        if chain:
            st_ref[h] = st
        o = lax.dot_general(by_group(qs[:, kc]), jnp.concatenate(states, axis=1), (((1,), (1,)), ((), ())),
                            preferred_element_type=F32)
        o = o * lax.rsqrt(jnp.mean(o * o, axis=-1, keepdims=True) + EPS)
        o = o * og_ref[:, vc] * sr[:, vc]
        o_ref[:, vc] = o.astype(o_ref.dtype)

    if chain:
        @pl.when(c == pl.num_programs(1) - 1)
        def _():
            for h in range(B_HEADS):
                sout_ref[0, h] = st_ref[h].T


def _gla(proj, lg, s0, og, *, nseq, t, clen, ngrp, chain):
    rows = clen * ngrp
    m = nseq * t
    if chain:
        steps = t // rows
        grid = (nseq, steps)
        rmap = lambda b, c: b * steps + c
        nstate = 1
    else:
        grid = (m // rows, 1)
        rmap = lambda b, c: b
        nstate = ngrp
    blk = lambda width, col: pl.BlockSpec((rows, width), lambda b, c: (rmap(b, c), col))
    state_spec = pl.BlockSpec((nstate, B_HEADS, B_KEY_DIM, B_VAL_DIM), lambda b, c: (b, 0, 0, 0))
    return pl.pallas_call(
        functools.partial(_gla_kernel, clen=clen, ngrp=ngrp, chain=chain),
        out_shape=[jax.ShapeDtypeStruct((m, B_WIDTH), BF16),
                   jax.ShapeDtypeStruct((nseq, B_HEADS, B_KEY_DIM, B_VAL_DIM), F32)],
        grid=grid,
        in_specs=[blk(B_KEY_WIDTH, 0),
                  blk(B_KEY_WIDTH, 1),
                  blk(B_WIDTH, 1),
                  blk(B_WIDTH, 2),
                  blk(B_KEY_WIDTH, 0),
                  state_spec,
                  pl.BlockSpec((1, B_WIDTH), lambda b, c: (0, 0))],
        out_specs=[blk(B_WIDTH, 0), state_spec],
        scratch_shapes=[pltpu.VMEM((B_HEADS, B_VAL_DIM, B_KEY_DIM), F32)],
        compiler_params=_params(("parallel", "arbitrary")),
        name="gla",
    )(proj, proj, proj, proj, lg, s0, og)


def _outproj_kernel(a_ref, b_ref, w_ref, x_ref, g_ref, o_ref, *w16_ref, convert):
    wb_ref = w_ref
    if convert:
        wb_ref, = w16_ref

        @pl.when(pl.program_id(0) == 0)
        def _():
            wb_ref[...] = w_ref[...].astype(BF16)

    ka = a_ref.shape[1]
    y = _dot(a_ref[...], wb_ref[0:ka, :]) + _dot(b_ref[...], wb_ref[ka:, :])
    o_ref[...] = x_ref[...] + _rms(y, g_ref[...])


def _outproj(a, b, w, x, g, *, tm, convert=False):
    m, d = x.shape
    k = a.shape[1] + b.shape[1]
    once = pl.Buffered(1)
    out_shape = [jax.ShapeDtypeStruct((m, d), F32)]
    out_specs = [pl.BlockSpec((tm, d), lambda i: (i, 0))]
    if convert:
        w_spec = pl.BlockSpec((None, k, d), lambda i: (0, 0, 0), pipeline_mode=once)
        out_shape.append(jax.ShapeDtypeStruct((k, d), BF16))
        out_specs.append(pl.BlockSpec((k, d), lambda i: (0, 0), pipeline_mode=once))
    else:
        w_spec = pl.BlockSpec((k, d), lambda i: (0, 0), pipeline_mode=once)
    res = pl.pallas_call(
        functools.partial(_outproj_kernel, convert=convert),
        out_shape=out_shape,
        grid=(m // tm,),
        in_specs=[pl.BlockSpec((tm, a.shape[1]), lambda i: (i, 0)),
                  pl.BlockSpec((tm, b.shape[1]), lambda i: (i, 0)),
                  w_spec,
                  pl.BlockSpec((tm, d), lambda i: (i, 0)),
                  pl.BlockSpec((1, d), lambda i: (0, 0))],
        out_specs=out_specs,
        compiler_params=_params(("arbitrary",)),
        name="outproj_convert" if convert else "outproj",
    )(a, b, w, x, g)
    return res if convert else res[0]


def _ffn_kernel(x_ref, gpre_ref, wg_ref, wv_ref, cw_ref, cb_ref, wd_ref, gpost_ref, st_ref,
                o_ref, tail_ref, *rest, step, hist, tps, tm, nf, convert):
    if convert:
        wg_out, wv_out, wd_out, h_ref, gext_ref, carry_ref, act_a, act_b = rest
    else:
        h_ref, gext_ref, carry_ref, act_a, act_b = rest
    i = pl.program_id(0)
    j = pl.program_id(1)
    first = (i % tps) == 0

    def up_and_gate(act_ref, hb=None):
        if hb is None:
            hb = h_ref[...]
        wg, wv = wg_ref[...], wv_ref[...]
        if convert:
            wg, wv = wg.astype(BF16), wv.astype(BF16)
            wg_out[...] = wg
            wv_out[...] = wv
        gate = _dot(hb, wg)
        val = _dot(hb, wv)
        cw = cw_ref[j]
        gext_ref[0:hist, :] = jnp.where(first, st_ref[0, j], carry_ref[j])
        gext_ref[hist:hist + tm, :] = gate
        prev2 = gext_ref[hist - 2 * step:hist - 2 * step + tm, :]
        prev1 = gext_ref[hist - step:hist - step + tm, :]
        conv = cb_ref[j] + cw[0:1, :] * prev2 + cw[1:2, :] * prev1 + cw[2:3, :] * gate
        act_ref[...] = (jax.nn.gelu(conv) * val).astype(BF16)
        tail = gate[tm - hist:, :]
        carry_ref[j] = tail
        tail_ref[0, j] = tail

    def down(act_ref):
        wd = wd_ref[...]
        if convert:
            wd = wd.astype(BF16)
            wd_out[...] = wd
        o_ref[...] += _dot(act_ref[...], wd)

    @pl.when(j == 0)
    def _():
        @pl.when(i == 0)
        def _():
            carry_ref[...] = jnp.zeros_like(carry_ref)

        o_ref[...] = jnp.zeros_like(o_ref)
        hb = _rms(x_ref[...], gpre_ref[...]).astype(BF16)
        h_ref[...] = hb
        up_and_gate(act_a, hb)

    for parity, (src, dst) in enumerate(((act_b, act_a), (act_a, act_b))):
        @pl.when((j > 0) & (j < nf) & (j % 2 == parity))
        def _(src=src, dst=dst):
            down(src)
            up_and_gate(dst)

    @pl.when(j == nf)
    def _():
        down(act_a if (nf - 1) % 2 == 0 else act_b)
        o_ref[...] = x_ref[...] + _rms(o_ref[...], gpost_ref[...])


def _ffn(x, gpre, weights, cw, cb, gpost, state, *, layer, step, hist, tps, tm, tf, convert):
    m, d = x.shape
    nf = D_FF // tf
    nm = m // tm
    up = lambda j: jnp.minimum(j, nf - 1)
    down = lambda j: jnp.maximum(j - 1, 0)

    def by_tile(a):
        a = a.reshape(a.shape[:-1] + (nf, tf))
        return jnp.swapaxes(a, -2, -3)

    cw, cb, state = by_tile(cw), by_tile(cb), by_tile(state)
    if convert:
        w_up, w_down = weights
        w_args = (w_up, w_up, w_down)
        w_specs = [pl.BlockSpec((None, d, tf), lambda i, j: (layer, 0, up(j))),
                   pl.BlockSpec((None, d, tf), lambda i, j: (layer, 0, nf + up(j))),
                   pl.BlockSpec((None, tf, d), lambda i, j: (layer, down(j), 0))]
        extra_shapes = [jax.ShapeDtypeStruct((d, D_FF), BF16), jax.ShapeDtypeStruct((d, D_FF), BF16),
                        jax.ShapeDtypeStruct((D_FF, d), BF16)]
        extra_specs = [pl.BlockSpec((d, tf), lambda i, j: (0, up(j))),
                       pl.BlockSpec((d, tf), lambda i, j: (0, up(j))),
                       pl.BlockSpec((tf, d), lambda i, j: (down(j), 0))]
    else:
        w_args = weights
        w_specs = [pl.BlockSpec((d, tf), lambda i, j: (0, up(j))),
                   pl.BlockSpec((d, tf), lambda i, j: (0, up(j))),
                   pl.BlockSpec((tf, d), lambda i, j: (down(j), 0))]
        extra_shapes, extra_specs = [], []
    res = pl.pallas_call(
        functools.partial(_ffn_kernel, step=step, hist=hist, tps=tps, tm=tm, nf=nf, convert=convert),
        out_shape=[jax.ShapeDtypeStruct((m, d), F32), jax.ShapeDtypeStruct((nm, nf, hist, tf), F32)] + extra_shapes,
        grid=(nm, nf + 1),
        in_specs=[pl.BlockSpec((tm, d), lambda i, j: (i, 0)),
                  pl.BlockSpec((None, 1, d), lambda i, j: (layer, 0, 0)),
                  w_specs[0], w_specs[1],
                  pl.BlockSpec((None, nf, 3, tf), lambda i, j: (layer, 0, 0, 0)),
                  pl.BlockSpec((None, nf, 1, tf), lambda i, j: (layer, 0, 0, 0)),
                  w_specs[2],
                  pl.BlockSpec((None, 1, d), lambda i, j: (layer, 0, 0)),
                  pl.BlockSpec((1, nf, hist, tf), lambda i, j: (i // tps, 0, 0, 0))],
        out_specs=[pl.BlockSpec((tm, d), lambda i, j: (i, 0), pipeline_mode=pl.Buffered(1)),
                   pl.BlockSpec((1, nf, hist, tf), lambda i, j: (i, 0, 0, 0))] + extra_specs,
        scratch_shapes=[pltpu.VMEM((tm, d), BF16),
                        pltpu.VMEM((hist + tm, tf), F32), pltpu.VMEM((nf, hist, tf), F32),
                        pltpu.VMEM((tm, tf), BF16), pltpu.VMEM((tm, tf), BF16)],
        compiler_params=_params(("arbitrary", "arbitrary")),
        name="ffn_convert" if convert else "ffn",
    )(x, gpre, w_args[0], w_args[1], cw, cb, w_args[2], gpost, state)
    res = list(res)
    res[1] = jnp.swapaxes(res[1], 1, 2).reshape(nm, hist, D_FF)
    return res


def _pool_kernel(c_ref, st_ref, cmap_ref, cs_ref, o_ref, tail_ref, ext_ref, carry_ref,
                 *, step, hist, tps, tm, pos0):
    i = pl.program_id(0)
    first = (i % tps) == 0

    @pl.when(first)
    def _():
        ext_ref[0:hist, :] = st_ref[0]

    @pl.when(jnp.logical_not(first))
    def _():
        ext_ref[0:hist, :] = carry_ref[...]

    ext_ref[hist:hist + tm, :] = c_ref[...]
    row = lax.broadcasted_iota(jnp.int32, (tm, 1), 0)
    if step > 1:
        row = lax.shift_right_logical(row, int(math.log2(step)))
    pos = pos0 + (i % tps) * (tm // step) + row
    for g, win in enumerate(POOL_WINDOWS):
        cols = slice(g * C_GROUP_DIM, (g + 1) * C_GROUP_DIM)
        acc = ext_ref[:, cols]
        d = 1
        while d < win:
            acc = acc + pltpu.roll(acc, d * step, 0)
            d *= 2
        tot = acc[hist:, :]
        cur = ext_ref[hist:hist + tm, cols]
        cnt = jnp.minimum(pos + 1, win).astype(F32)
        delta = tot / cnt - cur
        y = _dot(delta.astype(BF16), cmap_ref[g]) * cs_ref[:, cols]
        o_ref[:, cols] = y.astype(o_ref.dtype)
    tail = ext_ref[tm:tm + hist, :]
    carry_ref[...] = tail
    tail_ref[0] = tail


def _pool(proj, state, cmap, cscale, *, step, hist, tps, tm, pos0):
    m = proj.shape[0]
    nm = m // tm
    assert all(w & (w - 1) == 0 for w in POOL_WINDOWS) and hist >= (max(POOL_WINDOWS) - 1) * step and tm >= hist
    return pl.pallas_call(
        functools.partial(_pool_kernel, step=step, hist=hist, tps=tps, tm=tm, pos0=pos0),
        out_shape=[jax.ShapeDtypeStruct((m, C_WIDTH), BF16), jax.ShapeDtypeStruct((nm, hist, C_WIDTH), F32)],
        grid=(nm,),
        in_specs=[pl.BlockSpec((tm, C_WIDTH), lambda i: (i, 0)),
                  pl.BlockSpec((1, hist, C_WIDTH), lambda i: (i // tps, 0, 0)),
                  pl.BlockSpec(cmap.shape, lambda i: (0, 0, 0)),
                  pl.BlockSpec((1, C_WIDTH), lambda i: (0, 0))],
        out_specs=[pl.BlockSpec((tm, C_WIDTH), lambda i: (i, 0)),
                   pl.BlockSpec((1, hist, C_WIDTH), lambda i: (i, 0, 0))],
        scratch_shapes=[pltpu.VMEM((hist + tm, C_WIDTH), F32), pltpu.VMEM((hist, C_WIDTH), F32)],
        compiler_params=_params(("arbitrary",)),
        name="pool",
    )(proj, state, cmap, cscale)


S5_CHUNK = 512
S5_NCHUNK = S5_CH // S5_CHUNK
S5_FOLD = SUBLANES // 2


def _s5_kernel(u_ref, s0re_ref, s0im_ref, bblk_ref, cblk_ref, dskip_ref, wglu_ref, kc_ref,
               o_ref, tre_ref, tim_ref, cre_ref, cim_ref, *, step, tps, tm):
    i = pl.program_id(0)
    crow = cre_ref.shape[0]

    @pl.when((i % tps) == 0)
    def _():
        cre_ref[...] = jnp.broadcast_to(s0re_ref[0], cre_ref.shape) if step == 1 else s0re_ref[0]
        cim_ref[...] = jnp.broadcast_to(s0im_ref[0], cim_ref.shape) if step == 1 else s0im_ref[0]

    u = u_ref[...]
    lhs = [u.astype(BF16)]
    if step == 1:
        row_in_block = lax.broadcasted_iota(jnp.int32, (tm, 1), 0) & (SUBLANES - 1)
        for k in range(1, S5_FOLD):
            lhs.append(jnp.where(row_in_block >= k, pltpu.roll(u, k, 0), 0.0).astype(BF16))
    ys = []
    for m in range(S5_NCHUNK):
        cols = slice(m * S5_CHUNK, (m + 1) * S5_CHUNK)
        ucols = slice(m * LANES, (m + 1) * LANES)
        if step == 1:
            r = _dot(jnp.concatenate([x[:, ucols] for x in lhs], axis=1), bblk_ref[m])
        else:
            r = _dot(lhs[0][:, ucols], bblk_ref[m, 0:LANES, :])
        cr, ci = cre_ref[:, cols], cim_ref[:, cols]
        sre, sim = [], []
        if step == 1:
            ar, ai, pwr, pwi = [kc_ref[k, :, cols] for k in range(4)]
            for rb in range(tm // SUBLANES):
                rows = slice(rb * SUBLANES, (rb + 1) * SUBLANES)
                xr = r[rows, :S5_CHUNK]
                xi = r[rows, S5_CHUNK:]
                sr = pltpu.roll(xr, S5_FOLD, 0)
                si = pltpu.roll(xi, S5_FOLD, 0)
                xr, xi = xr + (ar * sr - ai * si), xi + (ar * si + ai * sr)
                xr, xi = xr + (pwr * cr - pwi * ci), xi + (pwr * ci + pwi * cr)
                sre.append(xr)
                sim.append(xi)
                cr = jnp.broadcast_to(xr[SUBLANES - 1:SUBLANES, :], xr.shape)
                ci = jnp.broadcast_to(xi[SUBLANES - 1:SUBLANES, :], xi.shape)
        else:
            lr = jnp.broadcast_to(kc_ref[0, 0:1, cols], (crow, S5_CHUNK))
            li = jnp.broadcast_to(kc_ref[1, 0:1, cols], (crow, S5_CHUNK))
            for t in range(tm // step):
                rows = slice(t * step, (t + 1) * step)
                cr, ci = (r[rows, :S5_CHUNK] + (lr * cr - li * ci),
                          r[rows, S5_CHUNK:] + (lr * ci + li * cr))
                sre.append(cr)
                sim.append(ci)
        cre_ref[:, cols] = cr
        cim_ref[:, cols] = ci
        ys.append(_dot(jnp.concatenate(sre, axis=0).astype(BF16), cblk_ref[m, 0:S5_CHUNK, :])
                  + _dot(jnp.concatenate(sim, axis=0).astype(BF16), cblk_ref[m, S5_CHUNK:, :]))

    tre_ref[0] = cre_ref[...]
    tim_ref[0] = cim_ref[...]

    y = jnp.concatenate(ys, axis=1) + dskip_ref[...] * u
    z = _dot(jax.nn.gelu(y).astype(BF16), wglu_ref[...])
    o_ref[...] = (z[:, :D_WIDTH] * jax.nn.sigmoid(z[:, D_WIDTH:])).astype(o_ref.dtype)


def _s5(proj, s0re, s0im, bblk, cblk, dskip, wglu, kconst, *, step, tps, tm):
    m = proj.shape[0]
    nm = m // tm
    crow = s0re.shape[1] if step > 1 else SUBLANES
    srow = s0re.shape[1]
    return pl.pallas_call(
        functools.partial(_s5_kernel, step=step, tps=tps, tm=tm),
        out_shape=[jax.ShapeDtypeStruct((m, D_WIDTH), BF16),
                   jax.ShapeDtypeStruct((nm, crow, S5_CH), F32),
                   jax.ShapeDtypeStruct((nm, crow, S5_CH), F32)],
        grid=(nm,),
        in_specs=[pl.BlockSpec((tm, D_WIDTH), lambda i: (i, 1)),
                  pl.BlockSpec((1, srow, S5_CH), lambda i: (i // tps, 0, 0)),
                  pl.BlockSpec((1, srow, S5_CH), lambda i: (i // tps, 0, 0)),
                  pl.BlockSpec(bblk.shape, lambda i: (0, 0, 0), pipeline_mode=pl.Buffered(1)),
                  pl.BlockSpec(cblk.shape, lambda i: (0, 0, 0), pipeline_mode=pl.Buffered(1)),
                  pl.BlockSpec((1, D_WIDTH), lambda i: (0, 0)),
                  pl.BlockSpec(wglu.shape, lambda i: (0, 0), pipeline_mode=pl.Buffered(1)),
                  pl.BlockSpec(kconst.shape, lambda i: (0, 0, 0), pipeline_mode=pl.Buffered(1))],
        out_specs=[pl.BlockSpec((tm, D_WIDTH), lambda i: (i, 0)),
                   pl.BlockSpec((1, crow, S5_CH), lambda i: (i, 0, 0)),
                   pl.BlockSpec((1, crow, S5_CH), lambda i: (i, 0, 0))],
        scratch_shapes=[pltpu.VMEM((crow, S5_CH), F32), pltpu.VMEM((crow, S5_CH), F32)],
        compiler_params=_params(("arbitrary",)),
        name="s5",
    )(proj, s0re, s0im, bblk, cblk, dskip, wglu, kconst)


def _s5_constants(a_re, a_im, log_dt, b_re, b_im, c_re, c_im):
    per = S5_CHUNK // S5_STATE
    nchunk = S5_GROUPS // per
    on_diag = jnp.arange(per)[:, None] == jnp.arange(per)[None, :]
    zero = jnp.zeros((), BF16)
    dt = jnp.exp(log_dt)[:, None]
    zr, zi = a_re * dt, a_im * dt
    mag = jnp.exp(zr)
    lr, li = mag * jnp.cos(zi), mag * jnp.sin(zi)
    den = a_re * a_re + a_im * a_im
    nr, ni = lr - 1.0, li
    kr, ki = (nr * a_re + ni * a_im) / den, (ni * a_re - nr * a_im) / den
    bbr = kr[..., None] * b_re - ki[..., None] * b_im
    bbi = kr[..., None] * b_im + ki[..., None] * b_re
    qr, qi = [jnp.ones_like(lr)], [jnp.zeros_like(li)]
    for _ in range(S5_FOLD - 1):
        qr, qi = qr + [qr[-1] * lr - qi[-1] * li], qi + [qr[-1] * li + qi[-1] * lr]
    qr, qi = jnp.stack(qr)[..., None], jnp.stack(qi)[..., None]
    f = jnp.stack([qr * bbr - qi * bbi, qr * bbi + qi * bbr])
    f = f.reshape(2, S5_FOLD, nchunk, per, S5_STATE, S5_GROUP_DIM).transpose(2, 1, 3, 5, 0, 4).astype(BF16)
    f = f.reshape(nchunk, S5_FOLD * LANES, 2 * S5_STATE)
    rows = S5_FOLD * LANES
    src = jnp.arange(2 * S5_STATE)[:, None]
    dst = jnp.arange(2 * S5_CHUNK)[None, :]
    spread = ((src // S5_STATE == dst // S5_CHUNK) & (src % S5_STATE == dst % S5_STATE)).astype(BF16)
    row_group = (jnp.arange(rows) % LANES) // S5_GROUP_DIM
    col_group = (jnp.arange(2 * S5_CHUNK) % S5_CHUNK) // S5_STATE
    bblk = jnp.dot(f.reshape(nchunk * rows, 2 * S5_STATE), spread, preferred_element_type=BF16)
    bblk = jnp.where(row_group[:, None] == col_group[None, :], bblk.reshape(nchunk, rows, 2 * S5_CHUNK), zero)
    c = jnp.stack([c_re, -c_im]).reshape(2, nchunk, per, S5_GROUP_DIM, S5_STATE).transpose(1, 0, 2, 4, 3).astype(BF16)
    cblk = jnp.where(on_diag[None, None, :, None, :, None], c[:, :, :, :, None, :], zero)
    cblk = cblk.reshape(nchunk, 2 * S5_CHUNK, LANES)
    lr, li = lr.reshape(1, S5_CH), li.reshape(1, S5_CH)
    pr, pi = [lr], [li]
    for _ in range(SUBLANES - 1):
        pr, pi = pr + [pr[-1] * lr - pi[-1] * li], pi + [pr[-1] * li + pi[-1] * lr]
    rowid = jnp.arange(SUBLANES)[:, None]

    def masked(p, d):
        return jnp.where(rowid >= d, jnp.broadcast_to(p[d - 1], (SUBLANES, S5_CH)), 0.0)

    k_prompt = jnp.stack([masked(pr, S5_FOLD), masked(pi, S5_FOLD),
                          jnp.concatenate(pr, axis=0), jnp.concatenate(pi, axis=0)])
    k_sample = jnp.stack([jnp.broadcast_to(lr, (SUBLANES, S5_CH)), jnp.broadcast_to(li, (SUBLANES, S5_CH))])
    return bblk, cblk, k_prompt, k_sample


def _time_major(a):
    a = jnp.swapaxes(a, 0, 1)
    return a.reshape((a.shape[0] * a.shape[1],) + a.shape[2:])


def kernel(x_prompt, x_sample, state_gla, state_pool, state_s5_re, state_s5_im, state_ffn_conv, norm_mix_pre, norm_mix_post, norm_ffn_pre, norm_ffn_post, w_in_even, a_w_s, a_b_s, a_v_norm, b_w_gate, b_gate_bias, b_out_norm, w_out_even, w_in_odd, c_map, c_scale, s5_a_re, s5_a_im, s5_log_dt, s5_b_re, s5_b_im, s5_c_re, s5_c_im, s5_d, s5_w_glu, w_out_odd, ffn_w_up, ffn_conv_w, ffn_conv_b, ffn_w_down):
    bp = x_prompt.shape[0]
    nb, ts = x_sample.shape[0], x_sample.shape[1]
    xp = x_prompt.reshape(bp * SEQ, D_MODEL)
    xs = x_sample.reshape(nb * ts, D_MODEL)

    row = lambda v: v.reshape(1, -1)
    n_main = 2 * A_WIDTH + 2 * B_KEY_WIDTH + 2 * B_WIDTH
    w_in0 = w_in_even[0].astype(BF16)
    w_lr = jnp.pad(w_in0[:, n_main:], ((0, 0), (0, LANES - B_GATE_RANK)))
    w_gate = jnp.pad(b_w_gate[0], ((0, LANES - B_GATE_RANK), (0, 0))).astype(BF16)
    gate = (w_lr, w_gate, row(b_gate_bias[0]))
    pos = jnp.arange(A_BLOCK)
    causal = (pos[None, :] // CHUNK) <= (pos[:, None] // CHUNK)
    ws_prompt = jnp.where(causal[None], a_w_s[0], 0.0).astype(BF16)
    per = A_BLOCK // ts
    ws_small = jnp.where(causal[None, :ts, :ts], a_w_s[0][:, :ts, :ts], 0.0)
    ws_sample = jnp.einsum('hij,ab->haibj', ws_small, jnp.eye(per, dtype=F32)).reshape(A_HEADS, A_BLOCK, A_BLOCK).astype(BF16)
    bs_prompt = a_b_s[0].T
    bs_sample = jnp.tile(a_b_s[0][:, :ts].T, (per, 1))
    cmap = c_map[0].astype(BF16)
    bblk, cblk, k_prompt, k_sample = _s5_constants(s5_a_re[0], s5_a_im[0], s5_log_dt[0], s5_b_re[0], s5_b_im[0],
                                                   s5_c_re[0], s5_c_im[0])
    wglu = s5_w_glu[0].astype(BF16)

    tm = 512
    tf = 512
    tps_p = SEQ // tm
    ffn_hist_p = SUBLANES
    pool_hist_p = 2 * SUBLANES
    step_s = nb
    tps_s = (nb * ts) // tm
    ffn_hist_s = 2 * step_s
    pool_hist_s = (POOL_BUF + 1) * step_s

    tm_ffn = 1024
    tps_ffn_p = SEQ // tm_ffn
    tps_ffn_s = (nb * ts) // tm_ffn

    ffn_w16 = {}

    def ffn_layer(x, layer, state, *, step, hist, tps, convert):
        weights = (ffn_w_up, ffn_w_down) if convert else ffn_w16[layer]
        res = _ffn(x, norm_ffn_pre[:, None], weights, ffn_conv_w, ffn_conv_b[:, None], norm_ffn_post[:, None],
                   state, layer=layer, step=step, hist=hist, tps=tps, tm=tm_ffn, tf=256 if convert else tf,
                   convert=convert)
        if convert:
            ffn_w16[layer] = tuple(res[2:])
        return res[0], res[1]

    proj, lg, a_out, a_v = _inproj_sgu(xs, row(norm_mix_pre[0]), w_in0, gate, row(a_v_norm[0]), ws_sample, bs_sample,
                                       tm=1024, emit_av=True)
    b_out, gla_s = _gla(proj, lg, state_gla[0], row(b_out_norm[0]), nseq=nb, t=ts, clen=ts, ngrp=8, chain=False)
    xs, w_out0 = _outproj(a_out, b_out, w_out_even, xs, row(norm_mix_post[0]), tm=tm, convert=True)
    xs = _time_major(xs.reshape(nb, ts, D_MODEL))
    ffn_state = lambda layer: _time_major(state_ffn_conv[layer])[None]
    xs, ffn0_s = ffn_layer(xs, 0, ffn_state(0), step=step_s, hist=ffn_hist_s, tps=tps_ffn_s, convert=True)
    proj, w_in1 = _inproj(xs, row(norm_mix_pre[1]), w_in_odd, n=D_MODEL, tm=1024, tn=512, convert=True)
    pool_state = jnp.pad(_time_major(state_pool[0]), ((step_s, 0), (0, 0)))[None]
    c_out, pool_tail_s = _pool(proj, pool_state, cmap, row(c_scale[0]),
                               step=step_s, hist=pool_hist_s, tps=tps_s, tm=tm, pos0=PAST_LEN)
    d_out, s5re_tail_s, s5im_tail_s = _s5(proj, state_s5_re[0].reshape(1, nb, S5_CH),
                                          state_s5_im[0].reshape(1, nb, S5_CH),
                                          bblk, cblk, row(s5_d[0]), wglu, k_sample, step=step_s, tps=tps_s, tm=tm)
    xs, w_out1 = _outproj(c_out, d_out, w_out_odd, xs, row(norm_mix_post[1]), tm=tm, convert=True)
    xs, ffn1_s = ffn_layer(xs, 1, ffn_state(1), step=step_s, hist=ffn_hist_s, tps=tps_ffn_s, convert=True)

    proj, lg, a_out = _inproj_sgu(xp, row(norm_mix_pre[0]), w_in0, gate, row(a_v_norm[0]), ws_prompt, bs_prompt,
                                  tm=1024, emit_av=False)
    b_out, gla_p = _gla(proj, lg, jnp.zeros((bp, B_HEADS, B_KEY_DIM, B_VAL_DIM), F32), row(b_out_norm[0]),
                        nseq=bp, t=SEQ, clen=CHUNK, ngrp=4, chain=True)
    xp = _outproj(a_out, b_out, w_out0, xp, row(norm_mix_post[0]), tm=tm)
    xp, ffn0_p = ffn_layer(xp, 0, jnp.zeros((bp, ffn_hist_p, D_FF), F32), step=1, hist=ffn_hist_p,
                           tps=tps_ffn_p, convert=False)
    proj = _inproj(xp, row(norm_mix_pre[1]), w_in1, n=D_MODEL, tm=1024, tn=D_MODEL)
    c_out, pool_tail_p = _pool(proj, jnp.zeros((bp, pool_hist_p, C_WIDTH), F32), cmap, row(c_scale[0]),
                               step=1, hist=pool_hist_p, tps=tps_p, tm=tm, pos0=0)
    zero_state = jnp.zeros((bp, 1, S5_CH), F32)
    d_out, s5re_tail_p, s5im_tail_p = _s5(proj, zero_state, zero_state, bblk, cblk, row(s5_d[0]), wglu, k_prompt,
                                          step=1, tps=tps_ffn_p, tm=tm_ffn)
    xp = _outproj(c_out, d_out, w_out1, xp, row(norm_mix_post[1]), tm=tm)
    xp, ffn1_p = ffn_layer(xp, 1, jnp.zeros((bp, ffn_hist_p, D_FF), F32), step=1, hist=ffn_hist_p,
                           tps=tps_ffn_p, convert=False)

    last = slice(tps_p - 1, None, tps_p)
    y_prompt = xp.reshape(bp, SEQ, D_MODEL)
    gla_prompt = gla_p[None]
    pool_prompt = pool_tail_p[last, pool_hist_p - POOL_BUF:][None]
    last_ffn = slice(tps_ffn_p - 1, None, tps_ffn_p)
    s5_re_prompt = s5re_tail_p[last_ffn, 0].reshape(1, bp, S5_GROUPS, S5_STATE)
    s5_im_prompt = s5im_tail_p[last_ffn, 0].reshape(1, bp, S5_GROUPS, S5_STATE)
    ffn_prompt = jnp.stack([ffn0_p[last_ffn, ffn_hist_p - 2:], ffn1_p[last_ffn, ffn_hist_p - 2:]])

    def batch_major(a, nt):
        return jnp.swapaxes(a.reshape(nt, nb, a.shape[-1]), 0, 1)

    y_sample = batch_major(xs, ts)
    gla_sample = gla_s[None]
    av_sample = a_v.reshape(1, nb, ts, A_WIDTH)
    pool_sample = batch_major(pool_tail_s[-1, step_s:], POOL_BUF)[None]
    s5_re_sample = s5re_tail_s[-1].reshape(1, nb, S5_GROUPS, S5_STATE)
    s5_im_sample = s5im_tail_s[-1].reshape(1, nb, S5_GROUPS, S5_STATE)
    ffn_sample = jnp.stack([batch_major(ffn0_s[-1], 2), batch_major(ffn1_s[-1], 2)])

    return (y_prompt, y_sample, gla_prompt, gla_sample, av_sample, pool_prompt, pool_sample,
            s5_re_prompt, s5_im_prompt, s5_re_sample, s5_im_sample, ffn_prompt, ffn_sample)
```

```python
import functools
import math

import jax
import jax.numpy as jnp
from jax import lax
from jax.experimental import pallas as pl
from jax.experimental.pallas import tpu as pltpu

F32 = jnp.float32
BF16 = jnp.bfloat16

D_MODEL = 2048
SEQ = 4096
DEC_BATCH = 32
DEC_SEQ = 32
PAST_LEN = 4096
CHUNK = 64
A_WIDTH = 1024
A_HEADS = 8
A_BLOCK = 128
B_HEADS = 4
B_KEY_DIM = 128
B_KEY_WIDTH = 512
B_VAL_DIM = 256
B_WIDTH = 1024
B_GATE_RANK = 16
B_GATE_TAU = 16.0
C_WIDTH = 1024
C_GROUP_DIM = 256
POOL_WINDOWS = (2, 4, 8, 16)
POOL_BUF = 15
D_WIDTH = 1024
S5_GROUPS = 64
S5_GROUP_DIM = 16
S5_STATE = 64
S5_CH = S5_GROUPS * S5_STATE
D_FF = 5632
EPS = 1e-6

LANES = 128
SUBLANES = 8
VMEM_LIMIT = 56 * 1024 * 1024


def _params(sem):
    return pltpu.CompilerParams(dimension_semantics=sem, vmem_limit_bytes=VMEM_LIMIT)


def _rms(x, g):
    return x * lax.rsqrt(jnp.mean(x * x, axis=-1, keepdims=True) + EPS) * g


def _dot(a, b):
    return jnp.dot(a, b, preferred_element_type=F32)


def _inproj_kernel(x_ref, g_ref, w_ref, *rest, with_gate, convert):
    rest = list(rest)
    if with_gate:
        wlr_ref, wgate_ref, gbias_ref = rest[:3]
        rest = rest[3:]
    o_ref = rest.pop(0)
    if with_gate:
        lg_ref = rest.pop(0)
    if convert:
        w16_ref = rest.pop(0)
    h_ref, = rest
    j = pl.program_id(1)

    def project(hb):
        w = w_ref[...]
        if convert:
            w = w.astype(BF16)
            w16_ref[...] = w
        o_ref[...] = _dot(hb, w)

    @pl.when(j == 0)
    def _():
        hb = _rms(x_ref[...], g_ref[...]).astype(BF16)
        h_ref[...] = hb
        project(hb)
        if with_gate:
            glr = _dot(hb, wlr_ref[...])
            z = _dot(glr.astype(BF16), wgate_ref[...]) + gbias_ref[...]
            lg_ref[...] = (jnp.minimum(z, 0.0) - jnp.log(1.0 + jnp.exp(-jnp.abs(z)))) * (1.0 / B_GATE_TAU)

    @pl.when(j > 0)
    def _():
        project(h_ref[...])


def _inproj(x, g, w, gate=None, *, n, tm, tn, convert=False):
    m, d = x.shape
    grid = (m // tm, n // tn)
    one_tile = m == tm
    in_specs = [pl.BlockSpec((tm, d), lambda i, j: (i, 0), pipeline_mode=pl.Buffered(1) if one_tile else None),
                pl.BlockSpec((1, d), lambda i, j: (0, 0)),
                pl.BlockSpec((None, d, tn), lambda i, j: (0, 0, j)) if convert
                else pl.BlockSpec((d, tn), lambda i, j: (0, j), pipeline_mode=pl.Buffered(1) if n == tn else None)]
    out_shape = [jax.ShapeDtypeStruct((m, n), F32)]
    out_specs = [pl.BlockSpec((tm, tn), lambda i, j: (i, j))]
    args = [x, g, w]
    if gate is not None:
        wlr, wgate, gbias = gate
        in_specs += [pl.BlockSpec(wlr.shape, lambda i, j: (0, 0)),
                     pl.BlockSpec(wgate.shape, lambda i, j: (0, 0)),
                     pl.BlockSpec(gbias.shape, lambda i, j: (0, 0))]
        out_shape.append(jax.ShapeDtypeStruct((m, B_KEY_WIDTH), F32))
        out_specs.append(pl.BlockSpec((tm, B_KEY_WIDTH), lambda i, j: (i, 0)))
        args += [wlr, wgate, gbias]
    if convert:
        out_shape.append(jax.ShapeDtypeStruct((d, n), BF16))
        out_specs.append(pl.BlockSpec((d, tn), lambda i, j: (0, j)))
    res = pl.pallas_call(
        functools.partial(_inproj_kernel, with_gate=gate is not None, convert=convert),
        out_shape=out_shape, grid=grid, in_specs=in_specs, out_specs=out_specs,
        scratch_shapes=[pltpu.VMEM((tm, d), BF16)],
        compiler_params=_params(("parallel", "arbitrary")),
        name=("inproj_gate" if gate is not None else "inproj") + ("_convert" if convert else ""),
    )(*args)
    return res if len(res) > 1 else res[0]


def _sgu_kernel(u_ref, v_ref, gain_ref, w_ref, b_ref, o_ref, *av_ref, nblk):
    for n in range(nblk):
        rows = slice(n * A_BLOCK, (n + 1) * A_BLOCK)
        v = jax.nn.gelu(v_ref[rows, :])
        mu = jnp.mean(v, axis=-1, keepdims=True)
        vc = v - mu
        vn = vc * lax.rsqrt(jnp.mean(vc * vc, axis=-1, keepdims=True) + EPS) * gain_ref[...]
        if av_ref:
            av_ref[0][rows, :] = vn
        vb = vn.astype(BF16)
        for h in range(A_HEADS):
            cols = slice(h * LANES, (h + 1) * LANES)
            s = _dot(w_ref[h], vb[:, cols]) + b_ref[:, h:h + 1]
            o_ref[rows, cols] = (jax.nn.gelu(u_ref[rows, cols]) * s).astype(o_ref.dtype)


def _sgu(proj, gain, w, b, *, nblk, emit_av):
    m = proj.shape[0]
    tm = nblk * A_BLOCK
    n_out = 2 if emit_av else 1
    return pl.pallas_call(
        functools.partial(_sgu_kernel, nblk=nblk),
        out_shape=[jax.ShapeDtypeStruct((m, A_WIDTH), BF16), jax.ShapeDtypeStruct((m, A_WIDTH), F32)][:n_out],
        grid=(m // tm,),
        in_specs=[pl.BlockSpec((tm, A_WIDTH), lambda i: (i, 0)),
                  pl.BlockSpec((tm, A_WIDTH), lambda i: (i, 1)),
                  pl.BlockSpec((1, A_WIDTH), lambda i: (0, 0)),
                  pl.BlockSpec(w.shape, lambda i: (0, 0, 0)),
                  pl.BlockSpec(b.shape, lambda i: (0, 0))],
        out_specs=[pl.BlockSpec((tm, A_WIDTH), lambda i: (i, 0)),
                   pl.BlockSpec((tm, A_WIDTH), lambda i: (i, 0))][:n_out],
        compiler_params=_params(("parallel",)),
        name="sgu",
    )(proj, proj, gain, w, b)


def _gla_kernel(q_ref, k_ref, v_ref, r_ref, lg_ref, s0_ref, og_ref, o_ref, sout_ref, st_ref, *, clen, ngrp, chain):
    c = pl.program_id(1)
    rows_all = ngrp * clen
    shift = int(math.log2(clen))

    if chain:
        @pl.when(c == 0)
        def _():
            for h in range(B_HEADS):
                st_ref[h] = s0_ref[0, h].T

    row_i = lax.broadcasted_iota(jnp.int32, (rows_all, rows_all), 0)
    col_i = lax.broadcasted_iota(jnp.int32, (rows_all, rows_all), 1)
    same_group = lax.shift_right_logical(row_i, shift) == lax.shift_right_logical(col_i, shift)
    tri = ((row_i >= col_i) & same_group).astype(BF16)
    lg = lg_ref[...]
    lg_hi = lg.astype(BF16)
    rest = lg - lg_hi.astype(F32)
    lg_mid = rest.astype(BF16)
    lg_lo = (rest - lg_mid.astype(F32)).astype(BF16)
    cum = _dot(tri, lg_hi) + _dot(tri, lg_mid) + _dot(tri, lg_lo)
    tots = [cum[(g + 1) * clen - 1:(g + 1) * clen, :] for g in range(ngrp)]
    tot_rows = jnp.concatenate([jnp.broadcast_to(t, (clen, B_KEY_WIDTH)) for t in tots], axis=0)
    kd = (k_ref[...] * jnp.exp(tot_rows - cum)).astype(BF16)
    qs = (q_ref[...] * (B_KEY_DIM ** -0.5)).astype(BF16)
    vb = v_ref[...].astype(BF16)
    sr = jax.nn.silu(r_ref[...])
    grp = lax.shift_right_logical(lax.broadcasted_iota(jnp.int32, (rows_all, 1), 0), shift)
    zero = jnp.zeros((), BF16)

    def by_group(x):
        return jnp.concatenate([jnp.where(grp == g, x, zero) for g in range(ngrp)], axis=1)

    for h in range(B_HEADS):
        kc = slice(h * B_KEY_DIM, (h + 1) * B_KEY_DIM)
        vc = slice(h * B_VAL_DIM, (h + 1) * B_VAL_DIM)
        upd = lax.dot_general(vb[:, vc], by_group(kd[:, kc]), (((0,), (0,)), ((), ())),
                              preferred_element_type=F32)
        states = []
        st = st_ref[h] if chain else None
        for g in range(ngrp):
            if not chain:
                st = s0_ref[g, h].T
            st = jnp.exp(tots[g][:, kc]) * st + upd[:, g * B_KEY_DIM:(g + 1) * B_KEY_DIM]
            states.append(st.astype(BF16))
            if not chain:
                sout_ref[g, h] = st.T
        if chain:
            st_ref[h] = st
        o = lax.dot_general(by_group(qs[:, kc]), jnp.concatenate(states, axis=1), (((1,), (1,)), ((), ())),
                            preferred_element_type=F32)
        o = o * lax.rsqrt(jnp.mean(o * o, axis=-1, keepdims=True) + EPS)
        o = o * og_ref[:, vc] * sr[:, vc]
        o_ref[:, vc] = o.astype(o_ref.dtype)

    if chain:
        @pl.when(c == pl.num_programs(1) - 1)
        def _():
            for h in range(B_HEADS):
                sout_ref[0, h] = st_ref[h].T


def _gla(proj, lg, s0, og, *, nseq, t, clen, ngrp, chain):
    rows = clen * ngrp
    m = nseq * t
    if chain:
        steps = t // rows
        grid = (nseq, steps)
        rmap = lambda b, c: b * steps + c
        nstate = 1
    else:
        grid = (m // rows, 1)
        rmap = lambda b, c: b
        nstate = ngrp
    blk = lambda width, col: pl.BlockSpec((rows, width), lambda b, c: (rmap(b, c), col))
    state_spec = pl.BlockSpec((nstate, B_HEADS, B_KEY_DIM, B_VAL_DIM), lambda b, c: (b, 0, 0, 0))
    return pl.pallas_call(
        functools.partial(_gla_kernel, clen=clen, ngrp=ngrp, chain=chain),
        out_shape=[jax.ShapeDtypeStruct((m, B_WIDTH), BF16),
                   jax.ShapeDtypeStruct((nseq, B_HEADS, B_KEY_DIM, B_VAL_DIM), F32)],
        grid=grid,
        in_specs=[blk(B_KEY_WIDTH, 4),
                  blk(B_KEY_WIDTH, 5),
                  blk(B_WIDTH, 3),
                  blk(B_WIDTH, 4),
                  blk(B_KEY_WIDTH, 0),
                  state_spec,
                  pl.BlockSpec((1, B_WIDTH), lambda b, c: (0, 0))],
        out_specs=[blk(B_WIDTH, 0), state_spec],
        scratch_shapes=[pltpu.VMEM((B_HEADS, B_VAL_DIM, B_KEY_DIM), F32)],
        compiler_params=_params(("parallel", "arbitrary")),
        name="gla",
    )(proj, proj, proj, proj, lg, s0, og)


def _outproj_kernel(a_ref, b_ref, w_ref, x_ref, g_ref, o_ref, *w16_ref, convert):
    wb_ref = w_ref
    if convert:
        wb_ref, = w16_ref

        @pl.when(pl.program_id(0) == 0)
        def _():
            wb_ref[...] = w_ref[...].astype(BF16)

    ka = a_ref.shape[1]
    y = _dot(a_ref[...], wb_ref[0:ka, :]) + _dot(b_ref[...], wb_ref[ka:, :])
    o_ref[...] = x_ref[...] + _rms(y, g_ref[...])


def _outproj(a, b, w, x, g, *, tm, convert=False):
    m, d = x.shape
    k = a.shape[1] + b.shape[1]
    once = pl.Buffered(1)
    out_shape = [jax.ShapeDtypeStruct((m, d), F32)]
    out_specs = [pl.BlockSpec((tm, d), lambda i: (i, 0))]
    if convert:
        w_spec = pl.BlockSpec((None, k, d), lambda i: (0, 0, 0), pipeline_mode=once)
        out_shape.append(jax.ShapeDtypeStruct((k, d), BF16))
        out_specs.append(pl.BlockSpec((k, d), lambda i: (0, 0), pipeline_mode=once))
    else:
        w_spec = pl.BlockSpec((k, d), lambda i: (0, 0), pipeline_mode=once)
    res = pl.pallas_call(
        functools.partial(_outproj_kernel, convert=convert),
        out_shape=out_shape,
        grid=(m // tm,),
        in_specs=[pl.BlockSpec((tm, a.shape[1]), lambda i: (i, 0)),
                  pl.BlockSpec((tm, b.shape[1]), lambda i: (i, 0)),
                  w_spec,
                  pl.BlockSpec((tm, d), lambda i: (i, 0)),
                  pl.BlockSpec((1, d), lambda i: (0, 0))],
        out_specs=out_specs,
        compiler_params=_params(("arbitrary",)),
        name="outproj_convert" if convert else "outproj",
    )(a, b, w, x, g)
    return res if convert else res[0]


def _ffn_kernel(x_ref, gpre_ref, wg_ref, wv_ref, cw_ref, cb_ref, wd_ref, gpost_ref, st_ref,
                o_ref, tail_ref, *rest, step, hist, tps, tm, nf, convert):
    if convert:
        wg_out, wv_out, wd_out, h_ref, gext_ref, carry_ref, act_a, act_b = rest
    else:
        h_ref, gext_ref, carry_ref, act_a, act_b = rest
    i = pl.program_id(0)
    j = pl.program_id(1)
    first = (i % tps) == 0

    def up_and_gate(act_ref, hb=None):
        if hb is None:
            hb = h_ref[...]
        wg, wv = wg_ref[...], wv_ref[...]
        if convert:
            wg, wv = wg.astype(BF16), wv.astype(BF16)
            wg_out[...] = wg
            wv_out[...] = wv
        gate = _dot(hb, wg)
        val = _dot(hb, wv)
        cw = cw_ref[j]
        gext_ref[0:hist, :] = jnp.where(first, st_ref[0, j], carry_ref[j])
        gext_ref[hist:hist + tm, :] = gate
        prev2 = gext_ref[hist - 2 * step:hist - 2 * step + tm, :]
        prev1 = gext_ref[hist - step:hist - step + tm, :]
        conv = cb_ref[j] + cw[0:1, :] * prev2 + cw[1:2, :] * prev1 + cw[2:3, :] * gate
        act_ref[...] = (jax.nn.gelu(conv) * val).astype(BF16)
        tail = gate[tm - hist:, :]
        carry_ref[j] = tail
        tail_ref[0, j] = tail

    def down(act_ref):
        wd = wd_ref[...]
        if convert:
            wd = wd.astype(BF16)
            wd_out[...] = wd
        o_ref[...] += _dot(act_ref[...], wd)

    @pl.when(j == 0)
    def _():
        @pl.when(i == 0)
        def _():
            carry_ref[...] = jnp.zeros_like(carry_ref)

        o_ref[...] = jnp.zeros_like(o_ref)
        hb = _rms(x_ref[...], gpre_ref[...]).astype(BF16)
        h_ref[...] = hb
        up_and_gate(act_a, hb)

    for parity, (src, dst) in enumerate(((act_b, act_a), (act_a, act_b))):
        @pl.when((j > 0) & (j < nf) & (j % 2 == parity))
        def _(src=src, dst=dst):
            down(src)
            up_and_gate(dst)

    @pl.when(j == nf)
    def _():
        down(act_a if (nf - 1) % 2 == 0 else act_b)
        o_ref[...] = x_ref[...] + _rms(o_ref[...], gpost_ref[...])


def _ffn(x, gpre, weights, cw, cb, gpost, state, *, layer, step, hist, tps, tm, tf, convert):
    m, d = x.shape
    nf = D_FF // tf
    nm = m // tm
    up = lambda j: jnp.minimum(j, nf - 1)
    down = lambda j: jnp.maximum(j - 1, 0)

    def by_tile(a):
        a = a.reshape(a.shape[:-1] + (nf, tf))
        return jnp.swapaxes(a, -2, -3)

    cw, cb, state = by_tile(cw), by_tile(cb), by_tile(state)
    if convert:
        w_up, w_down = weights
        w_args = (w_up, w_up, w_down)
        w_specs = [pl.BlockSpec((None, d, tf), lambda i, j: (layer, 0, up(j))),
                   pl.BlockSpec((None, d, tf), lambda i, j: (layer, 0, nf + up(j))),
                   pl.BlockSpec((None, tf, d), lambda i, j: (layer, down(j), 0))]
        extra_shapes = [jax.ShapeDtypeStruct((d, D_FF), BF16), jax.ShapeDtypeStruct((d, D_FF), BF16),
                        jax.ShapeDtypeStruct((D_FF, d), BF16)]
        extra_specs = [pl.BlockSpec((d, tf), lambda i, j: (0, up(j))),
                       pl.BlockSpec((d, tf), lambda i, j: (0, up(j))),
                       pl.BlockSpec((tf, d), lambda i, j: (down(j), 0))]
    else:
        w_args = weights
        w_specs = [pl.BlockSpec((d, tf), lambda i, j: (0, up(j))),
                   pl.BlockSpec((d, tf), lambda i, j: (0, up(j))),
                   pl.BlockSpec((tf, d), lambda i, j: (down(j), 0))]
        extra_shapes, extra_specs = [], []
    res = pl.pallas_call(
        functools.partial(_ffn_kernel, step=step, hist=hist, tps=tps, tm=tm, nf=nf, convert=convert),
        out_shape=[jax.ShapeDtypeStruct((m, d), F32), jax.ShapeDtypeStruct((nm, nf, hist, tf), F32)] + extra_shapes,
        grid=(nm, nf + 1),
        in_specs=[pl.BlockSpec((tm, d), lambda i, j: (i, 0)),
                  pl.BlockSpec((None, 1, d), lambda i, j: (layer, 0, 0)),
                  w_specs[0], w_specs[1],
                  pl.BlockSpec((None, nf, 3, tf), lambda i, j: (layer, 0, 0, 0)),
                  pl.BlockSpec((None, nf, 1, tf), lambda i, j: (layer, 0, 0, 0)),
                  w_specs[2],
                  pl.BlockSpec((None, 1, d), lambda i, j: (layer, 0, 0)),
                  pl.BlockSpec((1, nf, hist, tf), lambda i, j: (i // tps, 0, 0, 0))],
        out_specs=[pl.BlockSpec((tm, d), lambda i, j: (i, 0), pipeline_mode=pl.Buffered(1)),
                   pl.BlockSpec((1, nf, hist, tf), lambda i, j: (i, 0, 0, 0))] + extra_specs,
        scratch_shapes=[pltpu.VMEM((tm, d), BF16),
                        pltpu.VMEM((hist + tm, tf), F32), pltpu.VMEM((nf, hist, tf), F32),
                        pltpu.VMEM((tm, tf), BF16), pltpu.VMEM((tm, tf), BF16)],
        compiler_params=_params(("arbitrary", "arbitrary")),
        name="ffn_convert" if convert else "ffn",
    )(x, gpre, w_args[0], w_args[1], cw, cb, w_args[2], gpost, state)
    res = list(res)
    res[1] = jnp.swapaxes(res[1], 1, 2).reshape(nm, hist, D_FF)
    return res


def _pool_kernel(c_ref, st_ref, cmap_ref, cs_ref, o_ref, tail_ref, ext_ref, carry_ref,
                 *, step, hist, tps, tm, pos0):
    i = pl.program_id(0)
    first = (i % tps) == 0

    @pl.when(first)
    def _():
        ext_ref[0:hist, :] = st_ref[0]

    @pl.when(jnp.logical_not(first))
    def _():
        ext_ref[0:hist, :] = carry_ref[...]

    ext_ref[hist:hist + tm, :] = c_ref[...]
    row = lax.broadcasted_iota(jnp.int32, (tm, 1), 0)
    if step > 1:
        row = lax.shift_right_logical(row, int(math.log2(step)))
    pos = pos0 + (i % tps) * (tm // step) + row
    for g, win in enumerate(POOL_WINDOWS):
        cols = slice(g * C_GROUP_DIM, (g + 1) * C_GROUP_DIM)
        acc = ext_ref[:, cols]
        d = 1
        while d < win:
            acc = acc + pltpu.roll(acc, d * step, 0)
            d *= 2
        tot = acc[hist:, :]
        cur = ext_ref[hist:hist + tm, cols]
        cnt = jnp.minimum(pos + 1, win).astype(F32)
        delta = tot / cnt - cur
        y = _dot(delta.astype(BF16), cmap_ref[g]) * cs_ref[:, cols]
        o_ref[:, cols] = y.astype(o_ref.dtype)
    tail = ext_ref[tm:tm + hist, :]
    carry_ref[...] = tail
    tail_ref[0] = tail


def _pool(proj, state, cmap, cscale, *, step, hist, tps, tm, pos0):
    m = proj.shape[0]
    nm = m // tm
    assert all(w & (w - 1) == 0 for w in POOL_WINDOWS) and hist >= (max(POOL_WINDOWS) - 1) * step and tm >= hist
    return pl.pallas_call(
        functools.partial(_pool_kernel, step=step, hist=hist, tps=tps, tm=tm, pos0=pos0),
        out_shape=[jax.ShapeDtypeStruct((m, C_WIDTH), BF16), jax.ShapeDtypeStruct((nm, hist, C_WIDTH), F32)],
        grid=(nm,),
        in_specs=[pl.BlockSpec((tm, C_WIDTH), lambda i: (i, 0)),
                  pl.BlockSpec((1, hist, C_WIDTH), lambda i: (i // tps, 0, 0)),
                  pl.BlockSpec(cmap.shape, lambda i: (0, 0, 0)),
                  pl.BlockSpec((1, C_WIDTH), lambda i: (0, 0))],
        out_specs=[pl.BlockSpec((tm, C_WIDTH), lambda i: (i, 0)),
                   pl.BlockSpec((1, hist, C_WIDTH), lambda i: (i, 0, 0))],
        scratch_shapes=[pltpu.VMEM((hist + tm, C_WIDTH), F32), pltpu.VMEM((hist, C_WIDTH), F32)],
        compiler_params=_params(("arbitrary",)),
        name="pool",
    )(proj, state, cmap, cscale)


S5_CHUNK = 512
S5_NCHUNK = S5_CH // S5_CHUNK
S5_FOLD = SUBLANES // 2


def _s5_kernel(u_ref, s0re_ref, s0im_ref, bblk_ref, cblk_ref, dskip_ref, wglu_ref, kc_ref,
               o_ref, tre_ref, tim_ref, cre_ref, cim_ref, *, step, tps, tm):
    i = pl.program_id(0)
    crow = cre_ref.shape[0]

    @pl.when((i % tps) == 0)
    def _():
        cre_ref[...] = jnp.broadcast_to(s0re_ref[0], cre_ref.shape) if step == 1 else s0re_ref[0]
        cim_ref[...] = jnp.broadcast_to(s0im_ref[0], cim_ref.shape) if step == 1 else s0im_ref[0]

    u = u_ref[...]
    lhs = [u.astype(BF16)]
    if step == 1:
        row_in_block = lax.broadcasted_iota(jnp.int32, (tm, 1), 0) & (SUBLANES - 1)
        for k in range(1, S5_FOLD):
            lhs.append(jnp.where(row_in_block >= k, pltpu.roll(u, k, 0), 0.0).astype(BF16))
    ys = []
    for m in range(S5_NCHUNK):
        cols = slice(m * S5_CHUNK, (m + 1) * S5_CHUNK)
        ucols = slice(m * LANES, (m + 1) * LANES)
        if step == 1:
            r = _dot(jnp.concatenate([x[:, ucols] for x in lhs], axis=1), bblk_ref[m])
        else:
            r = _dot(lhs[0][:, ucols], bblk_ref[m, 0:LANES, :])
        cr, ci = cre_ref[:, cols], cim_ref[:, cols]
        sre, sim = [], []
        if step == 1:
            ar, ai, pwr, pwi = [kc_ref[k, :, cols] for k in range(4)]
            for rb in range(tm // SUBLANES):
                rows = slice(rb * SUBLANES, (rb + 1) * SUBLANES)
                xr = r[rows, :S5_CHUNK]
                xi = r[rows, S5_CHUNK:]
                sr = pltpu.roll(xr, S5_FOLD, 0)
                si = pltpu.roll(xi, S5_FOLD, 0)
                xr, xi = xr + (ar * sr - ai * si), xi + (ar * si + ai * sr)
                xr, xi = xr + (pwr * cr - pwi * ci), xi + (pwr * ci + pwi * cr)
                sre.append(xr)
                sim.append(xi)
                cr = jnp.broadcast_to(xr[SUBLANES - 1:SUBLANES, :], xr.shape)
                ci = jnp.broadcast_to(xi[SUBLANES - 1:SUBLANES, :], xi.shape)
        else:
            lr = jnp.broadcast_to(kc_ref[0, 0:1, cols], (crow, S5_CHUNK))
            li = jnp.broadcast_to(kc_ref[1, 0:1, cols], (crow, S5_CHUNK))
            for t in range(tm // step):
                rows = slice(t * step, (t + 1) * step)
                cr, ci = (r[rows, :S5_CHUNK] + (lr * cr - li * ci),
                          r[rows, S5_CHUNK:] + (lr * ci + li * cr))
                sre.append(cr)
                sim.append(ci)
        cre_ref[:, cols] = cr
        cim_ref[:, cols] = ci
        ys.append(_dot(jnp.concatenate(sre, axis=0).astype(BF16), cblk_ref[m, 0:S5_CHUNK, :])
                  + _dot(jnp.concatenate(sim, axis=0).astype(BF16), cblk_ref[m, S5_CHUNK:, :]))

    tre_ref[0] = cre_ref[...]
    tim_ref[0] = cim_ref[...]

    y = jnp.concatenate(ys, axis=1) + dskip_ref[...] * u
    z = _dot(jax.nn.gelu(y).astype(BF16), wglu_ref[...])
    o_ref[...] = (z[:, :D_WIDTH] * jax.nn.sigmoid(z[:, D_WIDTH:])).astype(o_ref.dtype)


def _s5(proj, s0re, s0im, bblk, cblk, dskip, wglu, kconst, *, step, tps, tm):
    m = proj.shape[0]
    nm = m // tm
    crow = s0re.shape[1] if step > 1 else SUBLANES
    srow = s0re.shape[1]
    return pl.pallas_call(
        functools.partial(_s5_kernel, step=step, tps=tps, tm=tm),
        out_shape=[jax.ShapeDtypeStruct((m, D_WIDTH), BF16),
                   jax.ShapeDtypeStruct((nm, crow, S5_CH), F32),
                   jax.ShapeDtypeStruct((nm, crow, S5_CH), F32)],
        grid=(nm,),
        in_specs=[pl.BlockSpec((tm, D_WIDTH), lambda i: (i, 1)),
                  pl.BlockSpec((1, srow, S5_CH), lambda i: (i // tps, 0, 0)),
                  pl.BlockSpec((1, srow, S5_CH), lambda i: (i // tps, 0, 0)),
                  pl.BlockSpec(bblk.shape, lambda i: (0, 0, 0), pipeline_mode=pl.Buffered(1)),
                  pl.BlockSpec(cblk.shape, lambda i: (0, 0, 0), pipeline_mode=pl.Buffered(1)),
                  pl.BlockSpec((1, D_WIDTH), lambda i: (0, 0)),
                  pl.BlockSpec(wglu.shape, lambda i: (0, 0), pipeline_mode=pl.Buffered(1)),
                  pl.BlockSpec(kconst.shape, lambda i: (0, 0, 0), pipeline_mode=pl.Buffered(1))],
        out_specs=[pl.BlockSpec((tm, D_WIDTH), lambda i: (i, 0)),
                   pl.BlockSpec((1, crow, S5_CH), lambda i: (i, 0, 0)),
                   pl.BlockSpec((1, crow, S5_CH), lambda i: (i, 0, 0))],
        scratch_shapes=[pltpu.VMEM((crow, S5_CH), F32), pltpu.VMEM((crow, S5_CH), F32)],
        compiler_params=_params(("arbitrary",)),
        name="s5",
    )(proj, s0re, s0im, bblk, cblk, dskip, wglu, kconst)


def _s5_constants(a_re, a_im, log_dt, b_re, b_im, c_re, c_im):
    per = S5_CHUNK // S5_STATE
    nchunk = S5_GROUPS // per
    on_diag = jnp.arange(per)[:, None] == jnp.arange(per)[None, :]
    zero = jnp.zeros((), BF16)
    dt = jnp.exp(log_dt)[:, None]
    zr, zi = a_re * dt, a_im * dt
    mag = jnp.exp(zr)
    lr, li = mag * jnp.cos(zi), mag * jnp.sin(zi)
    den = a_re * a_re + a_im * a_im
    nr, ni = lr - 1.0, li
    kr, ki = (nr * a_re + ni * a_im) / den, (ni * a_re - nr * a_im) / den
    bbr = kr[..., None] * b_re - ki[..., None] * b_im
    bbi = kr[..., None] * b_im + ki[..., None] * b_re
    qr, qi = [jnp.ones_like(lr)], [jnp.zeros_like(li)]
    for _ in range(S5_FOLD - 1):
        qr, qi = qr + [qr[-1] * lr - qi[-1] * li], qi + [qr[-1] * li + qi[-1] * lr]
    qr, qi = jnp.stack(qr)[..., None], jnp.stack(qi)[..., None]
    f = jnp.stack([qr * bbr - qi * bbi, qr * bbi + qi * bbr])
    f = f.reshape(2, S5_FOLD, nchunk, per, S5_STATE, S5_GROUP_DIM).transpose(2, 1, 3, 5, 0, 4).astype(BF16)
    f = f.reshape(nchunk, S5_FOLD * LANES, 2 * S5_STATE)
    rows = S5_FOLD * LANES
    src = jnp.arange(2 * S5_STATE)[:, None]
    dst = jnp.arange(2 * S5_CHUNK)[None, :]
    spread = ((src // S5_STATE == dst // S5_CHUNK) & (src % S5_STATE == dst % S5_STATE)).astype(BF16)
    row_group = (jnp.arange(rows) % LANES) // S5_GROUP_DIM
    col_group = (jnp.arange(2 * S5_CHUNK) % S5_CHUNK) // S5_STATE
    bblk = jnp.dot(f.reshape(nchunk * rows, 2 * S5_STATE), spread, preferred_element_type=BF16)
    bblk = jnp.where(row_group[:, None] == col_group[None, :], bblk.reshape(nchunk, rows, 2 * S5_CHUNK), zero)
    c = jnp.stack([c_re, -c_im]).reshape(2, nchunk, per, S5_GROUP_DIM, S5_STATE).transpose(1, 0, 2, 4, 3).astype(BF16)
    cblk = jnp.where(on_diag[None, None, :, None, :, None], c[:, :, :, :, None, :], zero)
    cblk = cblk.reshape(nchunk, 2 * S5_CHUNK, LANES)
    lr, li = lr.reshape(1, S5_CH), li.reshape(1, S5_CH)
    pr, pi = [lr], [li]
    for _ in range(SUBLANES - 1):
        pr, pi = pr + [pr[-1] * lr - pi[-1] * li], pi + [pr[-1] * li + pi[-1] * lr]
    rowid = jnp.arange(SUBLANES)[:, None]

    def masked(p, d):
        return jnp.where(rowid >= d, jnp.broadcast_to(p[d - 1], (SUBLANES, S5_CH)), 0.0)

    k_prompt = jnp.stack([masked(pr, S5_FOLD), masked(pi, S5_FOLD),
                          jnp.concatenate(pr, axis=0), jnp.concatenate(pi, axis=0)])
    k_sample = jnp.stack([jnp.broadcast_to(lr, (SUBLANES, S5_CH)), jnp.broadcast_to(li, (SUBLANES, S5_CH))])
    return bblk, cblk, k_prompt, k_sample


def _time_major(a):
    a = jnp.swapaxes(a, 0, 1)
    return a.reshape((a.shape[0] * a.shape[1],) + a.shape[2:])


def kernel(x_prompt, x_sample, state_gla, state_pool, state_s5_re, state_s5_im, state_ffn_conv, norm_mix_pre, norm_mix_post, norm_ffn_pre, norm_ffn_post, w_in_even, a_w_s, a_b_s, a_v_norm, b_w_gate, b_gate_bias, b_out_norm, w_out_even, w_in_odd, c_map, c_scale, s5_a_re, s5_a_im, s5_log_dt, s5_b_re, s5_b_im, s5_c_re, s5_c_im, s5_d, s5_w_glu, w_out_odd, ffn_w_up, ffn_conv_w, ffn_conv_b, ffn_w_down):
    bp = x_prompt.shape[0]
    nb, ts = x_sample.shape[0], x_sample.shape[1]
    xp = x_prompt.reshape(bp * SEQ, D_MODEL)
    xs = x_sample.reshape(nb * ts, D_MODEL)

    row = lambda v: v.reshape(1, -1)
    n_main = 2 * A_WIDTH + 2 * B_KEY_WIDTH + 2 * B_WIDTH
    w_in0 = w_in_even[0].astype(BF16)
    w_lr = jnp.pad(w_in0[:, n_main:], ((0, 0), (0, LANES - B_GATE_RANK)))
    w_gate = jnp.pad(b_w_gate[0], ((0, LANES - B_GATE_RANK), (0, 0))).astype(BF16)
    gate = (w_lr, w_gate, row(b_gate_bias[0]))
    pos = jnp.arange(A_BLOCK)
    causal = (pos[None, :] // CHUNK) <= (pos[:, None] // CHUNK)
    ws_prompt = jnp.where(causal[None], a_w_s[0], 0.0).astype(BF16)
    per = A_BLOCK // ts
    ws_small = jnp.where(causal[None, :ts, :ts], a_w_s[0][:, :ts, :ts], 0.0)
    ws_sample = jnp.einsum('hij,ab->haibj', ws_small, jnp.eye(per, dtype=F32)).reshape(A_HEADS, A_BLOCK, A_BLOCK).astype(BF16)
    bs_prompt = a_b_s[0].T
    bs_sample = jnp.tile(a_b_s[0][:, :ts].T, (per, 1))
    cmap = c_map[0].astype(BF16)
    bblk, cblk, k_prompt, k_sample = _s5_constants(s5_a_re[0], s5_a_im[0], s5_log_dt[0], s5_b_re[0], s5_b_im[0],
                                                   s5_c_re[0], s5_c_im[0])
    wglu = s5_w_glu[0].astype(BF16)

    tm = 512
    tf = 512
    tps_p = SEQ // tm
    ffn_hist_p = SUBLANES
    pool_hist_p = 2 * SUBLANES
    step_s = nb
    tps_s = (nb * ts) // tm
    ffn_hist_s = 2 * step_s
    pool_hist_s = (POOL_BUF + 1) * step_s

    tm_ffn = 1024
    tps_ffn_p = SEQ // tm_ffn
    tps_ffn_s = (nb * ts) // tm_ffn

    ffn_w16 = {}

    def ffn_layer(x, layer, state, *, step, hist, tps, convert):
        weights = (ffn_w_up, ffn_w_down) if convert else ffn_w16[layer]
        res = _ffn(x, norm_ffn_pre[:, None], weights, ffn_conv_w, ffn_conv_b[:, None], norm_ffn_post[:, None],
                   state, layer=layer, step=step, hist=hist, tps=tps, tm=tm_ffn, tf=256 if convert else tf,
                   convert=convert)
        if convert:
            ffn_w16[layer] = tuple(res[2:])
        return res[0], res[1]

    proj, lg = _inproj(xs, row(norm_mix_pre[0]), w_in0, gate, n=n_main, tm=1024, tn=1024)
    a_out, a_v = _sgu(proj, row(a_v_norm[0]), ws_sample, bs_sample, nblk=2, emit_av=True)
    b_out, gla_s = _gla(proj, lg, state_gla[0], row(b_out_norm[0]), nseq=nb, t=ts, clen=ts, ngrp=8, chain=False)
    xs, w_out0 = _outproj(a_out, b_out, w_out_even, xs, row(norm_mix_post[0]), tm=tm, convert=True)
    xs = _time_major(xs.reshape(nb, ts, D_MODEL))
    ffn_state = lambda layer: _time_major(state_ffn_conv[layer])[None]
    xs, ffn0_s = ffn_layer(xs, 0, ffn_state(0), step=step_s, hist=ffn_hist_s, tps=tps_ffn_s, convert=True)
    proj, w_in1 = _inproj(xs, row(norm_mix_pre[1]), w_in_odd, n=D_MODEL, tm=1024, tn=512, convert=True)
    pool_state = jnp.pad(_time_major(state_pool[0]), ((step_s, 0), (0, 0)))[None]
    c_out, pool_tail_s = _pool(proj, pool_state, cmap, row(c_scale[0]),
                               step=step_s, hist=pool_hist_s, tps=tps_s, tm=tm, pos0=PAST_LEN)
    d_out, s5re_tail_s, s5im_tail_s = _s5(proj, state_s5_re[0].reshape(1, nb, S5_CH),
                                          state_s5_im[0].reshape(1, nb, S5_CH),
                                          bblk, cblk, row(s5_d[0]), wglu, k_sample, step=step_s, tps=tps_s, tm=tm)
    xs, w_out1 = _outproj(c_out, d_out, w_out_odd, xs, row(norm_mix_post[1]), tm=tm, convert=True)
    xs, ffn1_s = ffn_layer(xs, 1, ffn_state(1), step=step_s, hist=ffn_hist_s, tps=tps_ffn_s, convert=True)

    proj, lg = _inproj(xp, row(norm_mix_pre[0]), w_in0, gate, n=n_main, tm=1024, tn=1280)
    a_out = _sgu(proj, row(a_v_norm[0]), ws_prompt, bs_prompt, nblk=4, emit_av=False)[0]
    b_out, gla_p = _gla(proj, lg, jnp.zeros((bp, B_HEADS, B_KEY_DIM, B_VAL_DIM), F32), row(b_out_norm[0]),
                        nseq=bp, t=SEQ, clen=CHUNK, ngrp=4, chain=True)
    xp = _outproj(a_out, b_out, w_out0, xp, row(norm_mix_post[0]), tm=tm)
    xp, ffn0_p = ffn_layer(xp, 0, jnp.zeros((bp, ffn_hist_p, D_FF), F32), step=1, hist=ffn_hist_p,
                           tps=tps_ffn_p, convert=False)
    proj = _inproj(xp, row(norm_mix_pre[1]), w_in1, n=D_MODEL, tm=1024, tn=D_MODEL)
    c_out, pool_tail_p = _pool(proj, jnp.zeros((bp, pool_hist_p, C_WIDTH), F32), cmap, row(c_scale[0]),
                               step=1, hist=pool_hist_p, tps=tps_p, tm=tm, pos0=0)
    zero_state = jnp.zeros((bp, 1, S5_CH), F32)
    d_out, s5re_tail_p, s5im_tail_p = _s5(proj, zero_state, zero_state, bblk, cblk, row(s5_d[0]), wglu, k_prompt,
                                          step=1, tps=tps_ffn_p, tm=tm_ffn)
    xp = _outproj(c_out, d_out, w_out1, xp, row(norm_mix_post[1]), tm=tm)
    xp, ffn1_p = ffn_layer(xp, 1, jnp.zeros((bp, ffn_hist_p, D_FF), F32), step=1, hist=ffn_hist_p,
                           tps=tps_ffn_p, convert=False)

    last = slice(tps_p - 1, None, tps_p)
    y_prompt = xp.reshape(bp, SEQ, D_MODEL)
    gla_prompt = gla_p[None]
    pool_prompt = pool_tail_p[last, pool_hist_p - POOL_BUF:][None]
    last_ffn = slice(tps_ffn_p - 1, None, tps_ffn_p)
    s5_re_prompt = s5re_tail_p[last_ffn, 0].reshape(1, bp, S5_GROUPS, S5_STATE)
    s5_im_prompt = s5im_tail_p[last_ffn, 0].reshape(1, bp, S5_GROUPS, S5_STATE)
    ffn_prompt = jnp.stack([ffn0_p[last_ffn, ffn_hist_p - 2:], ffn1_p[last_ffn, ffn_hist_p - 2:]])

    def batch_major(a, nt):
        return jnp.swapaxes(a.reshape(nt, nb, a.shape[-1]), 0, 1)

    y_sample = batch_major(xs, ts)
    gla_sample = gla_s[None]
    av_sample = a_v.reshape(1, nb, ts, A_WIDTH)
    pool_sample = batch_major(pool_tail_s[-1, step_s:], POOL_BUF)[None]
    s5_re_sample = s5re_tail_s[-1].reshape(1, nb, S5_GROUPS, S5_STATE)
    s5_im_sample = s5im_tail_s[-1].reshape(1, nb, S5_GROUPS, S5_STATE)
    ffn_sample = jnp.stack([batch_major(ffn0_s[-1], 2), batch_major(ffn1_s[-1], 2)])

    return (y_prompt, y_sample, gla_prompt, gla_sample, av_sample, pool_prompt, pool_sample,
            s5_re_prompt, s5_im_prompt, s5_re_sample, s5_im_sample, ffn_prompt, ffn_sample)
```
